```python
import math, functools
import jax, jax.numpy as jnp
from jax import lax
import numpy as np

D_MODEL = 1024
BATCH = 8
SEQ = 4096
DEPTH = 2

GRID_W = 64
CTX_LEN = 256
D_FF = 2816
DIFF_HEADS = 4
DIFF_HEAD_DIM = 64
DIFF_V_DIM = 2 * DIFF_HEAD_DIM
WIN_HEADS = 8
WIN_KV_HEADS = 2
WIN_HEAD_DIM = 64
WINDOW = 128
Q_BLOCK = 128
ROPE_BASE = 10000.0
ROPE_FREQS = DIFF_HEAD_DIM // 4
MLSTM_HEADS = 4
MLSTM_HEAD_DIM = 128
RET_HEADS = 4
RET_QK_DIM = 64
RET_V_DIM = 128
CHUNK = 128
ALPHA = (2.0 * DEPTH) ** 0.25
BETA = (8.0 * DEPTH) ** -0.25
N_EVEN = (DEPTH + 1) // 2
N_ODD = DEPTH // 2

ATTN_SIZES = (DIFF_HEADS * 2 * DIFF_HEAD_DIM, DIFF_HEADS * 2 * DIFF_HEAD_DIM, DIFF_HEADS * DIFF_V_DIM,
              WIN_HEADS * WIN_HEAD_DIM, WIN_KV_HEADS * WIN_HEAD_DIM, WIN_KV_HEADS * WIN_HEAD_DIM)
ATTN_IN = sum(ATTN_SIZES)
ATTN_OUT = DIFF_HEADS * DIFF_V_DIM + WIN_HEADS * WIN_HEAD_DIM
REC_SIZES = (MLSTM_HEADS * MLSTM_HEAD_DIM, MLSTM_HEADS * MLSTM_HEAD_DIM, MLSTM_HEADS * MLSTM_HEAD_DIM,
             MLSTM_HEADS * MLSTM_HEAD_DIM, 2 * 2 * MLSTM_HEADS,
             RET_HEADS * RET_QK_DIM, RET_HEADS * RET_QK_DIM, RET_HEADS * RET_V_DIM, RET_HEADS * RET_V_DIM)
REC_IN = sum(REC_SIZES)
REC_OUT = MLSTM_HEADS * MLSTM_HEAD_DIM + RET_HEADS * RET_V_DIM

kernel_name = 'hybrid_diffattn_swa_mlstm_retention_dit'


def layer_norm(x, g, b=None, eps=1e-5):
    xf = x.astype(jnp.float32)
    mu = xf.mean(-1, keepdims=True)
    var = jnp.square(xf - mu).mean(-1, keepdims=True)
    y = (xf - mu) * lax.rsqrt(var + eps) * g
    if b is not None:
        y = y + b
    return y.astype(x.dtype)


def rms_norm(x, g, eps=1e-5):
    xf = x.astype(jnp.float32)
    y = xf * lax.rsqrt(jnp.square(xf).mean(-1, keepdims=True) + eps) * g
    return y.astype(x.dtype)


def split_cols(p, sizes):
    out, start = [], 0
    for s in sizes:
        out.append(p[..., start:start + s])
        start += s
    return out


def modulation(cvec, w, b):
    return jax.nn.silu(cvec) @ w + b


def modulate(x, m, s):
    return x * (1.0 + m[:, :, s, 1]) + m[:, :, s, 0]


def post_norm_residual(x, y, m, s, weight, g, b):
    return layer_norm(ALPHA * x + weight * m[:, :, s, 2] * y, g, b)


def swiglu(h, w_in, w_out):
    gate, up = jnp.split(h @ w_in, 2, axis=-1)
    return (jax.nn.silu(gate) * up) @ w_out


def ffn_sublayer(stream, m, s, w_in, w_out, g, b):
    return post_norm_residual(stream, swiglu(modulate(stream, m, s), w_in, w_out), m, s, 0.5, g, b)


def axial_rope_tables(n_tokens):
    ROWS = n_tokens // GRID_W
    row = jnp.repeat(jnp.arange(ROWS), GRID_W)
    col = jnp.tile(jnp.arange(GRID_W), ROWS)
    inv = ROPE_BASE ** (-jnp.arange(ROPE_FREQS, dtype=jnp.float32) / ROPE_FREQS)
    ang = jnp.stack([row[:, None] * inv, col[:, None] * inv], axis=1)
    return jnp.cos(ang), jnp.sin(ang)


def apply_rope(x, cos, sin):
    mid = x.shape[2:-1]
    xs = x.reshape(*x.shape[:-1], 2, 2, ROPE_FREQS)
    x1, x2 = xs[..., 0, :], xs[..., 1, :]
    c = cos.reshape(cos.shape[0], *([1] * len(mid)), 2, ROPE_FREQS)
    s = sin.reshape(sin.shape[0], *([1] * len(mid)), 2, ROPE_FREQS)
    out = jnp.stack([x1 * c - x2 * s, x2 * c + x1 * s], axis=-2)
    return out.reshape(x.shape)


def to_blocks(a):
    B, T = a.shape[:2]
    return a.reshape(B, T // Q_BLOCK, Q_BLOCK, *a.shape[2:]).swapaxes(0, 1)


def from_blocks(a):
    nb, B, Q = a.shape[:3]
    return a.swapaxes(0, 1).reshape(B, nb * Q, *a.shape[3:])


def to_chunks(a):
    B, T = a.shape[:2]
    return jnp.moveaxis(a.reshape(B, T // CHUNK, CHUNK, *a.shape[2:]), (1, 3), (0, 2))


def from_chunks(y):
    y = jnp.moveaxis(y, (0, 2), (1, 3))
    return y.reshape(y.shape[0], y.shape[1] * y.shape[2], *y.shape[3:])


def diff_attention(q, k, v, lam):
    s = jnp.einsum('bqhid,bkhid->bhiqk', q, k).astype(jnp.float32) * (q.shape[-1] ** -0.5)
    p = jax.nn.softmax(s, axis=-1)
    a = p[:, :, 0] - lam * p[:, :, 1]
    return jnp.einsum('bhqk,bkhv->bqhv', a, v)


def sink_attention(q, keys, values, masks, sink):
    B, Q, H, d = q.shape
    KVH = keys[0].shape[2]
    G = H // KVH
    qg = q.reshape(B, Q, KVH, G, d) * (d ** -0.5)
    scores = []
    for k, m in zip(keys, masks):
        s = jnp.einsum('bqkgd,bskd->bkgqs', qg, k).astype(jnp.float32)
        scores.append(s if m is None else jnp.where(m, s, -jnp.inf))
    sink_col = jnp.broadcast_to(sink.astype(jnp.float32).reshape(1, KVH, G, 1, 1), (B, KVH, G, Q, 1))
    p = jax.nn.softmax(jnp.concatenate(scores + [sink_col], axis=-1), axis=-1)
    out, start = None, 0
    for v in values:
        n = v.shape[1]
        part = jnp.einsum('bkgqs,bskd->bqkgd', p[..., start:start + n], v)
        out = part if out is None else out + part
        start += n
    return out.reshape(B, Q, H, d)


def split_attn(p):
    B, T, _ = p.shape
    aq, ak, av, bq, bk, bv = split_cols(p, ATTN_SIZES)
    return (aq.reshape(B, T, DIFF_HEADS, 2, DIFF_HEAD_DIM), ak.reshape(B, T, DIFF_HEADS, 2, DIFF_HEAD_DIM),
            av.reshape(B, T, DIFF_HEADS, DIFF_V_DIM), bq.reshape(B, T, WIN_HEADS, WIN_HEAD_DIM),
            bk.reshape(B, T, WIN_KV_HEADS, WIN_HEAD_DIM), bv.reshape(B, T, WIN_KV_HEADS, WIN_HEAD_DIM))


def attention_mixer(hx, hc, w_in, w_out, lam_vec, sub_g, sink, lambda_init, need_ctx_out):
    B, T, _ = hx.shape
    cos, sin = axial_rope_tables(T)
    aqx, akx, avx, bqx, bkx, bvx = split_attn(hx @ w_in)
    aqc, akc, avc, bqc, bkc, bvc = split_attn(hc @ w_in)
    lam = (jnp.exp(jnp.sum(lam_vec[0] * lam_vec[1])) - jnp.exp(jnp.sum(lam_vec[2] * lam_vec[3]))
           + lambda_init).astype(jnp.float32)
    k_all = jnp.concatenate([apply_rope(akx, cos, sin), akc], axis=1)
    v_all = jnp.concatenate([avx, avc], axis=1)
    a_x = from_blocks(lax.map(lambda qb: diff_attention(qb, k_all, v_all, lam),
                              to_blocks(apply_rope(aqx, cos, sin))))
    pad = ((0, 0), (WINDOW, WINDOW), (0, 0), (0, 0))
    k_pad = jnp.pad(apply_rope(bkx, cos, sin), pad)
    v_pad = jnp.pad(bvx, pad)
    span = Q_BLOCK + 2 * WINDOW
    q_pos = jnp.arange(Q_BLOCK)
    k_off = jnp.arange(span) - WINDOW

    def window_block(args):
        i, qb = args
        start = i * Q_BLOCK
        kw = lax.dynamic_slice_in_dim(k_pad, start, span, axis=1)
        vw = lax.dynamic_slice_in_dim(v_pad, start, span, axis=1)
        j = start + k_off
        t = start + q_pos
        mask = (j[None, :] >= 0) & (j[None, :] < T) & (jnp.abs(t[:, None] - j[None, :]) <= WINDOW)
        return sink_attention(qb, (kw, bkc), (vw, bvc), (mask, None), sink)

    b_x = from_blocks(lax.map(window_block, (jnp.arange(T // Q_BLOCK), to_blocks(apply_rope(bqx, cos, sin)))))

    def merge(a, b_out):
        Bm, Tm = a.shape[:2]
        a = rms_norm(a, sub_g) * (1.0 - lambda_init)
        return jnp.concatenate([a.reshape(Bm, Tm, -1), b_out.reshape(Bm, Tm, -1)], axis=-1) @ w_out

    y_x = merge(a_x, b_x)
    y_c = None
    if need_ctx_out:
        a_c = diff_attention(aqc, akc, avc, lam)
        b_c = sink_attention(bqc, (bkc,), (bvc,), (None,), sink)
        y_c = merge(a_c, b_c)
    return y_x, y_c


def mlstm_scan(seqs, state, with_output):
    q, k, v, i_pre, log_f = seqs
    tril = jnp.tril(jnp.ones((CHUNK, CHUNK), dtype=bool))

    def step(carry, xs):
        C, n, m = carry
        qc, kc, vc, ic, fc = xs
        b = jnp.cumsum(fc, axis=-1)
        b_end = b[..., -1]
        w_end = b_end[..., None] - b + ic
        m_new = jnp.maximum(b_end + m, w_end.max(-1))
        decay_state = jnp.exp(b_end + m - m_new)
        w = jnp.exp(w_end - m_new[..., None])
        C_new = decay_state[..., None, None] * C + jnp.einsum('bhl,bhlk,bhlv->bhkv', w, kc, vc)
        n_new = decay_state[..., None] * n + jnp.einsum('bhl,bhlk->bhk', w, kc)
        if not with_output:
            return (C_new, n_new, m_new), None
        log_d = jnp.where(tril, b[..., :, None] - b[..., None, :] + ic[..., None, :], -jnp.inf)
        log_inter = b + m[..., None]
        m_t = jnp.maximum(log_inter, log_d.max(-1))
        s = jnp.einsum('bhqd,bhkd->bhqk', qc, kc) * jnp.exp(log_d - m_t[..., None])
        w_inter = jnp.exp(log_inter - m_t)
        num = jnp.einsum('bhqk,bhkv->bhqv', s, vc) + w_inter[..., None] * jnp.einsum('bhqd,bhdv->bhqv', qc, C)
        den = s.sum(-1) + w_inter * jnp.einsum('bhqd,bhd->bhq', qc, n)
        h = num / jnp.maximum(jnp.abs(den), jnp.exp(-m_t))[..., None]
        return (C_new, n_new, m_new), h

    state, ys = lax.scan(step, state, tuple(to_chunks(a) for a in seqs))
    return (from_chunks(ys) if with_output else None), state


def retention_scan(log_gamma, seqs, state, with_output):
    q, k, v = seqs
    pos = jnp.arange(CHUNK, dtype=jnp.float32)
    g = log_gamma[:, None]
    decay_to_end = jnp.exp(g * (CHUNK - 1 - pos))
    decay_chunk = jnp.exp(log_gamma * CHUNK)
    decay_from_start = jnp.exp(g * (pos + 1.0))
    rel = pos[:, None] - pos[None, :]
    decay_intra = jnp.where(rel >= 0, jnp.exp(log_gamma[:, None, None] * jnp.maximum(rel, 0.0)), 0.0)

    def step(S, xs):
        qc, kc, vc = xs
        S_new = decay_chunk[None, :, None, None] * S + jnp.einsum('hl,bhlk,bhlv->bhkv', decay_to_end, kc, vc)
        if not with_output:
            return S_new, None
        s = jnp.einsum('bhqd,bhkd->bhqk', qc, kc) * decay_intra
        o = (jnp.einsum('bhqk,bhkv->bhqv', s, vc)
             + decay_from_start[None, :, :, None] * jnp.einsum('bhqd,bhdv->bhqv', qc, S))
        return S_new, o

    state, ys = lax.scan(step, state, tuple(to_chunks(a) for a in seqs))
    return (from_chunks(ys) if with_output else None), state


def run_bidirectional(scans, lat_seqs, ctx_seqs, state0, need_ctx_out):
    lat_outs, ctx_outs = [], []
    for d in range(2):
        flip = (lambda a: jnp.flip(a, axis=1)) if d == 1 else (lambda a: a)
        ctx_h, ctx_state = scans[d](tuple(flip(a) for a in ctx_seqs[d]), state0, need_ctx_out)
        lat_h, _ = scans[d](tuple(flip(a) for a in lat_seqs[d]), ctx_state, True)
        lat_outs.append(flip(lat_h))
        if need_ctx_out:
            ctx_outs.append(flip(ctx_h))
    ctx_sum = ctx_outs[0] + ctx_outs[1] if need_ctx_out else None
    return lat_outs[0] + lat_outs[1], ctx_sum


def recurrent_mixer(hx, hc, w_in, w_out, gate_b, m_norm_g, ret_logit, ret_g, ret_b, need_ctx_out):
    H, d = MLSTM_HEADS, MLSTM_HEAD_DIM

    def prep(p):
        B, T, _ = p.shape
        mq, mk, mv, mo, mg, rq, rk, rv, rg = split_cols(p, REC_SIZES)
        gates = (mg.reshape(B, T, 2, 2, H) + gate_b).astype(jnp.float32)
        q, k, v = mq.reshape(B, T, H, d), mk.reshape(B, T, H, d) * (d ** -0.5), mv.reshape(B, T, H, d)
        m_seqs = [(q, k, v, gates[:, :, dr, 0], jax.nn.log_sigmoid(gates[:, :, dr, 1])) for dr in range(2)]
        r_qkv = (rq.reshape(B, T, RET_HEADS, RET_QK_DIM),
                 rk.reshape(B, T, RET_HEADS, RET_QK_DIM) * (RET_QK_DIM ** -0.5),
                 rv.reshape(B, T, RET_HEADS, RET_V_DIM))
        return m_seqs, [r_qkv, r_qkv], mo, rg

    m_x, r_x, o_x, g_x = prep(hx @ w_in)
    m_c, r_c, o_c, g_c = prep(hc @ w_in)
    B = hx.shape[0]
    m_state0 = (jnp.zeros((B, H, d, d), jnp.float32), jnp.zeros((B, H, d), jnp.float32),
                jnp.zeros((B, H), jnp.float32))
    r_state0 = jnp.zeros((B, RET_HEADS, RET_QK_DIM, RET_V_DIM), jnp.float32)
    log_gamma = jax.nn.log_sigmoid(ret_logit.astype(jnp.float32))
    m_lat, m_ctx = run_bidirectional([mlstm_scan, mlstm_scan], m_x, m_c, m_state0, need_ctx_out)
    r_scans = [functools.partial(retention_scan, log_gamma[dr]) for dr in range(2)]
    r_lat, r_ctx = run_bidirectional(r_scans, r_x, r_c, r_state0, need_ctx_out)

    def merge(h_m, h_r, o, g):
        Bm, Tm = o.shape[:2]
        hm = layer_norm(h_m, m_norm_g.reshape(H, d)) * jax.nn.sigmoid(o).reshape(Bm, Tm, H, d)
        hr = (layer_norm(h_r, ret_g.reshape(RET_HEADS, RET_V_DIM), ret_b.reshape(RET_HEADS, RET_V_DIM))
              * jax.nn.silu(g).reshape(Bm, Tm, RET_HEADS, RET_V_DIM))
        return jnp.concatenate([hm.reshape(Bm, Tm, -1), hr.reshape(Bm, Tm, -1)], axis=-1) @ w_out

    y_x = merge(m_lat, r_lat, o_x, g_x)
    y_c = merge(m_ctx, r_ctx, o_c, g_c) if need_ctx_out else None
    return y_x, y_c


def setup_inputs(seed: int = 0) -> dict:
    key = jax.random.key(seed)
    ks = jax.random.split(key, 22)
    nrm = lambda k, shape, s: jax.random.normal(k, shape, jnp.float32) * s
    gate_base = jnp.stack([jnp.zeros((MLSTM_HEADS,), jnp.float32), jnp.linspace(3.0, 6.0, MLSTM_HEADS)])
    gamma0 = 1.0 - 2.0 ** (-5.0 - jnp.arange(RET_HEADS, dtype=jnp.float32))
    return {
        'x': nrm(ks[0], (BATCH, SEQ, D_MODEL), 1.0),
        'c': nrm(ks[1], (BATCH, D_MODEL), 1.0),
        'ctx': nrm(ks[2], (BATCH, CTX_LEN, D_MODEL), 1.0),
        'c_ctx': nrm(ks[3], (D_MODEL,), 1.0),
        'ada_w': nrm(ks[4], (DEPTH, D_MODEL, 9 * D_MODEL), D_MODEL ** -0.5),
        'ada_b': nrm(ks[5], (DEPTH, 9 * D_MODEL), 0.02),
        'ln_g': 1.0 + nrm(ks[6], (DEPTH, 3, D_MODEL), 0.02),
        'ln_b': nrm(ks[7], (DEPTH, 3, D_MODEL), 0.02),
        'ffn_w_in': nrm(ks[8], (DEPTH, 2, D_MODEL, 2 * D_FF), D_MODEL ** -0.5),
        'ffn_w_out': nrm(ks[9], (DEPTH, 2, D_FF, D_MODEL), BETA * D_FF ** -0.5),
        'attn_w_in': nrm(ks[10], (N_EVEN, D_MODEL, ATTN_IN), D_MODEL ** -0.5),
        'attn_w_out': nrm(ks[11], (N_EVEN, ATTN_OUT, D_MODEL), BETA * ATTN_OUT ** -0.5),
        'diff_lambda': nrm(ks[12], (N_EVEN, 4, DIFF_HEAD_DIM), 0.1),
        'diff_norm_g': 1.0 + nrm(ks[13], (N_EVEN, DIFF_V_DIM), 0.02),
        'sink_logits': nrm(ks[14], (N_EVEN, WIN_HEADS), 0.5),
        'rec_w_in': nrm(ks[15], (N_ODD, D_MODEL, REC_IN), D_MODEL ** -0.5),
        'rec_w_out': nrm(ks[16], (N_ODD, REC_OUT, D_MODEL), BETA * REC_OUT ** -0.5),
        'mlstm_gate_b': gate_base + nrm(ks[17], (N_ODD, 2, 2, MLSTM_HEADS), 0.1),
        'mlstm_norm_g': 1.0 + nrm(ks[18], (N_ODD, MLSTM_HEADS * MLSTM_HEAD_DIM), 0.02),
        'ret_decay_logit': jnp.log(gamma0 / (1.0 - gamma0)) + nrm(ks[19], (N_ODD, 2, RET_HEADS), 0.1),
        'ret_norm_g': 1.0 + nrm(ks[20], (N_ODD, RET_HEADS * RET_V_DIM), 0.02),
        'ret_norm_b': nrm(ks[21], (N_ODD, RET_HEADS * RET_V_DIM), 0.02),
    }


def reference(x, c, ctx, c_ctx, ada_w, ada_b, ln_g, ln_b, ffn_w_in, ffn_w_out, attn_w_in, attn_w_out,
              diff_lambda, diff_norm_g, sink_logits, rec_w_in, rec_w_out, mlstm_gate_b, mlstm_norm_g,
              ret_decay_logit, ret_norm_g, ret_norm_b):
    out_dtype = x.dtype
    B, D = x.shape[0], x.shape[2]
    for l in range(DEPTH):
        last = l == DEPTH - 1
        mod_x = modulation(c, ada_w[l], ada_b[l]).reshape(B, 1, 3, 3, D)
        mod_c = modulation(c_ctx, ada_w[l], ada_b[l]).reshape(1, 1, 3, 3, D)
        x = ffn_sublayer(x, mod_x, 0, ffn_w_in[l, 0], ffn_w_out[l, 0], ln_g[l, 0], ln_b[l, 0])
        ctx = ffn_sublayer(ctx, mod_c, 0, ffn_w_in[l, 0], ffn_w_out[l, 0], ln_g[l, 0], ln_b[l, 0])
        hx, hc = modulate(x, mod_x, 1), modulate(ctx, mod_c, 1)
        i = l // 2
        if l % 2 == 0:
            lambda_init = 0.8 - 0.6 * math.exp(-0.3 * l)
            y_x, y_c = attention_mixer(hx, hc, attn_w_in[i], attn_w_out[i], diff_lambda[i], diff_norm_g[i],
                                       sink_logits[i], lambda_init, not last)
        else:
            y_x, y_c = recurrent_mixer(hx, hc, rec_w_in[i], rec_w_out[i], mlstm_gate_b[i], mlstm_norm_g[i],
                                       ret_decay_logit[i], ret_norm_g[i], ret_norm_b[i], not last)
        x = post_norm_residual(x, y_x, mod_x, 1, 1.0, ln_g[l, 1], ln_b[l, 1])
        x = ffn_sublayer(x, mod_x, 2, ffn_w_in[l, 1], ffn_w_out[l, 1], ln_g[l, 2], ln_b[l, 2])
        if not last:
            ctx = post_norm_residual(ctx, y_c, mod_c, 1, 1.0, ln_g[l, 1], ln_b[l, 1])
            ctx = ffn_sublayer(ctx, mod_c, 2, ffn_w_in[l, 1], ffn_w_out[l, 1], ln_g[l, 2], ln_b[l, 2])
    return x.astype(out_dtype)
```

```python
import functools
import math

import jax
import jax.numpy as jnp
from jax import lax
from jax.experimental import pallas as pl
from jax.experimental.pallas import tpu as pltpu

F32 = jnp.float32
BF16 = jnp.bfloat16

D_MODEL = 1024
DEPTH = 2
GRID_W = 64
D_FF = 2816
DIFF_HEADS = 4
DIFF_HEAD_DIM = 64
DIFF_V_DIM = 128
WIN_HEADS = 8
WIN_KV_HEADS = 2
WIN_GROUP = WIN_HEADS // WIN_KV_HEADS
WIN_HEAD_DIM = 64
WINDOW = 128
Q_BLOCK = 128
ROPE_BASE = 10000.0
ROPE_FREQS = 16
MLSTM_HEADS = 4
MLSTM_HEAD_DIM = 128
RET_HEADS = 4
RET_QK_DIM = 64
RET_V_DIM = 128
CHUNK = 128
ALPHA = (2.0 * DEPTH) ** 0.25
EPS = 1e-5
NEG_BIG = -1e30

LANES = 128
V7X_VMEM_LIMIT_BYTES = 56 * 1024 * 1024

TOKEN_TILE = 512
FF_CHUNK = 256
ATTN_TQ = 512
ATTN_TK = 512


def _params(semantics):
    return pltpu.CompilerParams(dimension_semantics=semantics, vmem_limit_bytes=V7X_VMEM_LIMIT_BYTES)


def _resident(shape):
    return pl.BlockSpec(shape, lambda *_: (0,) * len(shape), pipeline_mode=pl.Buffered(1))


def _layer_norm(y, g, b):
    mu = jnp.mean(y, axis=-1, keepdims=True)
    yc = y - mu
    var = jnp.mean(yc * yc, axis=-1, keepdims=True)
    out = yc * lax.rsqrt(var + EPS) * g
    return out if b is None else out + b


def _log_sigmoid(x):
    return jnp.minimum(x, 0.0) - jnp.log1p(jnp.exp(-jnp.abs(x)))


def _dot(a, b):
    return jnp.dot(a, b, preferred_element_type=F32)


def _dot_nt(a, b):
    return lax.dot_general(a, b, (((1,), (1,)), ((), ())), preferred_element_type=F32)


def _dot_tn(a, b):
    return lax.dot_general(a, b, (((0,), (0,)), ((), ())), preferred_element_type=F32)


def _mod_kernel(c_ref, w_ref, b_ref, o_ref):
    c = c_ref[...]
    h = c * jax.nn.sigmoid(c)
    o_ref[...] = jnp.dot(h, w_ref[...], preferred_element_type=F32,
                         precision=lax.Precision.HIGHEST) + b_ref[...]


def _modulation(cc, w, b):
    rows, d = cc.shape
    n = w.shape[1]
    tn = 1024
    return pl.pallas_call(
        _mod_kernel,
        out_shape=jax.ShapeDtypeStruct((rows, n), F32),
        grid=(n // tn,),
        in_specs=[pl.BlockSpec((rows, d), lambda j: (0, 0)),
                  pl.BlockSpec((d, tn), lambda j: (0, j)),
                  pl.BlockSpec((1, tn), lambda j: (0, j))],
        out_specs=pl.BlockSpec((rows, tn), lambda j: (0, j)),
        compiler_params=_params(("parallel",)),
        name="modulation",
    )(cc, w, b)


def _token_tile(n_tokens, tokens_per_row):
    tm = min(TOKEN_TILE, tokens_per_row)
    assert tokens_per_row % tm == 0 and n_tokens % tm == 0
    return tm


def _mod_spec(sub, tiles_per_row):
    return pl.BlockSpec((None, None, 3, D_MODEL), lambda i: (i // tiles_per_row, sub, 0, 0))


def _ffn_kernel(x_ref, mod_ref, w_in_ref, w_out_ref, g_ref, b_ref, o_ref, acc_ref):
    x = x_ref[...]
    shift, scale, gate = mod_ref[0:1, :], mod_ref[1:2, :], mod_ref[2:3, :]
    h = (x * (1.0 + scale) + shift).astype(BF16)
    for c in range(D_FF // FF_CHUNK):
        lo = c * FF_CHUNK
        gt = _dot(h, w_in_ref[:, lo:lo + FF_CHUNK])
        up = _dot(h, w_in_ref[:, D_FF + lo:D_FF + lo + FF_CHUNK])
        act = (gt * jax.nn.sigmoid(gt) * up).astype(BF16)
        part = _dot(act, w_out_ref[lo:lo + FF_CHUNK, :])
        if c == 0:
            acc_ref[...] = part
        else:
            acc_ref[...] += part
    y = ALPHA * x + 0.5 * gate * acc_ref[...]
    o_ref[...] = _layer_norm(y, g_ref[...], b_ref[...])


def _ffn(x2, mod4, sub, tokens_per_row, w_in, w_out, g, b):
    n = x2.shape[0]
    tm = _token_tile(n, tokens_per_row)
    tile = pl.BlockSpec((tm, D_MODEL), lambda i: (i, 0))
    return pl.pallas_call(
        _ffn_kernel,
        out_shape=jax.ShapeDtypeStruct((n, D_MODEL), F32),
        grid=(n // tm,),
        in_specs=[tile, _mod_spec(sub, tokens_per_row // tm),
                  _resident((D_MODEL, 2 * D_FF)), _resident((D_FF, D_MODEL)),
                  _resident((1, D_MODEL)), _resident((1, D_MODEL))],
        out_specs=tile,
        scratch_shapes=[pltpu.VMEM((tm, D_MODEL), F32)],
        compiler_params=_params(("parallel",)),
        name="ffn_sublayer",
    )(x2, mod4, w_in, w_out, g, b)


def _rope(p, cos, sin):
    lane = lax.broadcasted_iota(jnp.int32, (p.shape[0], LANES), 1)
    first_half = (lane & ROPE_FREQS) == 0
    outs = []
    for j in range(p.shape[1] // LANES):
        xs = p[:, j * LANES:(j + 1) * LANES]
        partner = jnp.where(first_half, pltpu.roll(xs, LANES - ROPE_FREQS, 1), pltpu.roll(xs, ROPE_FREQS, 1))
        outs.append(xs * cos + partner * sin)
    return jnp.concatenate(outs, axis=1)


_ATTN_COLS = (("aq", 512), ("ak", 512), ("av", 512), ("bq", 1024), ("bk", 256), ("bv", 256))


def _attn_proj_kernel(*refs, rope):
    if rope:
        x_ref, mod_ref, w_ref, cos_ref, sin_ref = refs[:5]
        outs = refs[5:]
        cos, sin = cos_ref[...], sin_ref[...]
    else:
        x_ref, mod_ref, w_ref = refs[:3]
        outs = refs[3:]
    x = x_ref[...]
    shift, scale = mod_ref[0:1, :], mod_ref[1:2, :]
    h = (x * (1.0 + scale) + shift).astype(BF16)
    lo = 0
    for (name, width), o_ref in zip(_ATTN_COLS, outs):
        p = _dot(h, w_ref[:, lo:lo + width])
        lo += width
        if rope and name in ("aq", "ak", "bq", "bk"):
            p = _rope(p, cos, sin)
        if name in ("aq", "bq"):
            p = p * (DIFF_HEAD_DIM ** -0.5)
        o_ref[...] = p.astype(BF16)


def _attn_proj(x2, mod4, tokens_per_row, w, tables):
    n = x2.shape[0]
    tm = _token_tile(n, tokens_per_row)
    rope = tables is not None
    tile = lambda width: pl.BlockSpec((tm, width), lambda i: (i, 0))
    in_specs = [tile(D_MODEL), _mod_spec(1, tokens_per_row // tm), _resident(w.shape)]
    args = [x2, mod4, w]
    if rope:
        tpr = tokens_per_row // tm
        tab = pl.BlockSpec((tm, LANES), lambda i: (i % tpr, 0))
        in_specs += [tab, tab]
        args += list(tables)
    return pl.pallas_call(
        functools.partial(_attn_proj_kernel, rope=rope),
        out_shape=[jax.ShapeDtypeStruct((n, width), BF16) for _, width in _ATTN_COLS],
        grid=(n // tm,),
        in_specs=in_specs,
        out_specs=[tile(width) for _, width in _ATTN_COLS],
        compiler_params=_params(("parallel",)),
        name="attn_in_proj",
    )(*args)


def _attn_out_kernel(x_ref, mod_ref, a_ref, b_ref, wa_ref, wb_ref, g_ref, bb_ref, o_ref):
    y = _dot(a_ref[...], wa_ref[...]) + _dot(b_ref[...], wb_ref[...])
    z = ALPHA * x_ref[...] + mod_ref[2:3, :] * y
    o_ref[...] = _layer_norm(z, g_ref[...], bb_ref[...])


def _attn_out(x2, mod4, tokens_per_row, a, b, wa, wb, g, bb):
    n = x2.shape[0]
    tm = _token_tile(n, tokens_per_row)
    tile = lambda width: pl.BlockSpec((tm, width), lambda i: (i, 0))
    return pl.pallas_call(
        _attn_out_kernel,
        out_shape=jax.ShapeDtypeStruct((n, D_MODEL), F32),
        grid=(n // tm,),
        in_specs=[tile(D_MODEL), _mod_spec(1, tokens_per_row // tm), tile(a.shape[1]), tile(b.shape[1]),
                  _resident(wa.shape), _resident(wb.shape), _resident((1, D_MODEL)), _resident((1, D_MODEL))],
        out_specs=tile(D_MODEL),
        compiler_params=_params(("parallel",)),
        name="attn_out_proj",
    )(x2, mod4, a, b, wa, wb, g, bb)


def _diff_attn_kernel(*refs, n_lat, lc, lambda_init):
    if n_lat:
        q_ref, k_ref, vt_ref, kc_ref, vct_ref, lam_ref, g_ref, o_ref = refs
    else:
        q_ref, kc_ref, vct_ref, lam_ref, g_ref, o_ref = refs
    q = q_ref[...]
    chunks = [(k_ref, vt_ref, c * ATTN_TK, ATTN_TK) for c in range(n_lat)] + [(kc_ref, vct_ref, 0, lc)]
    m, l, acc = [None, None], [None, None], [None, None]
    for idx, (kr, vr, lo, size) in enumerate(chunks):
        kblk = kr[lo:lo + size, :]
        vt = vr[:, lo:lo + size]
        lane = lax.broadcasted_iota(jnp.int32, kblk.shape, 1)
        zero = jnp.zeros_like(kblk)
        ksub = (jnp.where(lane < DIFF_HEAD_DIM, kblk, zero), jnp.where(lane >= DIFF_HEAD_DIM, kblk, zero))
        for sub in range(2):
            s = _dot_nt(ksub[sub], q)
            smax = jnp.max(s, axis=0, keepdims=True)
            if idx == 0:
                m_new = smax
                p = jnp.exp(s - m_new)
                l[sub] = jnp.sum(p, axis=0, keepdims=True)
                acc[sub] = _dot(vt, p.astype(BF16))
            else:
                m_new = jnp.maximum(m[sub], smax)
                alpha = jnp.exp(m[sub] - m_new)
                p = jnp.exp(s - m_new)
                l[sub] = alpha * l[sub] + jnp.sum(p, axis=0, keepdims=True)
                acc[sub] = alpha * acc[sub] + _dot(vt, p.astype(BF16))
            m[sub] = m_new
    t = lam_ref[...]
    lam = (jnp.exp(jnp.sum(t[0:1, :] * t[1:2, :], axis=1, keepdims=True))
           - jnp.exp(jnp.sum(t[2:3, :] * t[3:4, :], axis=1, keepdims=True)) + lambda_init)
    o_t = acc[0] * (1.0 / l[0]) - lam * (acc[1] * (1.0 / l[1]))
    o = o_t.T
    o = o * lax.rsqrt(jnp.mean(o * o, axis=-1, keepdims=True) + EPS) * g_ref[...]
    o_ref[...] = (o * (1.0 - lambda_init)).astype(BF16)


def _diff_attn(q, k, vt, kc, vct, lam_vec, g, lambda_init):
    bsz, tq_total, _ = q.shape
    lc = kc.shape[1]
    tq = min(ATTN_TQ, tq_total)
    n_lat = 0 if k is None else k.shape[1] // ATTN_TK
    assert tq_total % tq == 0 and (k is None or k.shape[1] % ATTN_TK == 0)
    qspec = pl.BlockSpec((None, tq, LANES), lambda b, h, i: (b, i, h))
    in_specs, args = [qspec], [q]
    if n_lat:
        t = k.shape[1]
        in_specs += [pl.BlockSpec((None, t, LANES), lambda b, h, i: (b, 0, h)),
                     pl.BlockSpec((None, LANES, t), lambda b, h, i: (b, h, 0))]
        args += [k, vt]
    in_specs += [pl.BlockSpec((None, lc, LANES), lambda b, h, i: (b, 0, h)),
                 pl.BlockSpec((None, LANES, lc), lambda b, h, i: (b, h, 0)),
                 pl.BlockSpec((4, DIFF_HEAD_DIM), lambda b, h, i: (0, 0)),
                 pl.BlockSpec((1, DIFF_V_DIM), lambda b, h, i: (0, 0))]
    args += [kc, vct, lam_vec, g]
    return pl.pallas_call(
        functools.partial(_diff_attn_kernel, n_lat=n_lat, lc=lc, lambda_init=lambda_init),
        out_shape=jax.ShapeDtypeStruct((bsz, tq_total, DIFF_HEADS * DIFF_V_DIM), BF16),
        grid=(bsz, DIFF_HEADS, tq_total // tq),
        in_specs=in_specs,
        out_specs=qspec,
        compiler_params=_params(("parallel", "parallel", "parallel")),
        name="diff_attention",
    )(*args)


def _win_attn_kernel(*refs, has_window, lc):
    if has_window:
        q_ref, kp_ref, kcur_ref, kn_ref, vp_ref, vcur_ref, vn_ref, kc_ref, vc_ref, sink_ref, o_ref = refs
        kw = jnp.concatenate([kp_ref[...], kcur_ref[...], kn_ref[...], kc_ref[...]], axis=0)
        vw = jnp.concatenate([vp_ref[...], vcur_ref[...], vn_ref[...], vc_ref[...]], axis=0)
        span = Q_BLOCK + 2 * WINDOW
        i = pl.program_id(1)
        last = pl.num_programs(1) - 1
        r = lax.broadcasted_iota(jnp.int32, (Q_BLOCK, span + lc), 0)
        c = lax.broadcasted_iota(jnp.int32, (Q_BLOCK, span + lc), 1)
        lo_valid = jnp.where(i == 0, WINDOW, 0)
        hi_valid = jnp.where(i == last, WINDOW + Q_BLOCK, span)
        in_win = (c >= jnp.maximum(r, lo_valid)) & (c <= r + 2 * WINDOW) & (c < hi_valid)
        bias = jnp.where(in_win | (c >= span), 0.0, NEG_BIG)
        bias = jnp.concatenate([bias] * WIN_GROUP, axis=0)
    else:
        q_ref, kc_ref, vc_ref, sink_ref, o_ref = refs
        kw, vw = kc_ref[...], vc_ref[...]
        bias = None
    nk = kw.shape[0]
    for kh in range(WIN_KV_HEADS):
        heads = [kh * WIN_GROUP + g for g in range(WIN_GROUP)]
        qg = jnp.concatenate([q_ref[:, h * LANES:(h + 1) * LANES] for h in heads], axis=0)
        s = _dot_nt(qg, kw[:, kh * LANES:(kh + 1) * LANES])
        if bias is not None:
            s = s + bias
        sink = jnp.concatenate([jnp.full((Q_BLOCK, 1), sink_ref[h], F32) for h in heads], axis=0)
        m = jnp.maximum(jnp.max(s, axis=1, keepdims=True), sink)
        p = jnp.exp(s - m)
        denom = jnp.sum(p, axis=1, keepdims=True) + jnp.exp(sink - m)
        out = _dot(p.astype(BF16), vw[:, kh * LANES:(kh + 1) * LANES]) * (1.0 / denom)
        for g, h in enumerate(heads):
            o_ref[:, h * LANES:(h + 1) * LANES] = out[g * Q_BLOCK:(g + 1) * Q_BLOCK, :].astype(BF16)


def _win_attn(q, k, v, kc, vc, sink, has_window):
    bsz, tq_total, _ = q.shape
    lc = kc.shape[1]
    nq = tq_total // Q_BLOCK
    qspec = pl.BlockSpec((None, Q_BLOCK, WIN_HEADS * LANES), lambda b, i: (b, i, 0))
    kvw = WIN_KV_HEADS * LANES
    in_specs, args = [qspec], [q]
    if has_window:
        prev = pl.BlockSpec((None, Q_BLOCK, kvw), lambda b, i: (b, jnp.maximum(i - 1, 0), 0))
        cur = pl.BlockSpec((None, Q_BLOCK, kvw), lambda b, i: (b, i, 0))
        nxt = pl.BlockSpec((None, Q_BLOCK, kvw), lambda b, i: (b, jnp.minimum(i + 1, nq - 1), 0))
        in_specs += [prev, cur, nxt, prev, cur, nxt]
        args += [k, k, k, v, v, v]
    ctx_spec = pl.BlockSpec((None, lc, kvw), lambda b, i: (b, 0, 0))
    in_specs += [ctx_spec, ctx_spec, pl.BlockSpec(memory_space=pltpu.SMEM)]
    args += [kc, vc, sink]
    return pl.pallas_call(
        functools.partial(_win_attn_kernel, has_window=has_window, lc=lc),
        out_shape=jax.ShapeDtypeStruct((bsz, tq_total, WIN_HEADS * LANES), BF16),
        grid=(bsz, nq),
        in_specs=in_specs,
        out_specs=qspec,
        compiler_params=_params(("parallel", "parallel")),
        name="window_attention",
    )(*args)


_REC_COLS = (("mq", 512, BF16), ("mk", 512, BF16), ("mv", 512, BF16), ("mo", 512, F32),
             ("rq", 512, BF16), ("rk", 512, BF16), ("rv", 512, BF16), ("rg", 512, F32), ("gates", 128, F32))
_N_GATES = 2 * 2 * MLSTM_HEADS


def _rec_proj_kernel(x_ref, mod_ref, w_ref, gate_b_ref, *outs):
    x = x_ref[...]
    shift, scale = mod_ref[0:1, :], mod_ref[1:2, :]
    h = (x * (1.0 + scale) + shift).astype(BF16)
    lo = 0
    for (name, width, dtype), o_ref in zip(_REC_COLS, outs):
        p = _dot(h, w_ref[:, lo:lo + width])
        lo += width
        if name == "mk":
            p = p * (MLSTM_HEAD_DIM ** -0.5)
        elif name == "rk":
            p = p * (RET_QK_DIM ** -0.5)
        elif name == "gates":
            p = p + gate_b_ref[...]
            lane = lax.broadcasted_iota(jnp.int32, p.shape, 1)
            p = jnp.where((lane & MLSTM_HEADS) != 0, _log_sigmoid(p), p)
        o_ref[...] = p.astype(dtype)


def _rec_proj(x2, mod4, tokens_per_row, w, gate_b):
    n = x2.shape[0]
    tm = _token_tile(n, tokens_per_row)
    tile = lambda width: pl.BlockSpec((tm, width), lambda i: (i, 0))
    return pl.pallas_call(
        _rec_proj_kernel,
        out_shape=[jax.ShapeDtypeStruct((n, width), dtype) for _, width, dtype in _REC_COLS],
        grid=(n // tm,),
        in_specs=[tile(D_MODEL), _mod_spec(1, tokens_per_row // tm), _resident(w.shape), _resident((1, LANES))],
        out_specs=[tile(width) for _, width, _ in _REC_COLS],
        compiler_params=_params(("parallel",)),
        name="rec_in_proj",
    )(x2, mod4, w, gate_b)


def _scan_specs(ncc, ncl, width, rows):
    ctx_f = pl.BlockSpec((None, rows, width), lambda b, s: (b, jnp.minimum(s, ncc - 1), 0))
    ctx_b = pl.BlockSpec((None, rows, width), lambda b, s: (b, jnp.maximum(ncc - 1 - s, 0), 0))
    lat_f = pl.BlockSpec((None, rows, width), lambda b, s: (b, jnp.maximum(s - ncc, 0), 0))
    lat_b = pl.BlockSpec((None, rows, width), lambda b, s: (b, jnp.minimum(ncl - 1 - (s - ncc), ncl - 1), 0))
    return ctx_f, ctx_b, lat_f, lat_b


def _tri_masks():
    r = lax.broadcasted_iota(jnp.int32, (CHUNK, CHUNK), 0)
    c = lax.broadcasted_iota(jnp.int32, (CHUNK, CHUNK), 1)
    return r, c


def _mlstm_chunk(q, k, v, gates, cum, gates_t, cum_t, d, h, mask, c_ref, n_ref, m_ref, with_output):
    ci = d * 2 * MLSTM_HEADS + h
    cf = ci + MLSTM_HEADS
    row = d * MLSTM_HEADS + h
    cum_col, ic_col = cum[:, cf:cf + 1], gates[:, ci:ci + 1]
    cum_row, ic_row = cum_t[cf:cf + 1, :], gates_t[ci:ci + 1, :]
    b_end = jnp.sum(gates_t[cf:cf + 1, :], axis=1, keepdims=True)
    m_prev = m_ref[row:row + 1, 0:1]
    c_prev = c_ref[row]
    n_prev = n_ref[row:row + 1, :]
    kf = k.astype(F32)
    w_end = b_end - cum_col + ic_col
    m_new = jnp.maximum(b_end + m_prev, jnp.max(w_end, axis=0, keepdims=True))
    decay = jnp.exp(b_end + m_prev - m_new)
    kw = kf * jnp.exp(w_end - m_new)
    c_ref[row] = decay * c_prev + _dot_tn(kw.astype(BF16), v)
    n_ref[row:row + 1, :] = decay * n_prev + jnp.sum(kw, axis=0, keepdims=True)
    m_ref[row:row + 1, :] = jnp.broadcast_to(m_new, (1, LANES))
    if not with_output:
        return None
    log_d = jnp.where(mask, cum_col - cum_row + ic_row, NEG_BIG)
    log_inter = cum_col + m_prev
    m_t = jnp.maximum(log_inter, jnp.max(log_d, axis=1, keepdims=True))
    s = _dot_nt(q, k) * jnp.exp(log_d - m_t)
    w_inter = jnp.exp(log_inter - m_t)
    num = _dot(s.astype(BF16), v) + w_inter * _dot(q, c_prev.astype(BF16))
    den = jnp.sum(s, axis=1, keepdims=True) + w_inter * jnp.sum(q.astype(F32) * n_prev, axis=1, keepdims=True)
    return num * (1.0 / jnp.maximum(jnp.abs(den), jnp.exp(-m_t)))


def _mlstm_kernel(kc_f, vc_f, gc_f, kc_b, vc_b, gc_b, q_f, k_f, v_f, g_f, q_b, k_b, v_b, g_b,
                  o_f, o_b, c_ref, n_ref, m_ref, *, ncc):
    s = pl.program_id(1)

    @pl.when(s == 0)
    def _():
        c_ref[...] = jnp.zeros_like(c_ref)
        n_ref[...] = jnp.zeros_like(n_ref)
        m_ref[...] = jnp.zeros_like(m_ref)

    r, c = _tri_masks()
    masks = (c <= r, c >= r)
    tri = (jnp.where(masks[0], 1.0, 0.0), jnp.where(masks[1], 1.0, 0.0))

    def run(srcs, with_output):
        for d, (q_ref, k_ref, v_ref, g_ref, o_ref) in enumerate(srcs):
            gates = g_ref[...]
            cum = jnp.dot(tri[d], gates, preferred_element_type=F32, precision=lax.Precision.HIGHEST)
            gates_t, cum_t = gates.T, cum.T
            for h in range(MLSTM_HEADS):
                cols = slice(h * LANES, (h + 1) * LANES)
                q = q_ref[:, cols] if with_output else None
                out = _mlstm_chunk(q, k_ref[:, cols], v_ref[:, cols], gates, cum, gates_t, cum_t, d, h,
                                   masks[d], c_ref, n_ref, m_ref, with_output)
                if with_output:
                    o_ref[:, cols] = out

    @pl.when(s < ncc)
    def _():
        run(((None, kc_f, vc_f, gc_f, None), (None, kc_b, vc_b, gc_b, None)), False)

    @pl.when(s >= ncc)
    def _():
        run(((q_f, k_f, v_f, g_f, o_f), (q_b, k_b, v_b, g_b, o_b)), True)


def _mlstm_scan(q, k, v, gates, kc, vc, gatesc):
    bsz, t, width = q.shape
    ncl, ncc = t // CHUNK, kc.shape[1] // CHUNK
    cf, cb, lf, lb = _scan_specs(ncc, ncl, width, CHUNK)
    gcf, gcb, glf, glb = _scan_specs(ncc, ncl, LANES, CHUNK)
    nrows = 2 * MLSTM_HEADS
    return pl.pallas_call(
        functools.partial(_mlstm_kernel, ncc=ncc),
        out_shape=[jax.ShapeDtypeStruct((bsz, t, width), F32)] * 2,
        grid=(bsz, ncc + ncl),
        in_specs=[cf, cf, gcf, cb, cb, gcb, lf, lf, lf, glf, lb, lb, lb, glb],
        out_specs=[lf, lb],
        scratch_shapes=[pltpu.VMEM((nrows, MLSTM_HEAD_DIM, MLSTM_HEAD_DIM), F32),
                        pltpu.VMEM((nrows, LANES), F32), pltpu.VMEM((nrows, LANES), F32)],
        compiler_params=_params(("parallel", "arbitrary")),
        name="mlstm_scan",
    )(kc, vc, gatesc, kc, vc, gatesc, q, k, v, gates, q, k, v, gates)


def _ret_kernel(kc_f, vc_f, kc_b, vc_b, q_f, k_f, v_f, q_b, k_b, v_b, logit_ref, o_f, o_b, s_ref, *, ncc):
    s = pl.program_id(1)

    @pl.when(s == 0)
    def _():
        s_ref[...] = jnp.zeros_like(s_ref)

    log_gamma = _log_sigmoid(logit_ref[...])
    r, c = _tri_masks()
    rf, cf = r.astype(F32), c.astype(F32)
    pos = lax.broadcasted_iota(jnp.int32, (CHUNK, 1), 0).astype(F32)

    def run(srcs, with_output):
        for d, (q_ref, k_ref, v_ref, o_ref) in enumerate(srcs):
            for h in range(RET_HEADS):
                row = d * RET_HEADS + h
                cols = slice(h * LANES, (h + 1) * LANES)
                lg = log_gamma[d:d + 1, h:h + 1]
                k, v = k_ref[:, cols], v_ref[:, cols]
                to_end = jnp.exp(lg * ((CHUNK - 1.0 - pos) if d == 0 else pos))
                s_prev = s_ref[row]
                kd = (k.astype(F32) * to_end).astype(BF16)
                s_ref[row] = jnp.exp(lg * CHUNK) * s_prev + _dot_tn(kd, v)
                if with_output:
                    q = q_ref[:, cols]
                    rel = (rf - cf) if d == 0 else (cf - rf)
                    intra = jnp.where(rel >= 0.0, jnp.exp(lg * jnp.maximum(rel, 0.0)), 0.0)
                    from_start = jnp.exp(lg * ((pos + 1.0) if d == 0 else (CHUNK - pos)))
                    sc = (_dot_nt(q, k) * intra).astype(BF16)
                    o_ref[:, cols] = _dot(sc, v) + from_start * _dot(q, s_prev.astype(BF16))

    @pl.when(s < ncc)
    def _():
        run(((None, kc_f, vc_f, None), (None, kc_b, vc_b, None)), False)

    @pl.when(s >= ncc)
    def _():
        run(((q_f, k_f, v_f, o_f), (q_b, k_b, v_b, o_b)), True)


def _ret_scan(q, k, v, kc, vc, logit):
    bsz, t, width = q.shape
    ncl, ncc = t // CHUNK, kc.shape[1] // CHUNK
    cf, cb, lf, lb = _scan_specs(ncc, ncl, width, CHUNK)
    return pl.pallas_call(
        functools.partial(_ret_kernel, ncc=ncc),
        out_shape=[jax.ShapeDtypeStruct((bsz, t, width), F32)] * 2,
        grid=(bsz, ncc + ncl),
        in_specs=[cf, cf, cb, cb, lf, lf, lf, lb, lb, lb, pl.BlockSpec((8, LANES), lambda b, s: (0, 0))],
        out_specs=[lf, lb],
        scratch_shapes=[pltpu.VMEM((2 * RET_HEADS, LANES, RET_V_DIM), F32)],
        compiler_params=_params(("parallel", "arbitrary")),
        name="retention_scan",
    )(kc, vc, kc, vc, q, k, v, q, k, v, logit)


def _rec_out_kernel(x_ref, mod_ref, mf_ref, mb_ref, rf_ref, rb_ref, o_ref_in, rg_ref, mg_ref, rng_ref, rnb_ref,
                    w_ref, g_ref, b_ref, out_ref):
    hm = mf_ref[...] + mb_ref[...]
    hr = rf_ref[...] + rb_ref[...]
    o_gate = jax.nn.sigmoid(o_ref_in[...])
    rg = rg_ref[...]
    r_gate = rg * jax.nn.sigmoid(rg)
    parts_m, parts_r = [], []
    for h in range(MLSTM_HEADS):
        cols = slice(h * LANES, (h + 1) * LANES)
        parts_m.append(_layer_norm(hm[:, cols], mg_ref[:, cols], None) * o_gate[:, cols])
        parts_r.append(_layer_norm(hr[:, cols], rng_ref[:, cols], rnb_ref[:, cols]) * r_gate[:, cols])
    width = MLSTM_HEADS * MLSTM_HEAD_DIM
    y = (_dot(jnp.concatenate(parts_m, axis=1).astype(BF16), w_ref[0:width, :])
         + _dot(jnp.concatenate(parts_r, axis=1).astype(BF16), w_ref[width:, :]))
    z = ALPHA * x_ref[...] + mod_ref[2:3, :] * y
    out_ref[...] = _layer_norm(z, g_ref[...], b_ref[...])


def _rec_out(x2, mod4, tokens_per_row, mf, mb, rf, rb, o, rg, mg, rng, rnb, w, g, b):
    n = x2.shape[0]
    tm = _token_tile(n, tokens_per_row)
    tile = lambda width: pl.BlockSpec((tm, width), lambda i: (i, 0))
    half = tile(512)
    return pl.pallas_call(
        _rec_out_kernel,
        out_shape=jax.ShapeDtypeStruct((n, D_MODEL), F32),
        grid=(n // tm,),
        in_specs=[tile(D_MODEL), _mod_spec(1, tokens_per_row // tm), half, half, half, half, half, half,
                  _resident((1, 512)), _resident((1, 512)), _resident((1, 512)),
                  _resident(w.shape), _resident((1, D_MODEL)), _resident((1, D_MODEL))],
        out_specs=tile(D_MODEL),
        compiler_params=_params(("parallel",)),
        name="rec_out_proj",
    )(x2, mod4, mf, mb, rf, rb, o, rg, mg, rng, rnb, w, g, b)


def _pad_heads(w, n_heads, axis):
    shape = list(w.shape)
    shape[axis:axis + 1] = [n_heads, shape[axis] // n_heads]
    w = w.reshape(shape)
    pad = [(0, 0)] * w.ndim
    pad[axis + 1] = (0, LANES - shape[axis + 1])
    w = jnp.pad(w, pad)
    shape[axis:axis + 2] = [n_heads * LANES]
    return w.reshape(shape)


def _attn_weights(w_in, w_out):
    aq, ak, av, bq, bk, bv = jnp.split(w_in, [512, 1024, 1536, 2048, 2176], axis=1)
    w = jnp.concatenate([aq, ak, av, _pad_heads(bq, WIN_HEADS, 1), _pad_heads(bk, WIN_KV_HEADS, 1),
                         _pad_heads(bv, WIN_KV_HEADS, 1)], axis=1)
    wa = w_out[:DIFF_HEADS * DIFF_V_DIM]
    wb = _pad_heads(w_out[DIFF_HEADS * DIFF_V_DIM:], WIN_HEADS, 0)
    return w.astype(BF16), wa.astype(BF16), wb.astype(BF16)


def _rec_weights(w_in):
    mq, mk, mv, mo, mg, rq, rk, rv, rg = jnp.split(w_in, [512, 1024, 1536, 2048, 2064, 2320, 2576, 3088], axis=1)
    mg = jnp.pad(mg, ((0, 0), (0, LANES - _N_GATES)))
    w = jnp.concatenate([mq, mk, mv, mo, _pad_heads(rq, RET_HEADS, 1), _pad_heads(rk, RET_HEADS, 1), rv, rg, mg],
                        axis=1)
    return w.astype(BF16)


def _rope_tables(t):
    rows = t // GRID_W
    row = jnp.repeat(jnp.arange(rows), GRID_W)
    col = jnp.tile(jnp.arange(GRID_W), rows)
    inv = ROPE_BASE ** (-jnp.arange(ROPE_FREQS, dtype=F32) / ROPE_FREQS)
    ang_r, ang_c = row[:, None] * inv, col[:, None] * inv
    cos = jnp.concatenate([jnp.cos(ang_r)] * 2 + [jnp.cos(ang_c)] * 2, axis=1)
    sin = jnp.concatenate([-jnp.sin(ang_r), jnp.sin(ang_r), -jnp.sin(ang_c), jnp.sin(ang_c)], axis=1)
    return jnp.tile(cos, (1, 2)), jnp.tile(sin, (1, 2))


def kernel(x, c, ctx, c_ctx, ada_w, ada_b, ln_g, ln_b, ffn_w_in, ffn_w_out, attn_w_in, attn_w_out,
           diff_lambda, diff_norm_g, sink_logits, rec_w_in, rec_w_out, mlstm_gate_b, mlstm_norm_g,
           ret_decay_logit, ret_norm_g, ret_norm_b):
    bsz, t, d = x.shape
    lc = ctx.shape[1]
    assert d == D_MODEL and t % ATTN_TK == 0 and lc % CHUNK == 0
    out_dtype = x.dtype
    x2 = x.reshape(bsz * t, d).astype(F32)
    c2 = ctx.reshape(bsz * lc, d).astype(F32)
    mod_rows = 16
    cc = jnp.zeros((mod_rows, d), F32).at[:bsz].set(c).at[bsz].set(c_ctx)
    row2 = lambda v: v.reshape(1, -1).astype(F32)

    for l in range(DEPTH):
        last = l == DEPTH - 1
        i = l // 2
        mod = _modulation(cc, ada_w[l], ada_b[l].reshape(1, -1)).reshape(mod_rows, 3, 3, d)
        mod_x, mod_c = mod[:bsz], mod[bsz:bsz + 1]
        ffn = lambda z, m, sub, tpr, j: _ffn(z, m, sub, tpr, ffn_w_in[l, j].astype(BF16),
                                             ffn_w_out[l, j].astype(BF16), row2(ln_g[l, sub]), row2(ln_b[l, sub]))
        x2 = ffn(x2, mod_x, 0, t, 0)
        c2 = ffn(c2, mod_c, 0, bsz * lc, 0)
        g1, b1 = row2(ln_g[l, 1]), row2(ln_b[l, 1])
        if l % 2 == 0:
            lambda_init = 0.8 - 0.6 * math.exp(-0.3 * l)
            w, wa, wb = _attn_weights(attn_w_in[i], attn_w_out[i])
            aq, ak, av, bq, bk, bv = [a.reshape(bsz, t, -1) for a in _attn_proj(x2, mod_x, t, w, _rope_tables(t))]
            aqc, akc, avc, bqc, bkc, bvc = [a.reshape(bsz, lc, -1) for a in _attn_proj(c2, mod_c, bsz * lc, w, None)]
            avt, avct = jnp.swapaxes(av, 1, 2), jnp.swapaxes(avc, 1, 2)
            lam_vec, sub_g = diff_lambda[i].astype(F32), row2(diff_norm_g[i])
            sink = sink_logits[i].astype(F32)
            a_x = _diff_attn(aq, ak, avt, akc, avct, lam_vec, sub_g, lambda_init)
            b_x = _win_attn(bq, bk, bv, bkc, bvc, sink, True)
            flat = lambda a: a.reshape(-1, a.shape[-1])
            if not last:
                a_c = _diff_attn(aqc, None, None, akc, avct, lam_vec, sub_g, lambda_init)
                b_c = _win_attn(bqc, None, None, bkc, bvc, sink, False)
                c2_mix = _attn_out(c2, mod_c, bsz * lc, flat(a_c), flat(b_c), wa, wb, g1, b1)
            x2 = _attn_out(x2, mod_x, t, flat(a_x), flat(b_x), wa, wb, g1, b1)
        else:
            w = _rec_weights(rec_w_in[i])
            gate_b = jnp.pad(mlstm_gate_b[i].reshape(1, -1).astype(F32), ((0, 0), (0, LANES - _N_GATES)))
            px = [a.reshape(bsz, t, -1) for a in _rec_proj(x2, mod_x, t, w, gate_b)]
            pc = [a.reshape(bsz, lc, -1) for a in _rec_proj(c2, mod_c, bsz * lc, w, gate_b)]
            mq, mk, mv, mo, rq, rk, rv, rg, gates = px
            _, mkc, mvc, _, _, rkc, rvc, _, gatesc = pc
            assert last
            logit = jnp.zeros((8, LANES), F32).at[:2, :RET_HEADS].set(ret_decay_logit[i].astype(F32))
            m_f, m_b = _mlstm_scan(mq, mk, mv, gates, mkc, mvc, gatesc)
            r_f, r_b = _ret_scan(rq, rk, rv, rkc, rvc, logit)
            flat = lambda a: a.reshape(-1, a.shape[-1])
            x2 = _rec_out(x2, mod_x, t, flat(m_f), flat(m_b), flat(r_f), flat(r_b), flat(mo), flat(rg),
                          row2(mlstm_norm_g[i]), row2(ret_norm_g[i]), row2(ret_norm_b[i]),
                          rec_w_out[i].astype(BF16), g1, b1)
        x2 = ffn(x2, mod_x, 2, t, 1)
        if not last:
            c2 = ffn(c2_mix, mod_c, 2, bsz * lc, 1)
    return x2.reshape(bsz, t, d).astype(out_dtype)
```

```python
import functools
import math

import jax
import jax.numpy as jnp
from jax import lax
from jax.experimental import pallas as pl
from jax.experimental.pallas import tpu as pltpu

F32 = jnp.float32
BF16 = jnp.bfloat16

D_MODEL = 1024
DEPTH = 2
GRID_W = 64
D_FF = 2816
DIFF_HEADS = 4
DIFF_HEAD_DIM = 64
DIFF_V_DIM = 128
WIN_HEADS = 8
WIN_KV_HEADS = 2
WIN_GROUP = WIN_HEADS // WIN_KV_HEADS
WIN_HEAD_DIM = 64
WINDOW = 128
Q_BLOCK = 128
ROPE_BASE = 10000.0
ROPE_FREQS = 16
MLSTM_HEADS = 4
MLSTM_HEAD_DIM = 128
RET_HEADS = 4
RET_QK_DIM = 64
RET_V_DIM = 128
CHUNK = 128
ALPHA = (2.0 * DEPTH) ** 0.25
EPS = 1e-5
LOG2_E = 1.4426950408889634
NEG_BIG = -1e30

LANES = 128
BF16_SUBLANES = 16
V7X_VMEM_LIMIT_BYTES = 56 * 1024 * 1024

TOKEN_TILE = 512
FF_CHUNK = 256
ATTN_TQ = 512
ATTN_TK = 1024
ATTN_PIECE = 256


def _params(semantics):
    return pltpu.CompilerParams(dimension_semantics=semantics, vmem_limit_bytes=V7X_VMEM_LIMIT_BYTES)


def _resident(shape):
    return pl.BlockSpec(shape, lambda *_: (0,) * len(shape), pipeline_mode=pl.Buffered(1))


def _layer_norm(y, g, b):
    mu = jnp.mean(y, axis=-1, keepdims=True)
    yc = y - mu
    var = jnp.mean(yc * yc, axis=-1, keepdims=True)
    out = yc * lax.rsqrt(var + EPS) * g
    return out if b is None else out + b


def _log_sigmoid(x):
    return jnp.minimum(x, 0.0) - jnp.log1p(jnp.exp(-jnp.abs(x)))


def _dot(a, b):
    return jnp.dot(a, b, preferred_element_type=F32)


def _dot_nt(a, b):
    return lax.dot_general(a, b, (((1,), (1,)), ((), ())), preferred_element_type=F32)


def _dot_tn(a, b):
    return lax.dot_general(a, b, (((0,), (0,)), ((), ())), preferred_element_type=F32)


def _mod_kernel(c_ref, w_ref, b_ref, o_ref):
    c = c_ref[...]
    h = c * jax.nn.sigmoid(c)
    o_ref[...] = jnp.dot(h, w_ref[...], preferred_element_type=F32,
                         precision=lax.Precision.HIGHEST) + b_ref[...]


def _modulation(cc, w, b):
    rows, d = cc.shape
    n = w.shape[1]
    tn = 1024
    return pl.pallas_call(
        _mod_kernel,
        out_shape=jax.ShapeDtypeStruct((rows, n), F32),
        grid=(n // tn,),
        in_specs=[pl.BlockSpec((rows, d), lambda j: (0, 0)),
                  pl.BlockSpec((d, tn), lambda j: (0, j)),
                  pl.BlockSpec((1, tn), lambda j: (0, j))],
        out_specs=pl.BlockSpec((rows, tn), lambda j: (0, j)),
        compiler_params=_params(("parallel",)),
        name="modulation",
    )(cc, w, b)


def _token_tile(n_tokens, tokens_per_row):
    tm = min(TOKEN_TILE, tokens_per_row)
    assert tokens_per_row % tm == 0 and n_tokens % tm == 0
    return tm


def _mod_spec(sub, tiles_per_row):
    return pl.BlockSpec((None, None, 3, D_MODEL), lambda i: (i // tiles_per_row, sub, 0, 0))


def _ffn_kernel(x_ref, mod_ref, w_in_ref, w_out_ref, g_ref, b_ref, o_ref, acc_ref):
    x = x_ref[...]
    shift, scale, gate = mod_ref[0:1, :], mod_ref[1:2, :], mod_ref[2:3, :]
    h = (x * (1.0 + scale) + shift).astype(BF16)
    for c in range(D_FF // FF_CHUNK):
        lo = c * FF_CHUNK
        gt = _dot(h, w_in_ref[:, lo:lo + FF_CHUNK])
        up = _dot(h, w_in_ref[:, D_FF + lo:D_FF + lo + FF_CHUNK])
        act = (gt * jax.nn.sigmoid(gt) * up).astype(BF16)
        part = _dot(act, w_out_ref[lo:lo + FF_CHUNK, :])
        if c == 0:
            acc_ref[...] = part
        else:
            acc_ref[...] += part
    y = ALPHA * x + 0.5 * gate * acc_ref[...]
    o_ref[...] = _layer_norm(y, g_ref[...], b_ref[...])


def _ffn(x2, mod4, sub, tokens_per_row, w_in, w_out, g, b):
    n = x2.shape[0]
    tm = _token_tile(n, tokens_per_row)
    tile = pl.BlockSpec((tm, D_MODEL), lambda i: (i, 0))
    return pl.pallas_call(
        _ffn_kernel,
        out_shape=jax.ShapeDtypeStruct((n, D_MODEL), F32),
        grid=(n // tm,),
        in_specs=[tile, _mod_spec(sub, tokens_per_row // tm),
                  _resident((D_MODEL, 2 * D_FF)), _resident((D_FF, D_MODEL)),
                  _resident((1, D_MODEL)), _resident((1, D_MODEL))],
        out_specs=tile,
        scratch_shapes=[pltpu.VMEM((tm, D_MODEL), F32)],
        compiler_params=_params(("parallel",)),
        name="ffn_sublayer",
    )(x2, mod4, w_in, w_out, g, b)


def _rope(p, cos, sin):
    lane = lax.broadcasted_iota(jnp.int32, (p.shape[0], LANES), 1)
    first_half = (lane & ROPE_FREQS) == 0
    outs = []
    for j in range(p.shape[1] // LANES):
        xs = p[:, j * LANES:(j + 1) * LANES]
        partner = jnp.where(first_half, pltpu.roll(xs, LANES - ROPE_FREQS, 1), pltpu.roll(xs, ROPE_FREQS, 1))
        outs.append(xs * cos + partner * sin)
    return jnp.concatenate(outs, axis=1)


_ATTN_COLS = (("aq", 512), ("ak", 512), ("av", 512), ("bq", 1024), ("bk", 256), ("bv", 256))


def _attn_proj_kernel(*refs, rope):
    if rope:
        x_ref, mod_ref, w_ref, cos_ref, sin_ref = refs[:5]
        outs = refs[5:]
        cos, sin = cos_ref[...], sin_ref[...]
    else:
        x_ref, mod_ref, w_ref = refs[:3]
        outs = refs[3:]
    x = x_ref[...]
    shift, scale = mod_ref[0:1, :], mod_ref[1:2, :]
    h = (x * (1.0 + scale) + shift).astype(BF16)
    lo = 0
    for (name, width), o_ref in zip(_ATTN_COLS, outs):
        p = _dot(h, w_ref[:, lo:lo + width])
        lo += width
        if rope and name in ("aq", "ak", "bq", "bk"):
            p = _rope(p, cos, sin)
        if name == "aq":
            p = p * (DIFF_HEAD_DIM ** -0.5 * LOG2_E)
        elif name == "bq":
            p = p * (WIN_HEAD_DIM ** -0.5)
        o_ref[...] = p.astype(BF16)


def _attn_proj(x2, mod4, tokens_per_row, w, tables):
    n = x2.shape[0]
    tm = _token_tile(n, tokens_per_row)
    rope = tables is not None
    tile = lambda width: pl.BlockSpec((tm, width), lambda i: (i, 0))
    in_specs = [tile(D_MODEL), _mod_spec(1, tokens_per_row // tm), _resident(w.shape)]
    args = [x2, mod4, w]
    if rope:
        tpr = tokens_per_row // tm
        tab = pl.BlockSpec((tm, LANES), lambda i: (i % tpr, 0))
        in_specs += [tab, tab]
        args += list(tables)
    return pl.pallas_call(
        functools.partial(_attn_proj_kernel, rope=rope),
        out_shape=[jax.ShapeDtypeStruct((n, width), BF16) for _, width in _ATTN_COLS],
        grid=(n // tm,),
        in_specs=in_specs,
        out_specs=[tile(width) for _, width in _ATTN_COLS],
        compiler_params=_params(("parallel",)),
        name="attn_in_proj",
    )(*args)


def _attn_out_kernel(x_ref, mod_ref, a_ref, b_ref, wa_ref, wb_ref, g_ref, bb_ref, o_ref):
    y = _dot(a_ref[...], wa_ref[...]) + _dot(b_ref[...], wb_ref[...])
    z = ALPHA * x_ref[...] + mod_ref[2:3, :] * y
    o_ref[...] = _layer_norm(z, g_ref[...], bb_ref[...])


def _attn_out(x2, mod4, tokens_per_row, a, b, wa, wb, g, bb):
    n = x2.shape[0]
    tm = _token_tile(n, tokens_per_row)
    tile = lambda width: pl.BlockSpec((tm, width), lambda i: (i, 0))
    return pl.pallas_call(
        _attn_out_kernel,
        out_shape=jax.ShapeDtypeStruct((n, D_MODEL), F32),
        grid=(n // tm,),
        in_specs=[tile(D_MODEL), _mod_spec(1, tokens_per_row // tm), tile(a.shape[1]), tile(b.shape[1]),
                  _resident(wa.shape), _resident(wb.shape), _resident((1, D_MODEL)), _resident((1, D_MODEL))],
        out_specs=tile(D_MODEL),
        compiler_params=_params(("parallel",)),
        name="attn_out_proj",
    )(x2, mod4, a, b, wa, wb, g, bb)


def _diff_attn_kernel(*refs, n_lat, lc, lambda_init):
    if n_lat:
        q_ref, k_ref, vt_ref, kc_ref, vct_ref, lam_ref, g_ref, o_ref = refs
    else:
        q_ref, kc_ref, vct_ref, lam_ref, g_ref, o_ref = refs
    q = q_ref[...]
    chunks = [(k_ref, vt_ref, c * ATTN_TK, ATTN_TK) for c in range(n_lat)] + [(kc_ref, vct_ref, 0, lc)]
    m, acc = [None, None], [None, None]

    def n_pieces(chunk):
        return chunk[3] // ATTN_PIECE

    def scores(chunk, piece, sub):
        kr, _, lo, _ = chunk
        kblk = kr[lo + piece * ATTN_PIECE:lo + (piece + 1) * ATTN_PIECE, :]
        lane = lax.broadcasted_iota(jnp.int32, kblk.shape, 1)
        keep = (lane < DIFF_HEAD_DIM) if sub == 0 else (lane >= DIFF_HEAD_DIM)
        return _dot_nt(jnp.where(keep, kblk, jnp.zeros_like(kblk)), q)

    s_next = [[scores(chunks[0], j, sub) for j in range(n_pieces(chunks[0]))] for sub in range(2)]
    for idx, (kr, vr, lo, size) in enumerate(chunks):
        vt = jnp.concatenate([vr[:, lo:lo + size], jnp.ones((BF16_SUBLANES, size), BF16)], axis=0)
        nxt = chunks[idx + 1] if idx + 1 < len(chunks) else None
        for sub in range(2):
            s_cur, s_new = s_next[sub], []
            smax = functools.reduce(jnp.maximum, [jnp.max(s, axis=0, keepdims=True) for s in s_cur])
            m_new = smax if idx == 0 else jnp.maximum(m[sub], smax)
            ps = []
            for j, s in enumerate(s_cur):
                if nxt is not None and j < n_pieces(nxt):
                    s_new.append(scores(nxt, j, sub))
                ps.append(jnp.exp2(s - m_new).astype(BF16))
            if nxt is not None:
                s_new += [scores(nxt, j, sub) for j in range(len(s_new), n_pieces(nxt))]
            part = _dot(vt, jnp.concatenate(ps, axis=0))
            acc[sub] = part if idx == 0 else jnp.exp2(m[sub] - m_new) * acc[sub] + part
            m[sub], s_next[sub] = m_new, s_new
    t = lam_ref[...]
    lam = (jnp.exp(jnp.sum(t[0:1, :] * t[1:2, :], axis=1, keepdims=True))
           - jnp.exp(jnp.sum(t[2:3, :] * t[3:4, :], axis=1, keepdims=True)) + lambda_init)
    num = [a[:DIFF_V_DIM, :] * (1.0 / a[DIFF_V_DIM:DIFF_V_DIM + 1, :]) for a in acc]
    o = (num[0] - lam * num[1]).T
    o = o * lax.rsqrt(jnp.mean(o * o, axis=-1, keepdims=True) + EPS) * g_ref[...]
    o_ref[...] = (o * (1.0 - lambda_init)).astype(BF16)


def _diff_attn(q, k, vt, kc, vct, lam_vec, g, lambda_init):
    bsz, tq_total, _ = q.shape
    lc = kc.shape[1]
    tq = min(ATTN_TQ, tq_total)
    n_lat = 0 if k is None else k.shape[1] // ATTN_TK
    assert tq_total % tq == 0 and (k is None or k.shape[1] % ATTN_TK == 0)
    qspec = pl.BlockSpec((None, tq, LANES), lambda b, h, i: (b, i, h))
    in_specs, args = [qspec], [q]
    if n_lat:
        t = k.shape[1]
        in_specs += [pl.BlockSpec((None, t, LANES), lambda b, h, i: (b, 0, h)),
                     pl.BlockSpec((None, LANES, t), lambda b, h, i: (b, h, 0))]
        args += [k, vt]
    in_specs += [pl.BlockSpec((None, lc, LANES), lambda b, h, i: (b, 0, h)),
                 pl.BlockSpec((None, LANES, lc), lambda b, h, i: (b, h, 0)),
                 pl.BlockSpec((4, DIFF_HEAD_DIM), lambda b, h, i: (0, 0)),
                 pl.BlockSpec((1, DIFF_V_DIM), lambda b, h, i: (0, 0))]
    args += [kc, vct, lam_vec, g]
    return pl.pallas_call(
        functools.partial(_diff_attn_kernel, n_lat=n_lat, lc=lc, lambda_init=lambda_init),
        out_shape=jax.ShapeDtypeStruct((bsz, tq_total, DIFF_HEADS * DIFF_V_DIM), BF16),
        grid=(bsz, DIFF_HEADS, tq_total // tq),
        in_specs=in_specs,
        out_specs=qspec,
        compiler_params=_params(("parallel", "parallel", "parallel")),
        name="diff_attention",
    )(*args)


def _win_attn_kernel(*refs, has_window, lc):
    if has_window:
        q_ref, kp_ref, kcur_ref, kn_ref, vp_ref, vcur_ref, vn_ref, kc_ref, vc_ref, sink_ref, o_ref = refs
        kw = jnp.concatenate([kp_ref[...], kcur_ref[...], kn_ref[...], kc_ref[...]], axis=0)
        vw = jnp.concatenate([vp_ref[...], vcur_ref[...], vn_ref[...], vc_ref[...]], axis=0)
        span = Q_BLOCK + 2 * WINDOW
        i = pl.program_id(1)
        last = pl.num_programs(1) - 1
        r = lax.broadcasted_iota(jnp.int32, (Q_BLOCK, span + lc), 0)
        c = lax.broadcasted_iota(jnp.int32, (Q_BLOCK, span + lc), 1)
        lo_valid = jnp.where(i == 0, WINDOW, 0)
        hi_valid = jnp.where(i == last, WINDOW + Q_BLOCK, span)
        in_win = (c >= jnp.maximum(r, lo_valid)) & (c <= r + 2 * WINDOW) & (c < hi_valid)
        bias = jnp.where(in_win | (c >= span), 0.0, NEG_BIG)
        bias = jnp.concatenate([bias] * WIN_GROUP, axis=0)
    else:
        q_ref, kc_ref, vc_ref, sink_ref, o_ref = refs
        kw, vw = kc_ref[...], vc_ref[...]
        bias = None
    groups = [[kh * WIN_GROUP + g for g in range(WIN_GROUP)] for kh in range(WIN_KV_HEADS)]
    scores = []
    for kh, heads in enumerate(groups):
        qg = jnp.concatenate([q_ref[:, h * LANES:(h + 1) * LANES] for h in heads], axis=0)
        scores.append(_dot_nt(qg, kw[:, kh * LANES:(kh + 1) * LANES]))
    for kh, heads in enumerate(groups):
        s = scores[kh] if bias is None else scores[kh] + bias
        sink = jnp.concatenate([jnp.full((Q_BLOCK, 1), sink_ref[h], F32) for h in heads], axis=0)
        m = jnp.maximum(jnp.max(s, axis=1, keepdims=True), sink)
        p = jnp.exp(s - m)
        denom = jnp.sum(p, axis=1, keepdims=True) + jnp.exp(sink - m)
        out = _dot(p.astype(BF16), vw[:, kh * LANES:(kh + 1) * LANES]) * (1.0 / denom)
        for g, h in enumerate(heads):
            o_ref[:, h * LANES:(h + 1) * LANES] = out[g * Q_BLOCK:(g + 1) * Q_BLOCK, :].astype(BF16)


def _win_attn(q, k, v, kc, vc, sink, has_window):
    bsz, tq_total, _ = q.shape
    lc = kc.shape[1]
    nq = tq_total // Q_BLOCK
    qspec = pl.BlockSpec((None, Q_BLOCK, WIN_HEADS * LANES), lambda b, i: (b, i, 0))
    kvw = WIN_KV_HEADS * LANES
    in_specs, args = [qspec], [q]
    if has_window:
        prev = pl.BlockSpec((None, Q_BLOCK, kvw), lambda b, i: (b, jnp.maximum(i - 1, 0), 0))
        cur = pl.BlockSpec((None, Q_BLOCK, kvw), lambda b, i: (b, i, 0))
        nxt = pl.BlockSpec((None, Q_BLOCK, kvw), lambda b, i: (b, jnp.minimum(i + 1, nq - 1), 0))
        in_specs += [prev, cur, nxt, prev, cur, nxt]
        args += [k, k, k, v, v, v]
    ctx_spec = pl.BlockSpec((None, lc, kvw), lambda b, i: (b, 0, 0))
    in_specs += [ctx_spec, ctx_spec, pl.BlockSpec(memory_space=pltpu.SMEM)]
    args += [kc, vc, sink]
    return pl.pallas_call(
        functools.partial(_win_attn_kernel, has_window=has_window, lc=lc),
        out_shape=jax.ShapeDtypeStruct((bsz, tq_total, WIN_HEADS * LANES), BF16),
        grid=(bsz, nq),
        in_specs=in_specs,
        out_specs=qspec,
        compiler_params=_params(("parallel", "parallel")),
        name="window_attention",
    )(*args)


_REC_COLS = (("mq", 512, BF16), ("mk", 512, BF16), ("mv", 512, BF16), ("mo", 512, F32),
             ("rq", 512, BF16), ("rk", 512, BF16), ("rv", 512, BF16), ("rg", 512, F32),
             ("gi", 128, F32), ("gf", 128, F32))
_N_DIR_HEADS = 2 * MLSTM_HEADS


def _rec_proj_kernel(x_ref, mod_ref, w_ref, gate_b_ref, *outs):
    x = x_ref[...]
    shift, scale = mod_ref[0:1, :], mod_ref[1:2, :]
    h = (x * (1.0 + scale) + shift).astype(BF16)
    lo = 0
    for (name, width, dtype), o_ref in zip(_REC_COLS, outs):
        p = _dot(h, w_ref[:, lo:lo + width])
        lo += width
        if name == "mk":
            p = p * (MLSTM_HEAD_DIM ** -0.5)
        elif name == "rk":
            p = p * (RET_QK_DIM ** -0.5)
        elif name == "gi":
            p = p + gate_b_ref[0:1, :]
        elif name == "gf":
            p = _log_sigmoid(p + gate_b_ref[1:2, :])
        o_ref[...] = p.astype(dtype)


def _rec_proj(x2, mod4, tokens_per_row, w, gate_b):
    n = x2.shape[0]
    tm = _token_tile(n, tokens_per_row)
    tile = lambda width: pl.BlockSpec((tm, width), lambda i: (i, 0))
    return pl.pallas_call(
        _rec_proj_kernel,
        out_shape=[jax.ShapeDtypeStruct((n, width), dtype) for _, width, dtype in _REC_COLS],
        grid=(n // tm,),
        in_specs=[tile(D_MODEL), _mod_spec(1, tokens_per_row // tm), _resident(w.shape), _resident((2, LANES))],
        out_specs=[tile(width) for _, width, _ in _REC_COLS],
        compiler_params=_params(("parallel",)),
        name="rec_in_proj",
    )(x2, mod4, w, gate_b)


def _scan_specs(ncc, ncl, width, rows):
    ctx_f = pl.BlockSpec((None, rows, width), lambda b, s: (b, jnp.minimum(s, ncc - 1), 0))
    ctx_b = pl.BlockSpec((None, rows, width), lambda b, s: (b, jnp.maximum(ncc - 1 - s, 0), 0))
    lat_f = pl.BlockSpec((None, rows, width), lambda b, s: (b, jnp.maximum(s - ncc, 0), 0))
    lat_b = pl.BlockSpec((None, rows, width), lambda b, s: (b, jnp.minimum(ncl - 1 - (s - ncc), ncl - 1), 0))
    return ctx_f, ctx_b, lat_f, lat_b


def _tri_masks():
    r = lax.broadcasted_iota(jnp.int32, (CHUNK, CHUNK), 0)
    c = lax.broadcasted_iota(jnp.int32, (CHUNK, CHUNK), 1)
    return r, c


def _head_cols(ref, h):
    return ref[:, h * LANES:(h + 1) * LANES]


def _mlstm_step(srcs, masks, tri, c_ref, m_ref, with_output):
    units = [(d, h) for d in range(2) for h in range(MLSTM_HEADS)]
    dv = MLSTM_HEAD_DIM
    ones = jnp.ones((CHUNK, LANES), BF16)
    v_ext = {(d, h): jnp.concatenate([_head_cols(srcs[d][2], h), ones], axis=1) for d, h in units}
    gate = []
    for d in range(2):
        gi, gf = srcs[d][3][...], srcs[d][4][...]
        cum = jnp.dot(tri[d], gf, preferred_element_type=F32, precision=lax.Precision.HIGHEST)
        b_end = cum[CHUNK - 1:CHUNK, :] if d == 0 else cum[0:1, :]
        m_prev = m_ref[d:d + 1, :]
        w_end = b_end - cum + gi
        m_new = jnp.maximum(b_end + m_prev, jnp.max(w_end, axis=0, keepdims=True))
        decay = jnp.exp(b_end + m_prev - m_new)
        w = jnp.exp(w_end - m_new)
        m_ref[d:d + 1, :] = m_new
        log_inter = cum + m_prev if with_output else None
        key_term = (gi - cum).T if with_output else None
        gate.append((cum, decay, w, log_inter, key_term))
    if with_output:
        qk, qcn, log_d, li, rmax, m_t, sm, w_inter, pv = {}, {}, {}, {}, {}, {}, {}, {}, {}
        for d, h in units:
            q = _head_cols(srcs[d][0], h)
            qk[d, h] = _dot_nt(q, _head_cols(srcs[d][1], h))
            qcn[d, h] = _dot(q, c_ref[d * MLSTM_HEADS + h].astype(BF16))
        for d, h in units:
            lane = d * MLSTM_HEADS + h
            cum, _, _, log_inter, key_term = gate[d]
            log_d[d, h] = jnp.where(masks[d], cum[:, lane:lane + 1] + key_term[lane:lane + 1, :], NEG_BIG)
            li[d, h] = log_inter[:, lane:lane + 1]
        for u in units:
            rmax[u] = jnp.max(log_d[u], axis=1, keepdims=True)
        for u in units:
            m_t[u] = jnp.maximum(li[u], rmax[u])
            sm[u] = (qk[u] * jnp.exp(log_d[u] - m_t[u])).astype(BF16)
            w_inter[u] = jnp.exp(li[u] - m_t[u])
        for u in units:
            pv[u] = _dot(sm[u], v_ext[u])
        for d, h in units:
            u = (d, h)
            both = pv[u] + w_inter[u] * qcn[u]
            den = jnp.maximum(jnp.abs(both[:, dv:]), jnp.exp(-m_t[u]))
            srcs[d][5][:, h * LANES:(h + 1) * LANES] = both[:, :dv] * (1.0 / den)
    for d, h in units:
        row = d * MLSTM_HEADS + h
        _, decay, w, _, _ = gate[d]
        kw = (_head_cols(srcs[d][1], h).astype(F32) * w[:, row:row + 1]).astype(BF16)
        c_ref[row] = decay[:, row:row + 1] * c_ref[row] + _dot_tn(kw, v_ext[d, h])


def _mlstm_kernel(kc_f, vc_f, gic_f, gfc_f, kc_b, vc_b, gic_b, gfc_b,
                  q_f, k_f, v_f, gi_f, gf_f, q_b, k_b, v_b, gi_b, gf_b,
                  o_f, o_b, c_ref, m_ref, *, ncc):
    s = pl.program_id(1)

    @pl.when(s == 0)
    def _():
        c_ref[...] = jnp.zeros_like(c_ref)
        m_ref[...] = jnp.zeros_like(m_ref)

    r, c = _tri_masks()
    masks = (c <= r, c >= r)
    tri = (jnp.where(masks[0], 1.0, 0.0), jnp.where(masks[1], 1.0, 0.0))

    @pl.when(s < ncc)
    def _():
        _mlstm_step(((None, kc_f, vc_f, gic_f, gfc_f, None), (None, kc_b, vc_b, gic_b, gfc_b, None)),
                    masks, tri, c_ref, m_ref, False)

    @pl.when(s >= ncc)
    def _():
        _mlstm_step(((q_f, k_f, v_f, gi_f, gf_f, o_f), (q_b, k_b, v_b, gi_b, gf_b, o_b)),
                    masks, tri, c_ref, m_ref, True)


def _mlstm_scan(q, k, v, gi, gf, kc, vc, gic, gfc):
    bsz, t, width = q.shape
    ncl, ncc = t // CHUNK, kc.shape[1] // CHUNK
    cf, cb, lf, lb = _scan_specs(ncc, ncl, width, CHUNK)
    gcf, gcb, glf, glb = _scan_specs(ncc, ncl, LANES, CHUNK)
    return pl.pallas_call(
        functools.partial(_mlstm_kernel, ncc=ncc),
        out_shape=[jax.ShapeDtypeStruct((bsz, t, width), F32)] * 2,
        grid=(bsz, ncc + ncl),
        in_specs=[cf, cf, gcf, gcf, cb, cb, gcb, gcb, lf, lf, lf, glf, glf, lb, lb, lb, glb, glb],
        out_specs=[lf, lb],
        scratch_shapes=[pltpu.VMEM((_N_DIR_HEADS, MLSTM_HEAD_DIM, 2 * MLSTM_HEAD_DIM), F32),
                        pltpu.VMEM((8, LANES), F32)],
        compiler_params=_params(("parallel", "arbitrary")),
        name="mlstm_scan",
    )(kc, vc, gic, gfc, kc, vc, gic, gfc, q, k, v, gi, gf, q, k, v, gi, gf)


def _ret_kernel(kc_f, vc_f, kc_b, vc_b, q_f, k_f, v_f, q_b, k_b, v_b, logit_ref, o_f, o_b,
                s_ref, intra_ref, to_end_ref, from_start_ref, *, ncc):
    s = pl.program_id(1)
    units = [(d, h) for d in range(2) for h in range(RET_HEADS)]

    @pl.when(s == 0)
    def _():
        s_ref[...] = jnp.zeros_like(s_ref)
        log_gamma = _log_sigmoid(logit_ref[...])
        r, c = _tri_masks()
        rf, cf = r.astype(F32), c.astype(F32)
        for d, h in units:
            row = d * RET_HEADS + h
            lg = log_gamma[d:d + 1, h:h + 1]
            rel = (rf - cf) if d == 0 else (cf - rf)
            intra_ref[row] = jnp.where(rel >= 0.0, jnp.exp(lg * jnp.maximum(rel, 0.0)), 0.0)
            to_end_ref[row] = jnp.exp(lg * ((CHUNK - 1.0 - rf) if d == 0 else rf))
            from_start_ref[row] = jnp.exp(lg * ((rf + 1.0) if d == 0 else (CHUNK - rf)))

    def run(srcs, with_output):
        qk, qs = {}, {}
        if with_output:
            for d, h in units:
                q = _head_cols(srcs[d][0], h)
                qk[d, h] = _dot_nt(q, _head_cols(srcs[d][1], h))
                qs[d, h] = _dot(q, s_ref[d * RET_HEADS + h].astype(BF16))
            for d, h in units:
                row = d * RET_HEADS + h
                sc = (qk[d, h] * intra_ref[row]).astype(BF16)
                srcs[d][3][:, h * LANES:(h + 1) * LANES] = (_dot(sc, _head_cols(srcs[d][2], h))
                                                            + from_start_ref[row] * qs[d, h])
        for d, h in units:
            row = d * RET_HEADS + h
            kd = (_head_cols(srcs[d][1], h).astype(F32) * to_end_ref[row]).astype(BF16)
            chunk_decay = from_start_ref[row, CHUNK - 1:CHUNK, :] if d == 0 else from_start_ref[row, 0:1, :]
            s_ref[row] = chunk_decay * s_ref[row] + _dot_tn(kd, _head_cols(srcs[d][2], h))

    @pl.when(s < ncc)
    def _():
        run(((None, kc_f, vc_f, None), (None, kc_b, vc_b, None)), False)

    @pl.when(s >= ncc)
    def _():
        run(((q_f, k_f, v_f, o_f), (q_b, k_b, v_b, o_b)), True)


def _ret_scan(q, k, v, kc, vc, logit):
    bsz, t, width = q.shape
    ncl, ncc = t // CHUNK, kc.shape[1] // CHUNK
    cf, cb, lf, lb = _scan_specs(ncc, ncl, width, CHUNK)
    return pl.pallas_call(
        functools.partial(_ret_kernel, ncc=ncc),
        out_shape=[jax.ShapeDtypeStruct((bsz, t, width), F32)] * 2,
        grid=(bsz, ncc + ncl),
        in_specs=[cf, cf, cb, cb, lf, lf, lf, lb, lb, lb, pl.BlockSpec((8, LANES), lambda b, s: (0, 0))],
        out_specs=[lf, lb],
        scratch_shapes=[pltpu.VMEM((2 * RET_HEADS, LANES, RET_V_DIM), F32)]
        + [pltpu.VMEM((2 * RET_HEADS, CHUNK, LANES), F32)] * 3,
        compiler_params=_params(("parallel", "arbitrary")),
        name="retention_scan",
    )(kc, vc, kc, vc, q, k, v, q, k, v, logit)


def _rec_out_kernel(x_ref, mod_ref, mf_ref, mb_ref, rf_ref, rb_ref, o_ref_in, rg_ref, mg_ref, rng_ref, rnb_ref,
                    w_ref, g_ref, b_ref, out_ref):
    hm = mf_ref[...] + mb_ref[...]
    hr = rf_ref[...] + rb_ref[...]
    o_gate = jax.nn.sigmoid(o_ref_in[...])
    rg = rg_ref[...]
    r_gate = rg * jax.nn.sigmoid(rg)
    parts_m, parts_r = [], []
    for h in range(MLSTM_HEADS):
        cols = slice(h * LANES, (h + 1) * LANES)
        parts_m.append(_layer_norm(hm[:, cols], mg_ref[:, cols], None) * o_gate[:, cols])
        parts_r.append(_layer_norm(hr[:, cols], rng_ref[:, cols], rnb_ref[:, cols]) * r_gate[:, cols])
    width = MLSTM_HEADS * MLSTM_HEAD_DIM
    y = (_dot(jnp.concatenate(parts_m, axis=1).astype(BF16), w_ref[0:width, :])
         + _dot(jnp.concatenate(parts_r, axis=1).astype(BF16), w_ref[width:, :]))
    z = ALPHA * x_ref[...] + mod_ref[2:3, :] * y
    out_ref[...] = _layer_norm(z, g_ref[...], b_ref[...])


def _rec_out(x2, mod4, tokens_per_row, mf, mb, rf, rb, o, rg, mg, rng, rnb, w, g, b):
    n = x2.shape[0]
    tm = _token_tile(n, tokens_per_row)
    tile = lambda width: pl.BlockSpec((tm, width), lambda i: (i, 0))
    half = tile(512)
    return pl.pallas_call(
        _rec_out_kernel,
        out_shape=jax.ShapeDtypeStruct((n, D_MODEL), F32),
        grid=(n // tm,),
        in_specs=[tile(D_MODEL), _mod_spec(1, tokens_per_row // tm), half, half, half, half, half, half,
                  _resident((1, 512)), _resident((1, 512)), _resident((1, 512)),
                  _resident(w.shape), _resident((1, D_MODEL)), _resident((1, D_MODEL))],
        out_specs=tile(D_MODEL),
        compiler_params=_params(("parallel",)),
        name="rec_out_proj",
    )(x2, mod4, mf, mb, rf, rb, o, rg, mg, rng, rnb, w, g, b)


def _pad_heads(w, n_heads, axis):
    shape = list(w.shape)
    shape[axis:axis + 1] = [n_heads, shape[axis] // n_heads]
    w = w.reshape(shape)
    pad = [(0, 0)] * w.ndim
    pad[axis + 1] = (0, LANES - shape[axis + 1])
    w = jnp.pad(w, pad)
    shape[axis:axis + 2] = [n_heads * LANES]
    return w.reshape(shape)


def _attn_weights(w_in, w_out):
    aq, ak, av, bq, bk, bv = jnp.split(w_in, [512, 1024, 1536, 2048, 2176], axis=1)
    w = jnp.concatenate([aq, ak, av, _pad_heads(bq, WIN_HEADS, 1), _pad_heads(bk, WIN_KV_HEADS, 1),
                         _pad_heads(bv, WIN_KV_HEADS, 1)], axis=1)
    wa = w_out[:DIFF_HEADS * DIFF_V_DIM]
    wb = _pad_heads(w_out[DIFF_HEADS * DIFF_V_DIM:], WIN_HEADS, 0)
    return w.astype(BF16), wa.astype(BF16), wb.astype(BF16)


def _rec_weights(w_in):
    mq, mk, mv, mo, mg, rq, rk, rv, rg = jnp.split(w_in, [512, 1024, 1536, 2048, 2064, 2320, 2576, 3088], axis=1)
    gi, gf = _split_gates(mg)
    w = jnp.concatenate([mq, mk, mv, mo, _pad_heads(rq, RET_HEADS, 1), _pad_heads(rk, RET_HEADS, 1), rv, rg, gi, gf],
                        axis=1)
    return w.astype(BF16)


def _split_gates(g):
    g = g.reshape(*g.shape[:-1], 2, 2, MLSTM_HEADS)
    pad = [(0, 0)] * (g.ndim - 3) + [(0, LANES - _N_DIR_HEADS)]
    return tuple(jnp.pad(g[..., io, :].reshape(*g.shape[:-3], _N_DIR_HEADS), pad) for io in range(2))


def _rope_tables(t):
    rows = t // GRID_W
    row = jnp.repeat(jnp.arange(rows), GRID_W)
    col = jnp.tile(jnp.arange(GRID_W), rows)
    inv = ROPE_BASE ** (-jnp.arange(ROPE_FREQS, dtype=F32) / ROPE_FREQS)
    ang_r, ang_c = row[:, None] * inv, col[:, None] * inv
    cos = jnp.concatenate([jnp.cos(ang_r)] * 2 + [jnp.cos(ang_c)] * 2, axis=1)
    sin = jnp.concatenate([-jnp.sin(ang_r), jnp.sin(ang_r), -jnp.sin(ang_c), jnp.sin(ang_c)], axis=1)
    return jnp.tile(cos, (1, 2)), jnp.tile(sin, (1, 2))


def kernel(x, c, ctx, c_ctx, ada_w, ada_b, ln_g, ln_b, ffn_w_in, ffn_w_out, attn_w_in, attn_w_out,
           diff_lambda, diff_norm_g, sink_logits, rec_w_in, rec_w_out, mlstm_gate_b, mlstm_norm_g,
           ret_decay_logit, ret_norm_g, ret_norm_b):
    bsz, t, d = x.shape
    lc = ctx.shape[1]
    assert d == D_MODEL and t % ATTN_TK == 0 and lc % CHUNK == 0
    out_dtype = x.dtype
    x2 = x.reshape(bsz * t, d).astype(F32)
    c2 = ctx.reshape(bsz * lc, d).astype(F32)
    mod_rows = 16
    cc = jnp.zeros((mod_rows, d), F32).at[:bsz].set(c).at[bsz].set(c_ctx)
    row2 = lambda v: v.reshape(1, -1).astype(F32)

    for l in range(DEPTH):
        last = l == DEPTH - 1
        i = l // 2
        mod = _modulation(cc, ada_w[l], ada_b[l].reshape(1, -1)).reshape(mod_rows, 3, 3, d)
        mod_x, mod_c = mod[:bsz], mod[bsz:bsz + 1]
        ffn = lambda z, m, sub, tpr, j: _ffn(z, m, sub, tpr, ffn_w_in[l, j].astype(BF16),
                                             ffn_w_out[l, j].astype(BF16), row2(ln_g[l, sub]), row2(ln_b[l, sub]))
        x2 = ffn(x2, mod_x, 0, t, 0)
        c2 = ffn(c2, mod_c, 0, bsz * lc, 0)
        g1, b1 = row2(ln_g[l, 1]), row2(ln_b[l, 1])
        if l % 2 == 0:
            lambda_init = 0.8 - 0.6 * math.exp(-0.3 * l)
            w, wa, wb = _attn_weights(attn_w_in[i], attn_w_out[i])
            aq, ak, av, bq, bk, bv = [a.reshape(bsz, t, -1) for a in _attn_proj(x2, mod_x, t, w, _rope_tables(t))]
            aqc, akc, avc, bqc, bkc, bvc = [a.reshape(bsz, lc, -1) for a in _attn_proj(c2, mod_c, bsz * lc, w, None)]
            avt, avct = jnp.swapaxes(av, 1, 2), jnp.swapaxes(avc, 1, 2)
            lam_vec, sub_g = diff_lambda[i].astype(F32), row2(diff_norm_g[i])
            sink = sink_logits[i].astype(F32)
            a_x = _diff_attn(aq, ak, avt, akc, avct, lam_vec, sub_g, lambda_init)
            b_x = _win_attn(bq, bk, bv, bkc, bvc, sink, True)
            flat = lambda a: a.reshape(-1, a.shape[-1])
            if not last:
                a_c = _diff_attn(aqc, None, None, akc, avct, lam_vec, sub_g, lambda_init)
                b_c = _win_attn(bqc, None, None, bkc, bvc, sink, False)
                c2_mix = _attn_out(c2, mod_c, bsz * lc, flat(a_c), flat(b_c), wa, wb, g1, b1)
            x2 = _attn_out(x2, mod_x, t, flat(a_x), flat(b_x), wa, wb, g1, b1)
        else:
            w = _rec_weights(rec_w_in[i])
            gate_b = jnp.stack(_split_gates(mlstm_gate_b[i].reshape(-1).astype(F32)))
            px = [a.reshape(bsz, t, -1) for a in _rec_proj(x2, mod_x, t, w, gate_b)]
            pc = [a.reshape(bsz, lc, -1) for a in _rec_proj(c2, mod_c, bsz * lc, w, gate_b)]
            mq, mk, mv, mo, rq, rk, rv, rg, gi, gf = px
            _, mkc, mvc, _, _, rkc, rvc, _, gic, gfc = pc
            assert last
            logit = jnp.zeros((8, LANES), F32).at[:2, :RET_HEADS].set(ret_decay_logit[i].astype(F32))
            m_f, m_b = _mlstm_scan(mq, mk, mv, gi, gf, mkc, mvc, gic, gfc)
            r_f, r_b = _ret_scan(rq, rk, rv, rkc, rvc, logit)
            flat = lambda a: a.reshape(-1, a.shape[-1])
            x2 = _rec_out(x2, mod_x, t, flat(m_f), flat(m_b), flat(r_f), flat(r_b), flat(mo), flat(rg),
                          row2(mlstm_norm_g[i]), row2(ret_norm_g[i]), row2(ret_norm_b[i]),
                          rec_w_out[i].astype(BF16), g1, b1)
        x2 = ffn(x2, mod_x, 2, t, 1)
        if not last:
            c2 = ffn(c2_mix, mod_c, 2, bsz * lc, 1)
    return x2.reshape(bsz, t, d).astype(out_dtype)
```

```python
import functools
import math

import jax
import jax.numpy as jnp
from jax import lax
from jax.experimental import pallas as pl
from jax.experimental.pallas import tpu as pltpu

F32 = jnp.float32
BF16 = jnp.bfloat16

D_MODEL = 1024
DEPTH = 2
GRID_W = 64
D_FF = 2816
DIFF_HEADS = 4
DIFF_HEAD_DIM = 64
DIFF_V_DIM = 128
WIN_HEADS = 8
WIN_KV_HEADS = 2
WIN_GROUP = WIN_HEADS // WIN_KV_HEADS
WIN_HEAD_DIM = 64
WINDOW = 128
Q_BLOCK = 128
ROPE_BASE = 10000.0
ROPE_FREQS = 16
MLSTM_HEADS = 4
MLSTM_HEAD_DIM = 128
RET_HEADS = 4
RET_QK_DIM = 64
RET_V_DIM = 128
CHUNK = 128
ALPHA = (2.0 * DEPTH) ** 0.25
EPS = 1e-5
LOG2_E = 1.4426950408889634
NEG_BIG = -1e30

LANES = 128
BF16_SUBLANES = 16
V7X_VMEM_LIMIT_BYTES = 56 * 1024 * 1024

TOKEN_TILE = 512
FF_CHUNK = 256
ATTN_TQ = 512
ATTN_TK = 1024
ATTN_PIECE = 256
WIN_TQ = 256


def _params(semantics):
    return pltpu.CompilerParams(dimension_semantics=semantics, vmem_limit_bytes=V7X_VMEM_LIMIT_BYTES)


def _resident(shape):
    return pl.BlockSpec(shape, lambda *_: (0,) * len(shape), pipeline_mode=pl.Buffered(1))


def _layer_norm(y, g, b):
    mu = jnp.mean(y, axis=-1, keepdims=True)
    yc = y - mu
    var = jnp.mean(yc * yc, axis=-1, keepdims=True)
    out = yc * lax.rsqrt(var + EPS) * g
    return out if b is None else out + b


def _log_sigmoid(x):
    return jnp.minimum(x, 0.0) - jnp.log1p(jnp.exp(-jnp.abs(x)))


def _dot(a, b):
    return jnp.dot(a, b, preferred_element_type=F32)


def _dot_nt(a, b):
    return lax.dot_general(a, b, (((1,), (1,)), ((), ())), preferred_element_type=F32)


def _dot_tn(a, b):
    return lax.dot_general(a, b, (((0,), (0,)), ((), ())), preferred_element_type=F32)


def _mod_kernel(c_ref, w_ref, b_ref, o_ref):
    c = c_ref[...]
    h = c * jax.nn.sigmoid(c)
    o_ref[...] = jnp.dot(h, w_ref[...], preferred_element_type=F32,
                         precision=lax.Precision.HIGHEST) + b_ref[...]


def _modulation(cc, w, b):
    rows, d = cc.shape
    n = w.shape[1]
    tn = 1024
    return pl.pallas_call(
        _mod_kernel,
        out_shape=jax.ShapeDtypeStruct((rows, n), F32),
        grid=(n // tn,),
        in_specs=[pl.BlockSpec((rows, d), lambda j: (0, 0)),
                  pl.BlockSpec((d, tn), lambda j: (0, j)),
                  pl.BlockSpec((1, tn), lambda j: (0, j))],
        out_specs=pl.BlockSpec((rows, tn), lambda j: (0, j)),
        compiler_params=_params(("parallel",)),
        name="modulation",
    )(cc, w, b)


def _token_tile(n_tokens, tokens_per_row):
    tm = min(TOKEN_TILE, tokens_per_row)
    assert tokens_per_row % tm == 0 and n_tokens % tm == 0
    return tm


def _mod_spec(sub, tiles_per_row):
    return pl.BlockSpec((None, None, 3, D_MODEL), lambda i: (i // tiles_per_row, sub, 0, 0))


def _ffn_kernel(x_ref, mod_ref, w_in_ref, w_out_ref, g_ref, b_ref, o_ref, acc_ref):
    x = x_ref[...]
    shift, scale, gate = mod_ref[0:1, :], mod_ref[1:2, :], mod_ref[2:3, :]
    h = (x * (1.0 + scale) + shift).astype(BF16)
    for c in range(D_FF // FF_CHUNK):
        lo = c * FF_CHUNK
        gt = _dot(h, w_in_ref[:, lo:lo + FF_CHUNK])
        up = _dot(h, w_in_ref[:, D_FF + lo:D_FF + lo + FF_CHUNK])
        act = (gt * jax.nn.sigmoid(gt) * up).astype(BF16)
        part = _dot(act, w_out_ref[lo:lo + FF_CHUNK, :])
        if c == 0:
            acc_ref[...] = part
        else:
            acc_ref[...] += part
    y = ALPHA * x + 0.5 * gate * acc_ref[...]
    o_ref[...] = _layer_norm(y, g_ref[...], b_ref[...])


def _ffn(x2, mod4, sub, tokens_per_row, w_in, w_out, g, b):
    n = x2.shape[0]
    tm = _token_tile(n, tokens_per_row)
    tile = pl.BlockSpec((tm, D_MODEL), lambda i: (i, 0))
    return pl.pallas_call(
        _ffn_kernel,
        out_shape=jax.ShapeDtypeStruct((n, D_MODEL), F32),
        grid=(n // tm,),
        in_specs=[tile, _mod_spec(sub, tokens_per_row // tm),
                  _resident((D_MODEL, 2 * D_FF)), _resident((D_FF, D_MODEL)),
                  _resident((1, D_MODEL)), _resident((1, D_MODEL))],
        out_specs=tile,
        scratch_shapes=[pltpu.VMEM((tm, D_MODEL), F32)],
        compiler_params=_params(("parallel",)),
        name="ffn_sublayer",
    )(x2, mod4, w_in, w_out, g, b)


def _rope(p, cos, sin):
    lane = lax.broadcasted_iota(jnp.int32, (p.shape[0], LANES), 1)
    first_half = (lane & ROPE_FREQS) == 0
    outs = []
    for j in range(p.shape[1] // LANES):
        xs = p[:, j * LANES:(j + 1) * LANES]
        partner = jnp.where(first_half, pltpu.roll(xs, LANES - ROPE_FREQS, 1), pltpu.roll(xs, ROPE_FREQS, 1))
        outs.append(xs * cos + partner * sin)
    return jnp.concatenate(outs, axis=1)


_ATTN_COLS = (("aq", 512), ("ak", 512), ("av", 512), ("bq", 512), ("bk", 256), ("bv", 128))
_ATTN_GROUPS = (("aq",), ("ak",), ("av",), ("bq",), ("bk", "bv"))


def _attn_proj_kernel(*refs, rope):
    if rope:
        x_ref, mod_ref, w_ref, cos_ref, sin_ref = refs[:5]
        outs = refs[5:]
        cos, sin = cos_ref[...], sin_ref[...]
    else:
        x_ref, mod_ref, w_ref = refs[:3]
        outs = refs[3:]
    x = x_ref[...]
    shift, scale = mod_ref[0:1, :], mod_ref[1:2, :]
    h = (x * (1.0 + scale) + shift).astype(BF16)
    widths = dict(_ATTN_COLS)
    out_refs = dict(zip(widths, outs))
    lo = 0
    for group in _ATTN_GROUPS:
        total = sum(widths[name] for name in group)
        pg = _dot(h, w_ref[:, lo:lo + total])
        lo += total
        off = 0
        for name in group:
            p = pg[:, off:off + widths[name]]
            off += widths[name]
            if rope and name in ("aq", "ak", "bq", "bk"):
                p = _rope(p, cos, sin)
            if name in ("aq", "bq"):
                p = p * (DIFF_HEAD_DIM ** -0.5 * LOG2_E)
            out_refs[name][...] = p.astype(BF16)


def _attn_proj(x2, mod4, tokens_per_row, w, tables):
    n = x2.shape[0]
    tm = _token_tile(n, tokens_per_row)
    rope = tables is not None
    tile = lambda width: pl.BlockSpec((tm, width), lambda i: (i, 0))
    in_specs = [tile(D_MODEL), _mod_spec(1, tokens_per_row // tm), _resident(w.shape)]
    args = [x2, mod4, w]
    if rope:
        tpr = tokens_per_row // tm
        tab = pl.BlockSpec((tm, LANES), lambda i: (i % tpr, 0))
        in_specs += [tab, tab]
        args += list(tables)
    return pl.pallas_call(
        functools.partial(_attn_proj_kernel, rope=rope),
        out_shape=[jax.ShapeDtypeStruct((n, width), BF16) for _, width in _ATTN_COLS],
        grid=(n // tm,),
        in_specs=in_specs,
        out_specs=[tile(width) for _, width in _ATTN_COLS],
        compiler_params=_params(("parallel",)),
        name="attn_in_proj",
    )(*args)


def _attn_out_kernel(x_ref, mod_ref, a_ref, b_ref, wa_ref, wb_ref, g_ref, bb_ref, o_ref):
    y = _dot(a_ref[...], wa_ref[...]) + _dot(b_ref[...], wb_ref[...])
    z = ALPHA * x_ref[...] + mod_ref[2:3, :] * y
    o_ref[...] = _layer_norm(z, g_ref[...], bb_ref[...])


def _attn_out(x2, mod4, tokens_per_row, a, b, wa, wb, g, bb):
    n = x2.shape[0]
    tm = _token_tile(n, tokens_per_row)
    tile = lambda width: pl.BlockSpec((tm, width), lambda i: (i, 0))
    return pl.pallas_call(
        _attn_out_kernel,
        out_shape=jax.ShapeDtypeStruct((n, D_MODEL), F32),
        grid=(n // tm,),
        in_specs=[tile(D_MODEL), _mod_spec(1, tokens_per_row // tm), tile(a.shape[1]), tile(b.shape[1]),
                  _resident(wa.shape), _resident(wb.shape), _resident((1, D_MODEL)), _resident((1, D_MODEL))],
        out_specs=tile(D_MODEL),
        compiler_params=_params(("parallel",)),
        name="attn_out_proj",
    )(x2, mod4, a, b, wa, wb, g, bb)


def _diff_attn_kernel(*refs, n_lat, lc, lambda_init):
    if n_lat:
        q_ref, k_ref, vt_ref, kc_ref, vct_ref, lam_ref, g_ref, o_ref = refs
    else:
        q_ref, kc_ref, vct_ref, lam_ref, g_ref, o_ref = refs
    q = q_ref[...]
    chunks = [(k_ref, vt_ref, c * ATTN_TK, ATTN_TK) for c in range(n_lat)] + [(kc_ref, vct_ref, 0, lc)]
    m, acc = [None, None], [None, None]

    def n_pieces(chunk):
        return chunk[3] // ATTN_PIECE

    def scores(chunk, piece, sub):
        kr, _, lo, _ = chunk
        kblk = kr[lo + piece * ATTN_PIECE:lo + (piece + 1) * ATTN_PIECE, :]
        lane = lax.broadcasted_iota(jnp.int32, kblk.shape, 1)
        keep = (lane < DIFF_HEAD_DIM) if sub == 0 else (lane >= DIFF_HEAD_DIM)
        return _dot_nt(jnp.where(keep, kblk, jnp.zeros_like(kblk)), q)

    s_next = [[scores(chunks[0], j, sub) for j in range(n_pieces(chunks[0]))] for sub in range(2)]
    for idx, (kr, vr, lo, size) in enumerate(chunks):
        vt = jnp.concatenate([vr[:, lo:lo + size], jnp.ones((BF16_SUBLANES, size), BF16)], axis=0)
        nxt = chunks[idx + 1] if idx + 1 < len(chunks) else None
        for sub in range(2):
            s_cur, s_new = s_next[sub], []
            smax = functools.reduce(jnp.maximum, [jnp.max(s, axis=0, keepdims=True) for s in s_cur])
            m_new = smax if idx == 0 else jnp.maximum(m[sub], smax)
            ps = []
            for j, s in enumerate(s_cur):
                if nxt is not None and j < n_pieces(nxt):
                    s_new.append(scores(nxt, j, sub))
                ps.append(jnp.exp2(s - m_new).astype(BF16))
            if nxt is not None:
                s_new += [scores(nxt, j, sub) for j in range(len(s_new), n_pieces(nxt))]
            part = _dot(vt, jnp.concatenate(ps, axis=0))
            acc[sub] = part if idx == 0 else jnp.exp2(m[sub] - m_new) * acc[sub] + part
            m[sub], s_next[sub] = m_new, s_new
    t = lam_ref[...]
    lam = (jnp.exp(jnp.sum(t[0:1, :] * t[1:2, :], axis=1, keepdims=True))
           - jnp.exp(jnp.sum(t[2:3, :] * t[3:4, :], axis=1, keepdims=True)) + lambda_init)
    num = [a[:DIFF_V_DIM, :] * (1.0 / a[DIFF_V_DIM:DIFF_V_DIM + 1, :]) for a in acc]
    o = (num[0] - lam * num[1]).T
    o = o * lax.rsqrt(jnp.mean(o * o, axis=-1, keepdims=True) + EPS) * g_ref[...]
    o_ref[...] = (o * (1.0 - lambda_init)).astype(BF16)


def _diff_attn(q, k, vt, kc, vct, lam_vec, g, lambda_init):
    bsz, tq_total, _ = q.shape
    lc = kc.shape[1]
    tq = min(ATTN_TQ, tq_total)
    n_lat = 0 if k is None else k.shape[1] // ATTN_TK
    assert tq_total % tq == 0 and (k is None or k.shape[1] % ATTN_TK == 0)
    qspec = pl.BlockSpec((None, tq, LANES), lambda b, h, i: (b, i, h))
    in_specs, args = [qspec], [q]
    if n_lat:
        t = k.shape[1]
        in_specs += [pl.BlockSpec((None, t, LANES), lambda b, h, i: (b, 0, h)),
                     pl.BlockSpec((None, LANES, t), lambda b, h, i: (b, h, 0))]
        args += [k, vt]
    in_specs += [pl.BlockSpec((None, lc, LANES), lambda b, h, i: (b, 0, h)),
                 pl.BlockSpec((None, LANES, lc), lambda b, h, i: (b, h, 0)),
                 pl.BlockSpec((4, DIFF_HEAD_DIM), lambda b, h, i: (0, 0)),
                 pl.BlockSpec((1, DIFF_V_DIM), lambda b, h, i: (0, 0))]
    args += [kc, vct, lam_vec, g]
    return pl.pallas_call(
        functools.partial(_diff_attn_kernel, n_lat=n_lat, lc=lc, lambda_init=lambda_init),
        out_shape=jax.ShapeDtypeStruct((bsz, tq_total, DIFF_HEADS * DIFF_V_DIM), BF16),
        grid=(bsz, DIFF_HEADS, tq_total // tq),
        in_specs=in_specs,
        out_specs=qspec,
        compiler_params=_params(("parallel", "parallel", "parallel")),
        name="diff_attention",
    )(*args)


def _win_attn_kernel(*refs, has_window, lc, tq):
    hd = WIN_HEAD_DIM
    if has_window:
        n_blk = tq // Q_BLOCK + 2
        q_ref = refs[0]
        k_refs, v_refs = refs[1:1 + n_blk], refs[1 + n_blk:1 + 2 * n_blk]
        kc_ref, vc_ref, sink_ref, o_ref = refs[1 + 2 * n_blk:]
        kw = jnp.concatenate([ref[...] for ref in k_refs] + [kc_ref[...]], axis=0)
        vt = jnp.concatenate([ref[...] for ref in v_refs] + [vc_ref[...]], axis=1)
        span = tq + 2 * WINDOW
        nk = span + lc
        i = pl.program_id(1)
        last = pl.num_programs(1) - 1
        c = lax.broadcasted_iota(jnp.int32, (nk, tq), 0)
        r = lax.broadcasted_iota(jnp.int32, (nk, tq), 1)
        lo_valid = jnp.where(i == 0, WINDOW, 0)
        hi_valid = jnp.where(i == last, WINDOW + tq, span)
        in_win = (c >= jnp.maximum(r, lo_valid)) & (c <= r + 2 * WINDOW) & (c < hi_valid)
        bias = jnp.where(in_win | (c >= span), 0.0, NEG_BIG)
        bias = jnp.concatenate([bias, bias], axis=1)
    else:
        q_ref, kc_ref, vc_ref, sink_ref, o_ref = refs
        kw, vt = kc_ref[...], vc_ref[...]
        nk = lc
        bias = None
    lane = lax.broadcasted_iota(jnp.int32, (nk, LANES), 1)
    zero = jnp.zeros((nk, LANES), BF16)
    ones = jnp.ones((BF16_SUBLANES, nk), BF16)
    units = [(kh, p) for kh in range(WIN_KV_HEADS) for p in range(2)]
    heads = {(kh, p): [kh * WIN_GROUP + 2 * j + p for j in range(2)] for kh, p in units}

    def scores(kh, p):
        ksrc = kw[:, (0 if p == kh else 1) * LANES:(1 if p == kh else 2) * LANES]
        kmat = jnp.where((lane < hd) if p == 0 else (lane >= hd), ksrc, zero)
        qcat = jnp.concatenate([q_ref[:, (h // 2) * LANES:(h // 2 + 1) * LANES] for h in heads[kh, p]], axis=0)
        return _dot_nt(kmat, qcat)

    out_t = {}
    s_next = scores(*units[0])
    for idx, (kh, p) in enumerate(units):
        s = s_next if bias is None else s_next + bias
        if idx + 1 < len(units):
            s_next = scores(*units[idx + 1])
        sink = jnp.concatenate([jnp.full((1, tq), sink_ref[h] * LOG2_E, F32) for h in heads[kh, p]], axis=1)
        m = jnp.maximum(jnp.max(s, axis=0, keepdims=True), sink)
        vt_ext = jnp.concatenate([vt[kh * hd:(kh + 1) * hd, :], ones], axis=0)
        acc = _dot(vt_ext, jnp.exp2(s - m).astype(BF16))
        denom = acc[hd:hd + 1, :] + jnp.exp2(sink - m)
        out_t[kh, p] = acc[:hd, :] * (1.0 / denom)
    for kh in range(WIN_KV_HEADS):
        for j in range(2):
            pair = kh * 2 + j
            pair_t = jnp.concatenate([out_t[kh, p][:, j * tq:(j + 1) * tq] for p in range(2)], axis=0)
            o_ref[:, pair * LANES:(pair + 1) * LANES] = pair_t.T.astype(BF16)


def _win_attn(q, k, vt, kc, vct, sink, has_window):
    bsz, tq_total, width = q.shape
    lc = kc.shape[1]
    tq = min(WIN_TQ, tq_total)
    assert tq_total % tq == 0 and tq % Q_BLOCK == 0
    nq = tq_total // tq
    per = tq // Q_BLOCK
    n_kblk = tq_total // Q_BLOCK
    qspec = pl.BlockSpec((None, tq, width), lambda b, i: (b, i, 0))
    in_specs, args = [qspec], [q]
    if has_window:
        def kblock(j):
            return lambda b, i: (b, jnp.clip(i * per + j - 1, 0, n_kblk - 1))
        blk = [kblock(j) for j in range(per + 2)]
        in_specs += [pl.BlockSpec((None, Q_BLOCK, 2 * LANES), lambda b, i, f=f: (*f(b, i), 0)) for f in blk]
        in_specs += [pl.BlockSpec((None, LANES, Q_BLOCK), lambda b, i, f=f: (f(b, i)[0], 0, f(b, i)[1])) for f in blk]
        args += [k] * len(blk) + [vt] * len(blk)
    in_specs += [pl.BlockSpec((None, lc, 2 * LANES), lambda b, i: (b, 0, 0)),
                 pl.BlockSpec((None, LANES, lc), lambda b, i: (b, 0, 0)),
                 pl.BlockSpec(memory_space=pltpu.SMEM)]
    args += [kc, vct, sink]
    return pl.pallas_call(
        functools.partial(_win_attn_kernel, has_window=has_window, lc=lc, tq=tq),
        out_shape=jax.ShapeDtypeStruct((bsz, tq_total, width), BF16),
        grid=(bsz, nq),
        in_specs=in_specs,
        out_specs=qspec,
        compiler_params=_params(("parallel", "parallel")),
        name="window_attention",
    )(*args)


_REC_COLS = (("mq", 512, BF16), ("mk", 512, BF16), ("mv", 512, BF16), ("mo", 512, BF16),
             ("rq", 512, BF16), ("rk", 512, BF16), ("rv", 512, BF16), ("rg", 512, BF16),
             ("gi", 128, F32), ("gf", 128, F32))
_N_DIR_HEADS = 2 * MLSTM_HEADS


def _rec_proj_kernel(x_ref, mod_ref, w_ref, gate_b_ref, *outs):
    x = x_ref[...]
    shift, scale = mod_ref[0:1, :], mod_ref[1:2, :]
    h = (x * (1.0 + scale) + shift).astype(BF16)
    lo = 0
    for (name, width, dtype), o_ref in zip(_REC_COLS, outs):
        p = _dot(h, w_ref[:, lo:lo + width])
        lo += width
        if name == "mk":
            p = p * (MLSTM_HEAD_DIM ** -0.5)
        elif name == "rk":
            p = p * (RET_QK_DIM ** -0.5)
        elif name == "gi":
            p = p + gate_b_ref[0:1, :]
        elif name == "gf":
            p = _log_sigmoid(p + gate_b_ref[1:2, :])
        o_ref[...] = p.astype(dtype)


def _rec_proj(x2, mod4, tokens_per_row, w, gate_b):
    n = x2.shape[0]
    tm = _token_tile(n, tokens_per_row)
    tile = lambda width: pl.BlockSpec((tm, width), lambda i: (i, 0))
    return pl.pallas_call(
        _rec_proj_kernel,
        out_shape=[jax.ShapeDtypeStruct((n, width), dtype) for _, width, dtype in _REC_COLS],
        grid=(n // tm,),
        in_specs=[tile(D_MODEL), _mod_spec(1, tokens_per_row // tm), _resident(w.shape), _resident((2, LANES))],
        out_specs=[tile(width) for _, width, _ in _REC_COLS],
        compiler_params=_params(("parallel",)),
        name="rec_in_proj",
    )(x2, mod4, w, gate_b)


def _scan_specs(ncc, ncl, width, rows):
    ctx_f = pl.BlockSpec((None, rows, width), lambda b, s: (b, jnp.minimum(s, ncc - 1), 0))
    ctx_b = pl.BlockSpec((None, rows, width), lambda b, s: (b, jnp.maximum(ncc - 1 - s, 0), 0))
    lat_f = pl.BlockSpec((None, rows, width), lambda b, s: (b, jnp.maximum(s - ncc, 0), 0))
    lat_b = pl.BlockSpec((None, rows, width), lambda b, s: (b, jnp.minimum(ncl - 1 - (s - ncc), ncl - 1), 0))
    return ctx_f, ctx_b, lat_f, lat_b


def _tri_masks():
    r = lax.broadcasted_iota(jnp.int32, (CHUNK, CHUNK), 0)
    c = lax.broadcasted_iota(jnp.int32, (CHUNK, CHUNK), 1)
    return r, c


def _head_cols(ref, h):
    return ref[:, h * LANES:(h + 1) * LANES]


def _mlstm_step(srcs, masks, tri, c_ref, m_ref, with_output):
    units = [(d, h) for d in range(2) for h in range(MLSTM_HEADS)]
    dv = MLSTM_HEAD_DIM
    ones = jnp.ones((CHUNK, LANES), BF16)
    v_ext = {(d, h): jnp.concatenate([_head_cols(srcs[d][2], h), ones], axis=1) for d, h in units}
    gate = []
    for d in range(2):
        gi, gf = srcs[d][3][...], srcs[d][4][...]
        cum = jnp.dot(tri[d], gf, preferred_element_type=F32, precision=lax.Precision.HIGHEST)
        b_end = cum[CHUNK - 1:CHUNK, :] if d == 0 else cum[0:1, :]
        m_prev = m_ref[d:d + 1, :]
        w_end = b_end - cum + gi
        m_new = jnp.maximum(b_end + m_prev, jnp.max(w_end, axis=0, keepdims=True))
        decay = jnp.exp(b_end + m_prev - m_new)
        w = jnp.exp(w_end - m_new)
        m_ref[d:d + 1, :] = m_new
        log_inter = cum + m_prev if with_output else None
        key_term = (gi - cum).T if with_output else None
        gate.append((cum, decay, w, log_inter, key_term))
    if with_output:
        qk, qcn, log_d, li, rmax, m_t, sm, w_inter, pv = {}, {}, {}, {}, {}, {}, {}, {}, {}
        for d, h in units:
            q = _head_cols(srcs[d][0], h)
            qk[d, h] = _dot_nt(q, _head_cols(srcs[d][1], h))
            qcn[d, h] = _dot(q, c_ref[d * MLSTM_HEADS + h].astype(BF16))
        for d, h in units:
            lane = d * MLSTM_HEADS + h
            cum, _, _, log_inter, key_term = gate[d]
            log_d[d, h] = jnp.where(masks[d], cum[:, lane:lane + 1] + key_term[lane:lane + 1, :], NEG_BIG)
            li[d, h] = log_inter[:, lane:lane + 1]
        for u in units:
            rmax[u] = jnp.max(log_d[u], axis=1, keepdims=True)
        for u in units:
            m_t[u] = jnp.maximum(li[u], rmax[u])
            sm[u] = (qk[u] * jnp.exp(log_d[u] - m_t[u])).astype(BF16)
            w_inter[u] = jnp.exp(li[u] - m_t[u])
        for u in units:
            pv[u] = _dot(sm[u], v_ext[u])
        for d, h in units:
            u = (d, h)
            both = pv[u] + w_inter[u] * qcn[u]
            den = jnp.maximum(jnp.abs(both[:, dv:]), jnp.exp(-m_t[u]))
            srcs[d][5][:, h * LANES:(h + 1) * LANES] = (both[:, :dv] * (1.0 / den)).astype(BF16)
    for d, h in units:
        row = d * MLSTM_HEADS + h
        _, decay, w, _, _ = gate[d]
        kw = (_head_cols(srcs[d][1], h).astype(F32) * w[:, row:row + 1]).astype(BF16)
        c_ref[row] = decay[:, row:row + 1] * c_ref[row] + _dot_tn(kw, v_ext[d, h])


def _ret_tables(logit_ref, intra_ref, to_end_ref, from_start_ref):
    log_gamma = _log_sigmoid(logit_ref[...])
    r, c = _tri_masks()
    rf, cf = r.astype(F32), c.astype(F32)
    for d in range(2):
        for h in range(RET_HEADS):
            row = d * RET_HEADS + h
            lg = log_gamma[d:d + 1, h:h + 1]
            rel = (rf - cf) if d == 0 else (cf - rf)
            intra_ref[row] = jnp.where(rel >= 0.0, jnp.exp(lg * jnp.maximum(rel, 0.0)), 0.0)
            to_end_ref[row] = jnp.exp(lg * ((CHUNK - 1.0 - rf) if d == 0 else rf))
            from_start_ref[row] = jnp.exp(lg * ((rf + 1.0) if d == 0 else (CHUNK - rf)))


def _ret_step(srcs, s_ref, intra_ref, to_end_ref, from_start_ref, with_output):
    units = [(d, h) for d in range(2) for h in range(RET_HEADS)]
    qk, qs = {}, {}
    if with_output:
        for d, h in units:
            q = _head_cols(srcs[d][0], h)
            qk[d, h] = _dot_nt(q, _head_cols(srcs[d][1], h))
            qs[d, h] = _dot(q, s_ref[d * RET_HEADS + h].astype(BF16))
        for d, h in units:
            row = d * RET_HEADS + h
            sc = (qk[d, h] * intra_ref[row]).astype(BF16)
            out = _dot(sc, _head_cols(srcs[d][2], h)) + from_start_ref[row] * qs[d, h]
            srcs[d][3][:, h * LANES:(h + 1) * LANES] = out.astype(BF16)
    for d, h in units:
        row = d * RET_HEADS + h
        kd = (_head_cols(srcs[d][1], h).astype(F32) * to_end_ref[row]).astype(BF16)
        chunk_decay = from_start_ref[row, CHUNK - 1:CHUNK, :] if d == 0 else from_start_ref[row, 0:1, :]
        s_ref[row] = chunk_decay * s_ref[row] + _dot_tn(kd, _head_cols(srcs[d][2], h))


def _rec_scan_kernel(mkc_f, mvc_f, gic_f, gfc_f, mkc_b, mvc_b, gic_b, gfc_b,
                     mq_f, mk_f, mv_f, gi_f, gf_f, mq_b, mk_b, mv_b, gi_b, gf_b,
                     rkc_f, rvc_f, rkc_b, rvc_b, rq_f, rk_f, rv_f, rq_b, rk_b, rv_b, logit_ref,
                     mo_f, mo_b, ro_f, ro_b,
                     c_ref, m_ref, s_ref, intra_ref, to_end_ref, from_start_ref, *, ncc):
    s = pl.program_id(1)

    @pl.when(s == 0)
    def _():
        c_ref[...] = jnp.zeros_like(c_ref)
        m_ref[...] = jnp.zeros_like(m_ref)
        s_ref[...] = jnp.zeros_like(s_ref)
        _ret_tables(logit_ref, intra_ref, to_end_ref, from_start_ref)

    r, c = _tri_masks()
    masks = (c <= r, c >= r)
    tri = (jnp.where(masks[0], 1.0, 0.0), jnp.where(masks[1], 1.0, 0.0))
    tables = (s_ref, intra_ref, to_end_ref, from_start_ref)

    @pl.when(s < ncc)
    def _():
        _mlstm_step(((None, mkc_f, mvc_f, gic_f, gfc_f, None), (None, mkc_b, mvc_b, gic_b, gfc_b, None)),
                    masks, tri, c_ref, m_ref, False)
        _ret_step(((None, rkc_f, rvc_f, None), (None, rkc_b, rvc_b, None)), *tables, False)

    @pl.when(s >= ncc)
    def _():
        _mlstm_step(((mq_f, mk_f, mv_f, gi_f, gf_f, mo_f), (mq_b, mk_b, mv_b, gi_b, gf_b, mo_b)),
                    masks, tri, c_ref, m_ref, True)
        _ret_step(((rq_f, rk_f, rv_f, ro_f), (rq_b, rk_b, rv_b, ro_b)), *tables, True)


def _rec_scan(mq, mk, mv, gi, gf, rq, rk, rv, mkc, mvc, gic, gfc, rkc, rvc, logit):
    bsz, t, width = mq.shape
    ncl, ncc = t // CHUNK, mkc.shape[1] // CHUNK
    cf, cb, lf, lb = _scan_specs(ncc, ncl, width, CHUNK)
    gcf, gcb, glf, glb = _scan_specs(ncc, ncl, LANES, CHUNK)
    return pl.pallas_call(
        functools.partial(_rec_scan_kernel, ncc=ncc),
        out_shape=[jax.ShapeDtypeStruct((bsz, t, width), BF16)] * 4,
        grid=(bsz, ncc + ncl),
        in_specs=[cf, cf, gcf, gcf, cb, cb, gcb, gcb, lf, lf, lf, glf, glf, lb, lb, lb, glb, glb,
                  cf, cf, cb, cb, lf, lf, lf, lb, lb, lb, pl.BlockSpec((8, LANES), lambda b, s: (0, 0))],
        out_specs=[lf, lb, lf, lb],
        scratch_shapes=[pltpu.VMEM((_N_DIR_HEADS, MLSTM_HEAD_DIM, 2 * MLSTM_HEAD_DIM), F32),
                        pltpu.VMEM((8, LANES), F32),
                        pltpu.VMEM((2 * RET_HEADS, LANES, RET_V_DIM), F32)]
        + [pltpu.VMEM((2 * RET_HEADS, CHUNK, LANES), F32)] * 3,
        compiler_params=_params(("parallel", "arbitrary")),
        name="rec_scan",
    )(mkc, mvc, gic, gfc, mkc, mvc, gic, gfc, mq, mk, mv, gi, gf, mq, mk, mv, gi, gf,
      rkc, rvc, rkc, rvc, rq, rk, rv, rq, rk, rv, logit)


def _rec_out_kernel(x_ref, mod_ref, mf_ref, mb_ref, rf_ref, rb_ref, o_ref_in, rg_ref, mg_ref, rng_ref, rnb_ref,
                    w_ref, g_ref, b_ref, out_ref):
    hm = mf_ref[...].astype(F32) + mb_ref[...].astype(F32)
    hr = rf_ref[...].astype(F32) + rb_ref[...].astype(F32)
    o_gate = jax.nn.sigmoid(o_ref_in[...].astype(F32))
    rg = rg_ref[...].astype(F32)
    r_gate = rg * jax.nn.sigmoid(rg)
    parts_m, parts_r = [], []
    for h in range(MLSTM_HEADS):
        cols = slice(h * LANES, (h + 1) * LANES)
        parts_m.append(_layer_norm(hm[:, cols], mg_ref[:, cols], None) * o_gate[:, cols])
        parts_r.append(_layer_norm(hr[:, cols], rng_ref[:, cols], rnb_ref[:, cols]) * r_gate[:, cols])
    width = MLSTM_HEADS * MLSTM_HEAD_DIM
    y = (_dot(jnp.concatenate(parts_m, axis=1).astype(BF16), w_ref[0:width, :])
         + _dot(jnp.concatenate(parts_r, axis=1).astype(BF16), w_ref[width:, :]))
    z = ALPHA * x_ref[...] + mod_ref[2:3, :] * y
    out_ref[...] = _layer_norm(z, g_ref[...], b_ref[...])


def _rec_out(x2, mod4, tokens_per_row, mf, mb, rf, rb, o, rg, mg, rng, rnb, w, g, b):
    n = x2.shape[0]
    tm = _token_tile(n, tokens_per_row)
    tile = lambda width: pl.BlockSpec((tm, width), lambda i: (i, 0))
    half = tile(512)
    return pl.pallas_call(
        _rec_out_kernel,
        out_shape=jax.ShapeDtypeStruct((n, D_MODEL), F32),
        grid=(n // tm,),
        in_specs=[tile(D_MODEL), _mod_spec(1, tokens_per_row // tm), half, half, half, half, half, half,
                  _resident((1, 512)), _resident((1, 512)), _resident((1, 512)),
                  _resident(w.shape), _resident((1, D_MODEL)), _resident((1, D_MODEL))],
        out_specs=tile(D_MODEL),
        compiler_params=_params(("parallel",)),
        name="rec_out_proj",
    )(x2, mod4, mf, mb, rf, rb, o, rg, mg, rng, rnb, w, g, b)


def _pad_heads(w, n_heads, axis):
    shape = list(w.shape)
    shape[axis:axis + 1] = [n_heads, shape[axis] // n_heads]
    w = w.reshape(shape)
    pad = [(0, 0)] * w.ndim
    pad[axis + 1] = (0, LANES - shape[axis + 1])
    w = jnp.pad(w, pad)
    shape[axis:axis + 2] = [n_heads * LANES]
    return w.reshape(shape)


def _attn_weights(w_in, w_out):
    aq, ak, av, bq, bk, bv = jnp.split(w_in, [512, 1024, 1536, 2048, 2176], axis=1)
    bk_swapped = jnp.concatenate([bk[:, WIN_HEAD_DIM:], bk[:, :WIN_HEAD_DIM]], axis=1)
    w = jnp.concatenate([aq, ak, av, bq, bk, bk_swapped, bv], axis=1)
    wa, wb = w_out[:DIFF_HEADS * DIFF_V_DIM], w_out[DIFF_HEADS * DIFF_V_DIM:]
    return w.astype(BF16), wa.astype(BF16), wb.astype(BF16)


def _rec_weights(w_in):
    mq, mk, mv, mo, mg, rq, rk, rv, rg = jnp.split(w_in, [512, 1024, 1536, 2048, 2064, 2320, 2576, 3088], axis=1)
    gi, gf = _split_gates(mg)
    w = jnp.concatenate([mq, mk, mv, mo, _pad_heads(rq, RET_HEADS, 1), _pad_heads(rk, RET_HEADS, 1), rv, rg, gi, gf],
                        axis=1)
    return w.astype(BF16)


def _split_gates(g):
    g = g.reshape(*g.shape[:-1], 2, 2, MLSTM_HEADS)
    pad = [(0, 0)] * (g.ndim - 3) + [(0, LANES - _N_DIR_HEADS)]
    return tuple(jnp.pad(g[..., io, :].reshape(*g.shape[:-3], _N_DIR_HEADS), pad) for io in range(2))


def _rope_tables(t):
    rows = t // GRID_W
    row = jnp.repeat(jnp.arange(rows), GRID_W)
    col = jnp.tile(jnp.arange(GRID_W), rows)
    inv = ROPE_BASE ** (-jnp.arange(ROPE_FREQS, dtype=F32) / ROPE_FREQS)
    ang_r, ang_c = row[:, None] * inv, col[:, None] * inv
    cos = jnp.concatenate([jnp.cos(ang_r)] * 2 + [jnp.cos(ang_c)] * 2, axis=1)
    sin = jnp.concatenate([-jnp.sin(ang_r), jnp.sin(ang_r), -jnp.sin(ang_c), jnp.sin(ang_c)], axis=1)
    return jnp.tile(cos, (1, 2)), jnp.tile(sin, (1, 2))


def kernel(x, c, ctx, c_ctx, ada_w, ada_b, ln_g, ln_b, ffn_w_in, ffn_w_out, attn_w_in, attn_w_out,
           diff_lambda, diff_norm_g, sink_logits, rec_w_in, rec_w_out, mlstm_gate_b, mlstm_norm_g,
           ret_decay_logit, ret_norm_g, ret_norm_b):
    bsz, t, d = x.shape
    lc = ctx.shape[1]
    assert d == D_MODEL and t % ATTN_TK == 0 and lc % CHUNK == 0
    out_dtype = x.dtype
    x2 = x.reshape(bsz * t, d).astype(F32)
    c2 = ctx.reshape(bsz * lc, d).astype(F32)
    mod_rows = 16
    cc = jnp.zeros((mod_rows, d), F32).at[:bsz].set(c).at[bsz].set(c_ctx)
    row2 = lambda v: v.reshape(1, -1).astype(F32)

    for l in range(DEPTH):
        last = l == DEPTH - 1
        i = l // 2
        mod = _modulation(cc, ada_w[l], ada_b[l].reshape(1, -1)).reshape(mod_rows, 3, 3, d)
        mod_x, mod_c = mod[:bsz], mod[bsz:bsz + 1]
        ffn = lambda z, m, sub, tpr, j: _ffn(z, m, sub, tpr, ffn_w_in[l, j].astype(BF16),
                                             ffn_w_out[l, j].astype(BF16), row2(ln_g[l, sub]), row2(ln_b[l, sub]))
        x2 = ffn(x2, mod_x, 0, t, 0)
        c2 = ffn(c2, mod_c, 0, bsz * lc, 0)
        g1, b1 = row2(ln_g[l, 1]), row2(ln_b[l, 1])
        if l % 2 == 0:
            lambda_init = 0.8 - 0.6 * math.exp(-0.3 * l)
            w, wa, wb = _attn_weights(attn_w_in[i], attn_w_out[i])
            aq, ak, av, bq, bk, bv = [a.reshape(bsz, t, -1) for a in _attn_proj(x2, mod_x, t, w, _rope_tables(t))]
            aqc, akc, avc, bqc, bkc, bvc = [a.reshape(bsz, lc, -1) for a in _attn_proj(c2, mod_c, bsz * lc, w, None)]
            avt, avct = jnp.swapaxes(av, 1, 2), jnp.swapaxes(avc, 1, 2)
            bvt, bvct = jnp.swapaxes(bv, 1, 2), jnp.swapaxes(bvc, 1, 2)
            lam_vec, sub_g = diff_lambda[i].astype(F32), row2(diff_norm_g[i])
            sink = sink_logits[i].astype(F32)
            a_x = _diff_attn(aq, ak, avt, akc, avct, lam_vec, sub_g, lambda_init)
            b_x = _win_attn(bq, bk, bvt, bkc, bvct, sink, True)
            flat = lambda a: a.reshape(-1, a.shape[-1])
            if not last:
                a_c = _diff_attn(aqc, None, None, akc, avct, lam_vec, sub_g, lambda_init)
                b_c = _win_attn(bqc, None, None, bkc, bvct, sink, False)
                c2_mix = _attn_out(c2, mod_c, bsz * lc, flat(a_c), flat(b_c), wa, wb, g1, b1)
            x2 = _attn_out(x2, mod_x, t, flat(a_x), flat(b_x), wa, wb, g1, b1)
        else:
            w = _rec_weights(rec_w_in[i])
            gate_b = jnp.stack(_split_gates(mlstm_gate_b[i].reshape(-1).astype(F32)))
            px = [a.reshape(bsz, t, -1) for a in _rec_proj(x2, mod_x, t, w, gate_b)]
            pc = [a.reshape(bsz, lc, -1) for a in _rec_proj(c2, mod_c, bsz * lc, w, gate_b)]
            mq, mk, mv, mo, rq, rk, rv, rg, gi, gf = px
            _, mkc, mvc, _, _, rkc, rvc, _, gic, gfc = pc
            assert last
            logit = jnp.zeros((8, LANES), F32).at[:2, :RET_HEADS].set(ret_decay_logit[i].astype(F32))
            m_f, m_b, r_f, r_b = _rec_scan(mq, mk, mv, gi, gf, rq, rk, rv, mkc, mvc, gic, gfc, rkc, rvc, logit)
            flat = lambda a: a.reshape(-1, a.shape[-1])
            x2 = _rec_out(x2, mod_x, t, flat(m_f), flat(m_b), flat(r_f), flat(r_b), flat(mo), flat(rg),
                          row2(mlstm_norm_g[i]), row2(ret_norm_g[i]), row2(ret_norm_b[i]),
                          rec_w_out[i].astype(BF16), g1, b1)
        x2 = ffn(x2, mod_x, 2, t, 1)
        if not last:
            c2 = ffn(c2_mix, mod_c, 2, bsz * lc, 1)
    return x2.reshape(bsz, t, d).astype(out_dtype)
```

```python
import functools
import math

import jax
import jax.numpy as jnp
from jax import lax
from jax.experimental import pallas as pl
from jax.experimental.pallas import tpu as pltpu

F32 = jnp.float32
BF16 = jnp.bfloat16

D_MODEL = 1024
DEPTH = 2
GRID_W = 64
D_FF = 2816
DIFF_HEADS = 4
DIFF_HEAD_DIM = 64
DIFF_V_DIM = 128
WIN_HEADS = 8
WIN_KV_HEADS = 2
WIN_GROUP = WIN_HEADS // WIN_KV_HEADS
WIN_HEAD_DIM = 64
WINDOW = 128
Q_BLOCK = 128
ROPE_BASE = 10000.0
ROPE_FREQS = 16
MLSTM_HEADS = 4
MLSTM_HEAD_DIM = 128
RET_HEADS = 4
RET_QK_DIM = 64
RET_V_DIM = 128
CHUNK = 128
ALPHA = (2.0 * DEPTH) ** 0.25
EPS = 1e-5
LOG2_E = 1.4426950408889634
NEG_BIG = -1e30

LANES = 128
BF16_SUBLANES = 16
V7X_VMEM_LIMIT_BYTES = 56 * 1024 * 1024

TOKEN_TILE = 512
FF_CHUNK = 256
FFN_SUB_TILES = 1
ATTN_TQ = 512
ATTN_TK = 1024
ATTN_PIECE = 256
ATTN_HEADS_PER_STEP = 2
WIN_TQ = 256
SCAN_CHUNKS = 2


def _params(semantics):
    return pltpu.CompilerParams(dimension_semantics=semantics, vmem_limit_bytes=V7X_VMEM_LIMIT_BYTES)


def _resident(shape):
    return pl.BlockSpec(shape, lambda *_: (0,) * len(shape), pipeline_mode=pl.Buffered(1))


def _layer_norm(y, g, b):
    mu = jnp.mean(y, axis=-1, keepdims=True)
    yc = y - mu
    var = jnp.mean(yc * yc, axis=-1, keepdims=True)
    out = yc * lax.rsqrt(var + EPS) * g
    return out if b is None else out + b


def _log_sigmoid(x):
    return jnp.minimum(x, 0.0) - jnp.log1p(jnp.exp(-jnp.abs(x)))


def _dot(a, b):
    return jnp.dot(a, b, preferred_element_type=F32)


def _dot_nt(a, b):
    return lax.dot_general(a, b, (((1,), (1,)), ((), ())), preferred_element_type=F32)


def _dot_tn(a, b):
    return lax.dot_general(a, b, (((0,), (0,)), ((), ())), preferred_element_type=F32)


def _mod_kernel(c_ref, w_ref, b_ref, o_ref):
    c = c_ref[...]
    h = c * jax.nn.sigmoid(c)
    o_ref[...] = jnp.dot(h, w_ref[...], preferred_element_type=F32,
                         precision=lax.Precision.HIGHEST) + b_ref[...]


def _modulation(cc, w, b, layer):
    rows, d = cc.shape
    n = w.shape[2]
    tn = 1024
    return pl.pallas_call(
        _mod_kernel,
        out_shape=jax.ShapeDtypeStruct((rows, n), F32),
        grid=(n // tn,),
        in_specs=[pl.BlockSpec((rows, d), lambda j: (0, 0)),
                  pl.BlockSpec((None, d, tn), lambda j: (layer, 0, j)),
                  pl.BlockSpec((None, 1, tn), lambda j: (layer, 0, j))],
        out_specs=pl.BlockSpec((rows, tn), lambda j: (0, j)),
        compiler_params=_params(("parallel",)),
        name="modulation",
    )(cc, w, b)


def _token_tile(n_tokens, tokens_per_row):
    tm = min(TOKEN_TILE, tokens_per_row)
    assert tokens_per_row % tm == 0 and n_tokens % tm == 0
    return tm


def _mod_spec(sub, tiles_per_row):
    return pl.BlockSpec((None, None, 3, D_MODEL), lambda i: (i // tiles_per_row, sub, 0, 0))


def _ffn_kernel(x_ref, mod_ref, w_in_ref, w_out_ref, g_ref, b_ref, o_ref, acc_ref, *, n_sub):
    shift, scale, gate = mod_ref[0:1, :], mod_ref[1:2, :], mod_ref[2:3, :]
    rows = x_ref.shape[0] // n_sub

    def finish(s):
        x = x_ref[s * rows:(s + 1) * rows, :]
        y = ALPHA * x + 0.5 * gate * acc_ref[s]
        o_ref[s * rows:(s + 1) * rows, :] = _layer_norm(y, g_ref[...], b_ref[...])

    for s in range(n_sub):
        h = (x_ref[s * rows:(s + 1) * rows, :] * (1.0 + scale) + shift).astype(BF16)
        for c in range(D_FF // FF_CHUNK):
            lo = c * FF_CHUNK
            gt = _dot(h, w_in_ref[:, lo:lo + FF_CHUNK])
            up = _dot(h, w_in_ref[:, D_FF + lo:D_FF + lo + FF_CHUNK])
            act = (gt * jax.nn.sigmoid(gt) * up).astype(BF16)
            part = _dot(act, w_out_ref[lo:lo + FF_CHUNK, :])
            if c == 0:
                acc_ref[s] = part
            else:
                acc_ref[s] += part
            if c == 0 and s > 0:
                finish(s - 1)
    finish(n_sub - 1)


def _ffn(x2, mod4, sub, tokens_per_row, w_in, w_out, layer, which, g, b):
    n = x2.shape[0]
    tm = _token_tile(n, tokens_per_row)
    n_sub = FFN_SUB_TILES if (tokens_per_row % (FFN_SUB_TILES * tm) == 0) else 1
    tb = tm * n_sub
    tile = pl.BlockSpec((tb, D_MODEL), lambda i: (i, 0))
    weight = lambda shape: pl.BlockSpec((None, None) + shape, lambda i: (layer, which, 0, 0),
                                        pipeline_mode=pl.Buffered(1))
    return pl.pallas_call(
        functools.partial(_ffn_kernel, n_sub=n_sub),
        out_shape=jax.ShapeDtypeStruct((n, D_MODEL), F32),
        grid=(n // tb,),
        in_specs=[tile, _mod_spec(sub, tokens_per_row // tb),
                  weight((D_MODEL, 2 * D_FF)), weight((D_FF, D_MODEL)),
                  _resident((1, D_MODEL)), _resident((1, D_MODEL))],
        out_specs=tile,
        scratch_shapes=[pltpu.VMEM((n_sub, tm, D_MODEL), F32)],
        compiler_params=_params(("parallel",)),
        name="ffn_sublayer",
    )(x2, mod4, w_in, w_out, g, b)


def _rope(p, cos, sin):
    lane = lax.broadcasted_iota(jnp.int32, (p.shape[0], LANES), 1)
    first_half = (lane & ROPE_FREQS) == 0
    outs = []
    for j in range(p.shape[1] // LANES):
        xs = p[:, j * LANES:(j + 1) * LANES]
        partner = jnp.where(first_half, pltpu.roll(xs, LANES - ROPE_FREQS, 1), pltpu.roll(xs, ROPE_FREQS, 1))
        outs.append(xs * cos + partner * sin)
    return jnp.concatenate(outs, axis=1)


_ATTN_COLS = (("aq", 512), ("ak", 512), ("av", 512), ("bq", 512), ("bk", 256), ("bv", 128))
_ATTN_GROUPS = (("aq",), ("ak",), ("av",), ("bq",), ("bk", "bv"))


def _attn_proj_kernel(*refs, rope):
    if rope:
        x_ref, mod_ref, w_ref, cos_ref, sin_ref = refs[:5]
        outs = refs[5:]
        cos, sin = cos_ref[...], sin_ref[...]
    else:
        x_ref, mod_ref, w_ref = refs[:3]
        outs = refs[3:]
    x = x_ref[...]
    shift, scale = mod_ref[0:1, :], mod_ref[1:2, :]
    h = (x * (1.0 + scale) + shift).astype(BF16)
    widths = dict(_ATTN_COLS)
    out_refs = dict(zip(widths, outs))
    lo = 0
    for group in _ATTN_GROUPS:
        total = sum(widths[name] for name in group)
        pg = _dot(h, w_ref[:, lo:lo + total])
        lo += total
        off = 0
        for name in group:
            p = pg[:, off:off + widths[name]]
            off += widths[name]
            if rope and name in ("aq", "ak", "bq", "bk"):
                p = _rope(p, cos, sin)
            if name in ("aq", "bq"):
                p = p * (DIFF_HEAD_DIM ** -0.5 * LOG2_E)
            out_refs[name][...] = p.astype(BF16)


def _attn_proj(x2, mod4, tokens_per_row, w, tables):
    n = x2.shape[0]
    tm = _token_tile(n, tokens_per_row)
    rope = tables is not None
    tile = lambda width: pl.BlockSpec((tm, width), lambda i: (i, 0))
    in_specs = [tile(D_MODEL), _mod_spec(1, tokens_per_row // tm), _resident(w.shape)]
    args = [x2, mod4, w]
    if rope:
        tpr = tokens_per_row // tm
        tab = pl.BlockSpec((tm, LANES), lambda i: (i % tpr, 0))
        in_specs += [tab, tab]
        args += list(tables)
    return pl.pallas_call(
        functools.partial(_attn_proj_kernel, rope=rope),
        out_shape=[jax.ShapeDtypeStruct((n, width), BF16) for _, width in _ATTN_COLS],
        grid=(n // tm,),
        in_specs=in_specs,
        out_specs=[tile(width) for _, width in _ATTN_COLS],
        compiler_params=_params(("parallel",)),
        name="attn_in_proj",
    )(*args)


def _attn_out_kernel(x_ref, mod_ref, a_ref, b_ref, wa_ref, wb_ref, g_ref, bb_ref, o_ref):
    y = _dot(a_ref[...], wa_ref[...]) + _dot(b_ref[...], wb_ref[...])
    z = ALPHA * x_ref[...] + mod_ref[2:3, :] * y
    o_ref[...] = _layer_norm(z, g_ref[...], bb_ref[...])


def _attn_out(x2, mod4, tokens_per_row, a, b, wa, wb, g, bb):
    n = x2.shape[0]
    tm = _token_tile(n, tokens_per_row)
    tile = lambda width: pl.BlockSpec((tm, width), lambda i: (i, 0))
    return pl.pallas_call(
        _attn_out_kernel,
        out_shape=jax.ShapeDtypeStruct((n, D_MODEL), F32),
        grid=(n // tm,),
        in_specs=[tile(D_MODEL), _mod_spec(1, tokens_per_row // tm), tile(a.shape[1]), tile(b.shape[1]),
                  _resident(wa.shape), _resident(wb.shape), _resident((1, D_MODEL)), _resident((1, D_MODEL))],
        out_specs=tile(D_MODEL),
        compiler_params=_params(("parallel",)),
        name="attn_out_proj",
    )(x2, mod4, a, b, wa, wb, g, bb)


def _diff_attn_kernel(*refs, n_lat, lc, lambda_init):
    if n_lat:
        q_ref, k_ref, vt_ref, kc_ref, vct_ref, lam_ref, g_ref, o_ref = refs
    else:
        q_ref, kc_ref, vct_ref, lam_ref, g_ref, o_ref = refs
    t = lam_ref[...]
    lam = (jnp.exp(jnp.sum(t[0:1, :] * t[1:2, :], axis=1, keepdims=True))
           - jnp.exp(jnp.sum(t[2:3, :] * t[3:4, :], axis=1, keepdims=True)) + lambda_init)
    n_heads = q_ref.shape[1] // LANES
    stages = []
    for hd in range(n_heads):
        chunks = [(k_ref, vt_ref, c * ATTN_TK, ATTN_TK) for c in range(n_lat)] + [(kc_ref, vct_ref, 0, lc)]
        stages += [(hd, idx, idx == len(chunks) - 1, chunk) for idx, chunk in enumerate(chunks)]

    def n_pieces(stage):
        return stage[3][3] // ATTN_PIECE

    def scores(stage, piece, sub):
        hd, _, _, (kr, _, lo, _) = stage
        q = q_ref[:, hd * LANES:(hd + 1) * LANES]
        kblk = kr[lo + piece * ATTN_PIECE:lo + (piece + 1) * ATTN_PIECE, hd * LANES:(hd + 1) * LANES]
        lane = lax.broadcasted_iota(jnp.int32, kblk.shape, 1)
        keep = (lane < DIFF_HEAD_DIM) if sub == 0 else (lane >= DIFF_HEAD_DIM)
        return _dot_nt(jnp.where(keep, kblk, jnp.zeros_like(kblk)), q)

    def finish(hd, acc):
        num = [a[:DIFF_V_DIM, :] * (1.0 / a[DIFF_V_DIM:DIFF_V_DIM + 1, :]) for a in acc]
        o = (num[0] - lam * num[1]).T
        o = o * lax.rsqrt(jnp.mean(o * o, axis=-1, keepdims=True) + EPS) * g_ref[...]
        o_ref[:, hd * LANES:(hd + 1) * LANES] = (o * (1.0 - lambda_init)).astype(BF16)

    m, acc = [None, None], [None, None]
    s_next = [[scores(stages[0], j, sub) for j in range(n_pieces(stages[0]))] for sub in range(2)]
    for si, stage in enumerate(stages):
        hd, idx, is_last, (_, vr, lo, size) = stage
        vt = jnp.concatenate([vr[hd * LANES:(hd + 1) * LANES, lo:lo + size], jnp.ones((BF16_SUBLANES, size), BF16)],
                             axis=0)
        nxt = stages[si + 1] if si + 1 < len(stages) else None
        for sub in range(2):
            s_cur, s_new = s_next[sub], []
            smax = functools.reduce(jnp.maximum, [jnp.max(s, axis=0, keepdims=True) for s in s_cur])
            m_new = smax if idx == 0 else jnp.maximum(m[sub], smax)
            ps = []
            for j, s in enumerate(s_cur):
                if nxt is not None and j < n_pieces(nxt):
                    s_new.append(scores(nxt, j, sub))
                ps.append(jnp.exp2(s - m_new).astype(BF16))
            if nxt is not None:
                s_new += [scores(nxt, j, sub) for j in range(len(s_new), n_pieces(nxt))]
            part = _dot(vt, jnp.concatenate(ps, axis=0))
            acc[sub] = part if idx == 0 else jnp.exp2(m[sub] - m_new) * acc[sub] + part
            m[sub], s_next[sub] = m_new, s_new
        if is_last:
            finish(hd, acc)


def _diff_attn(q, k, vt, kc, vct, lam_vec, g, lambda_init):
    bsz, tq_total, _ = q.shape
    lc = kc.shape[1]
    tq = min(ATTN_TQ, tq_total)
    n_lat = 0 if k is None else k.shape[1] // ATTN_TK
    assert tq_total % tq == 0 and (k is None or k.shape[1] % ATTN_TK == 0)
    width = ATTN_HEADS_PER_STEP * LANES
    qspec = pl.BlockSpec((None, tq, width), lambda b, h, i: (b, i, h))
    in_specs, args = [qspec], [q]
    if n_lat:
        t = k.shape[1]
        in_specs += [pl.BlockSpec((None, t, width), lambda b, h, i: (b, 0, h)),
                     pl.BlockSpec((None, width, t), lambda b, h, i: (b, h, 0))]
        args += [k, vt]
    in_specs += [pl.BlockSpec((None, lc, width), lambda b, h, i: (b, 0, h)),
                 pl.BlockSpec((None, width, lc), lambda b, h, i: (b, h, 0)),
                 pl.BlockSpec((4, DIFF_HEAD_DIM), lambda b, h, i: (0, 0)),
                 pl.BlockSpec((1, DIFF_V_DIM), lambda b, h, i: (0, 0))]
    args += [kc, vct, lam_vec, g]
    return pl.pallas_call(
        functools.partial(_diff_attn_kernel, n_lat=n_lat, lc=lc, lambda_init=lambda_init),
        out_shape=jax.ShapeDtypeStruct((bsz, tq_total, DIFF_HEADS * DIFF_V_DIM), BF16),
        grid=(bsz, DIFF_HEADS // ATTN_HEADS_PER_STEP, tq_total // tq),
        in_specs=in_specs,
        out_specs=qspec,
        compiler_params=_params(("parallel", "parallel", "parallel")),
        name="diff_attention",
    )(*args)


def _win_attn_kernel(*refs, has_window, lc, tq):
    hd = WIN_HEAD_DIM
    if has_window:
        n_blk = tq // Q_BLOCK + 2
        q_ref = refs[0]
        k_refs, v_refs = refs[1:1 + n_blk], refs[1 + n_blk:1 + 2 * n_blk]
        kc_ref, vc_ref, sink_ref, o_ref = refs[1 + 2 * n_blk:]
        kw = jnp.concatenate([ref[...] for ref in k_refs] + [kc_ref[...]], axis=0)
        vt = jnp.concatenate([ref[...] for ref in v_refs] + [vc_ref[...]], axis=1)
        span = tq + 2 * WINDOW
        nk = span + lc
        i = pl.program_id(1)
        last = pl.num_programs(1) - 1
        c = lax.broadcasted_iota(jnp.int32, (nk, tq), 0)
        r = lax.broadcasted_iota(jnp.int32, (nk, tq), 1)
        lo_valid = jnp.where(i == 0, WINDOW, 0)
        hi_valid = jnp.where(i == last, WINDOW + tq, span)
        in_win = (c >= jnp.maximum(r, lo_valid)) & (c <= r + 2 * WINDOW) & (c < hi_valid)
        bias = jnp.where(in_win | (c >= span), 0.0, NEG_BIG)
        bias = jnp.concatenate([bias, bias], axis=1)
    else:
        q_ref, kc_ref, vc_ref, sink_ref, o_ref = refs
        kw, vt = kc_ref[...], vc_ref[...]
        nk = lc
        bias = None
    lane = lax.broadcasted_iota(jnp.int32, (nk, LANES), 1)
    zero = jnp.zeros((nk, LANES), BF16)
    ones = jnp.ones((BF16_SUBLANES, nk), BF16)
    units = [(kh, p) for kh in range(WIN_KV_HEADS) for p in range(2)]
    heads = {(kh, p): [kh * WIN_GROUP + 2 * j + p for j in range(2)] for kh, p in units}

    def scores(kh, p):
        ksrc = kw[:, (0 if p == kh else 1) * LANES:(1 if p == kh else 2) * LANES]
        kmat = jnp.where((lane < hd) if p == 0 else (lane >= hd), ksrc, zero)
        qcat = jnp.concatenate([q_ref[:, (h // 2) * LANES:(h // 2 + 1) * LANES] for h in heads[kh, p]], axis=0)
        return _dot_nt(kmat, qcat)

    out_t = {}
    s_next = scores(*units[0])
    for idx, (kh, p) in enumerate(units):
        s = s_next if bias is None else s_next + bias
        if idx + 1 < len(units):
            s_next = scores(*units[idx + 1])
        sink = jnp.concatenate([jnp.full((1, tq), sink_ref[h] * LOG2_E, F32) for h in heads[kh, p]], axis=1)
        m = jnp.maximum(jnp.max(s, axis=0, keepdims=True), sink)
        vt_ext = jnp.concatenate([vt[kh * hd:(kh + 1) * hd, :], ones], axis=0)
        acc = _dot(vt_ext, jnp.exp2(s - m).astype(BF16))
        denom = acc[hd:hd + 1, :] + jnp.exp2(sink - m)
        out_t[kh, p] = acc[:hd, :] * (1.0 / denom)
    for kh in range(WIN_KV_HEADS):
        for j in range(2):
            pair = kh * 2 + j
            pair_t = jnp.concatenate([out_t[kh, p][:, j * tq:(j + 1) * tq] for p in range(2)], axis=0)
            o_ref[:, pair * LANES:(pair + 1) * LANES] = pair_t.T.astype(BF16)


def _win_attn(q, k, vt, kc, vct, sink, has_window):
    bsz, tq_total, width = q.shape
    lc = kc.shape[1]
    tq = min(WIN_TQ, tq_total)
    assert tq_total % tq == 0 and tq % Q_BLOCK == 0
    nq = tq_total // tq
    per = tq // Q_BLOCK
    n_kblk = tq_total // Q_BLOCK
    qspec = pl.BlockSpec((None, tq, width), lambda b, i: (b, i, 0))
    in_specs, args = [qspec], [q]
    if has_window:
        def kblock(j):
            return lambda b, i: (b, jnp.clip(i * per + j - 1, 0, n_kblk - 1))
        blk = [kblock(j) for j in range(per + 2)]
        in_specs += [pl.BlockSpec((None, Q_BLOCK, 2 * LANES), lambda b, i, f=f: (*f(b, i), 0)) for f in blk]
        in_specs += [pl.BlockSpec((None, LANES, Q_BLOCK), lambda b, i, f=f: (f(b, i)[0], 0, f(b, i)[1])) for f in blk]
        args += [k] * len(blk) + [vt] * len(blk)
    in_specs += [pl.BlockSpec((None, lc, 2 * LANES), lambda b, i: (b, 0, 0)),
                 pl.BlockSpec((None, LANES, lc), lambda b, i: (b, 0, 0)),
                 pl.BlockSpec(memory_space=pltpu.SMEM)]
    args += [kc, vct, sink]
    return pl.pallas_call(
        functools.partial(_win_attn_kernel, has_window=has_window, lc=lc, tq=tq),
        out_shape=jax.ShapeDtypeStruct((bsz, tq_total, width), BF16),
        grid=(bsz, nq),
        in_specs=in_specs,
        out_specs=qspec,
        compiler_params=_params(("parallel", "parallel")),
        name="window_attention",
    )(*args)


_REC_COLS = (("mq", 512, BF16), ("mk", 512, BF16), ("mv", 512, BF16), ("mo", 512, BF16),
             ("rq", 512, BF16), ("rk", 512, BF16), ("rv", 512, BF16), ("rg", 512, BF16),
             ("gi", 128, F32), ("gf", 128, F32))
_N_DIR_HEADS = 2 * MLSTM_HEADS


def _rec_proj_kernel(x_ref, mod_ref, w_ref, gate_b_ref, *outs):
    x = x_ref[...]
    shift, scale = mod_ref[0:1, :], mod_ref[1:2, :]
    h = (x * (1.0 + scale) + shift).astype(BF16)
    lo = 0
    for (name, width, dtype), o_ref in zip(_REC_COLS, outs):
        p = _dot(h, w_ref[:, lo:lo + width])
        lo += width
        if name == "mk":
            p = p * (MLSTM_HEAD_DIM ** -0.5)
        elif name == "rk":
            p = p * (RET_QK_DIM ** -0.5)
        elif name == "gi":
            p = p + gate_b_ref[0:1, :]
        elif name == "gf":
            p = _log_sigmoid(p + gate_b_ref[1:2, :])
        o_ref[...] = p.astype(dtype)


def _rec_proj(x2, mod4, tokens_per_row, w, gate_b):
    n = x2.shape[0]
    tm = _token_tile(n, tokens_per_row)
    tile = lambda width: pl.BlockSpec((tm, width), lambda i: (i, 0))
    return pl.pallas_call(
        _rec_proj_kernel,
        out_shape=[jax.ShapeDtypeStruct((n, width), dtype) for _, width, dtype in _REC_COLS],
        grid=(n // tm,),
        in_specs=[tile(D_MODEL), _mod_spec(1, tokens_per_row // tm), _resident(w.shape), _resident((2, LANES))],
        out_specs=[tile(width) for _, width, _ in _REC_COLS],
        compiler_params=_params(("parallel",)),
        name="rec_in_proj",
    )(x2, mod4, w, gate_b)


def _scan_specs(ncc, ncl, width, rows):
    ctx_f = pl.BlockSpec((None, rows, width), lambda b, s: (b, jnp.minimum(s, ncc - 1), 0))
    ctx_b = pl.BlockSpec((None, rows, width), lambda b, s: (b, jnp.maximum(ncc - 1 - s, 0), 0))
    lat_f = pl.BlockSpec((None, rows, width), lambda b, s: (b, jnp.maximum(s - ncc, 0), 0))
    lat_b = pl.BlockSpec((None, rows, width), lambda b, s: (b, jnp.minimum(ncl - 1 - (s - ncc), ncl - 1), 0))
    return ctx_f, ctx_b, lat_f, lat_b


def _tri_masks():
    r = lax.broadcasted_iota(jnp.int32, (CHUNK, CHUNK), 0)
    c = lax.broadcasted_iota(jnp.int32, (CHUNK, CHUNK), 1)
    return r, c


def _head_cols(ref, h):
    return ref[:, h * LANES:(h + 1) * LANES]


def _mlstm_step(srcs, masks, tri, c_ref, m_ref, with_output):
    units = [(d, h) for d in range(2) for h in range(MLSTM_HEADS)]
    dv = MLSTM_HEAD_DIM
    ones = jnp.ones((CHUNK, LANES), BF16)
    v_ext = {(d, h): jnp.concatenate([_head_cols(srcs[d][2], h), ones], axis=1) for d, h in units}
    gate = []
    for d in range(2):
        gi, gf = srcs[d][3][...], srcs[d][4][...]
        cum = jnp.dot(tri[d], gf, preferred_element_type=F32, precision=lax.Precision.HIGHEST)
        b_end = cum[CHUNK - 1:CHUNK, :] if d == 0 else cum[0:1, :]
        m_prev = m_ref[d:d + 1, :]
        w_end = b_end - cum + gi
        m_new = jnp.maximum(b_end + m_prev, jnp.max(w_end, axis=0, keepdims=True))
        decay = jnp.exp(b_end + m_prev - m_new)
        w = jnp.exp(w_end - m_new)
        m_ref[d:d + 1, :] = m_new
        log_inter = cum + m_prev if with_output else None
        key_term = (gi - cum).T if with_output else None
        gate.append((cum, decay, w, log_inter, key_term))
    if with_output:
        qk, qcn, log_d, li, rmax, m_t, sm, w_inter, pv = {}, {}, {}, {}, {}, {}, {}, {}, {}
        for d, h in units:
            q = _head_cols(srcs[d][0], h)
            qk[d, h] = _dot_nt(q, _head_cols(srcs[d][1], h))
            qcn[d, h] = _dot(q, c_ref[d * MLSTM_HEADS + h].astype(BF16))
        for d, h in units:
            lane = d * MLSTM_HEADS + h
            cum, _, _, log_inter, key_term = gate[d]
            log_d[d, h] = jnp.where(masks[d], cum[:, lane:lane + 1] + key_term[lane:lane + 1, :], NEG_BIG)
            li[d, h] = log_inter[:, lane:lane + 1]
        for u in units:
            rmax[u] = jnp.max(log_d[u], axis=1, keepdims=True)
        for u in units:
            m_t[u] = jnp.maximum(li[u], rmax[u])
            sm[u] = (qk[u] * jnp.exp(log_d[u] - m_t[u])).astype(BF16)
            w_inter[u] = jnp.exp(li[u] - m_t[u])
        for u in units:
            pv[u] = _dot(sm[u], v_ext[u])
        for d, h in units:
            u = (d, h)
            both = pv[u] + w_inter[u] * qcn[u]
            den = jnp.maximum(jnp.abs(both[:, dv:]), jnp.exp(-m_t[u]))
            srcs[d][5][:, h * LANES:(h + 1) * LANES] = (both[:, :dv] * (1.0 / den)).astype(BF16)
    for d, h in units:
        row = d * MLSTM_HEADS + h
        _, decay, w, _, _ = gate[d]
        kw = (_head_cols(srcs[d][1], h).astype(F32) * w[:, row:row + 1]).astype(BF16)
        c_ref[row] = decay[:, row:row + 1] * c_ref[row] + _dot_tn(kw, v_ext[d, h])


def _ret_tables(logit_ref, intra_ref, to_end_ref, from_start_ref):
    log_gamma = _log_sigmoid(logit_ref[...])
    r, c = _tri_masks()
    rf, cf = r.astype(F32), c.astype(F32)
    for d in range(2):
        for h in range(RET_HEADS):
            row = d * RET_HEADS + h
            lg = log_gamma[d:d + 1, h:h + 1]
            rel = (rf - cf) if d == 0 else (cf - rf)
            intra_ref[row] = jnp.where(rel >= 0.0, jnp.exp(lg * jnp.maximum(rel, 0.0)), 0.0)
            to_end_ref[row] = jnp.exp(lg * ((CHUNK - 1.0 - rf) if d == 0 else rf))
            from_start_ref[row] = jnp.exp(lg * ((rf + 1.0) if d == 0 else (CHUNK - rf)))


def _ret_step(srcs, s_ref, intra_ref, to_end_ref, from_start_ref, with_output):
    units = [(d, h) for d in range(2) for h in range(RET_HEADS)]
    qk, qs = {}, {}
    if with_output:
        for d, h in units:
            q = _head_cols(srcs[d][0], h)
            qk[d, h] = _dot_nt(q, _head_cols(srcs[d][1], h))
            qs[d, h] = _dot(q, s_ref[d * RET_HEADS + h].astype(BF16))
        for d, h in units:
            row = d * RET_HEADS + h
            sc = (qk[d, h] * intra_ref[row]).astype(BF16)
            out = _dot(sc, _head_cols(srcs[d][2], h)) + from_start_ref[row] * qs[d, h]
            srcs[d][3][:, h * LANES:(h + 1) * LANES] = out.astype(BF16)
    for d, h in units:
        row = d * RET_HEADS + h
        kd = (_head_cols(srcs[d][1], h).astype(F32) * to_end_ref[row]).astype(BF16)
        chunk_decay = from_start_ref[row, CHUNK - 1:CHUNK, :] if d == 0 else from_start_ref[row, 0:1, :]
        s_ref[row] = chunk_decay * s_ref[row] + _dot_tn(kd, _head_cols(srcs[d][2], h))


def _rec_scan_kernel(mkc_f, mvc_f, gic_f, gfc_f, mkc_b, mvc_b, gic_b, gfc_b,
                     mq_f, mk_f, mv_f, gi_f, gf_f, mq_b, mk_b, mv_b, gi_b, gf_b,
                     rkc_f, rvc_f, rkc_b, rvc_b, rq_f, rk_f, rv_f, rq_b, rk_b, rv_b, logit_ref,
                     mo_f, mo_b, ro_f, ro_b,
                     c_ref, m_ref, s_ref, intra_ref, to_end_ref, from_start_ref, *, ncc):
    s = pl.program_id(1)

    @pl.when(s == 0)
    def _():
        c_ref[...] = jnp.zeros_like(c_ref)
        m_ref[...] = jnp.zeros_like(m_ref)
        s_ref[...] = jnp.zeros_like(s_ref)
        _ret_tables(logit_ref, intra_ref, to_end_ref, from_start_ref)

    r, c = _tri_masks()
    masks = (c <= r, c >= r)
    tri = (jnp.where(masks[0], 1.0, 0.0), jnp.where(masks[1], 1.0, 0.0))

    def run(m_f, m_b, r_f, r_b, with_output):
        for sub in range(SCAN_CHUNKS):
            lo = (sub * CHUNK, (SCAN_CHUNKS - 1 - sub) * CHUNK)
            rows = lambda refs, d: tuple(None if ref is None else _Rows(ref, lo[d]) for ref in refs)
            _mlstm_step((rows(m_f, 0), rows(m_b, 1)), masks, tri, c_ref, m_ref, with_output)
            _ret_step((rows(r_f, 0), rows(r_b, 1)), s_ref, intra_ref, to_end_ref, from_start_ref, with_output)

    @pl.when(s < ncc)
    def _():
        run((None, mkc_f, mvc_f, gic_f, gfc_f, None), (None, mkc_b, mvc_b, gic_b, gfc_b, None),
            (None, rkc_f, rvc_f, None), (None, rkc_b, rvc_b, None), False)

    @pl.when(s >= ncc)
    def _():
        run((mq_f, mk_f, mv_f, gi_f, gf_f, mo_f), (mq_b, mk_b, mv_b, gi_b, gf_b, mo_b),
            (rq_f, rk_f, rv_f, ro_f), (rq_b, rk_b, rv_b, ro_b), True)


class _Rows:
    def __init__(self, ref, lo):
        self.ref, self.rows = ref, slice(lo, lo + CHUNK)

    def __getitem__(self, idx):
        return self.ref[self.rows, :] if idx is Ellipsis else self.ref[self.rows, idx[1]]

    def __setitem__(self, idx, value):
        self.ref[self.rows, idx[1]] = value


def _rec_scan(mq, mk, mv, gi, gf, rq, rk, rv, mkc, mvc, gic, gfc, rkc, rvc, logit):
    bsz, t, width = mq.shape
    rows = CHUNK * SCAN_CHUNKS
    assert t % rows == 0 and mkc.shape[1] % rows == 0
    ncl, ncc = t // rows, mkc.shape[1] // rows
    cf, cb, lf, lb = _scan_specs(ncc, ncl, width, rows)
    gcf, gcb, glf, glb = _scan_specs(ncc, ncl, LANES, rows)
    return pl.pallas_call(
        functools.partial(_rec_scan_kernel, ncc=ncc),
        out_shape=[jax.ShapeDtypeStruct((bsz, t, width), BF16)] * 4,
        grid=(bsz, ncc + ncl),
        in_specs=[cf, cf, gcf, gcf, cb, cb, gcb, gcb, lf, lf, lf, glf, glf, lb, lb, lb, glb, glb,
                  cf, cf, cb, cb, lf, lf, lf, lb, lb, lb, pl.BlockSpec((8, LANES), lambda b, s: (0, 0))],
        out_specs=[lf, lb, lf, lb],
        scratch_shapes=[pltpu.VMEM((_N_DIR_HEADS, MLSTM_HEAD_DIM, 2 * MLSTM_HEAD_DIM), F32),
                        pltpu.VMEM((8, LANES), F32),
                        pltpu.VMEM((2 * RET_HEADS, LANES, RET_V_DIM), F32)]
        + [pltpu.VMEM((2 * RET_HEADS, CHUNK, LANES), F32)] * 3,
        compiler_params=_params(("parallel", "arbitrary")),
        name="rec_scan",
    )(mkc, mvc, gic, gfc, mkc, mvc, gic, gfc, mq, mk, mv, gi, gf, mq, mk, mv, gi, gf,
      rkc, rvc, rkc, rvc, rq, rk, rv, rq, rk, rv, logit)


def _rec_out_kernel(x_ref, mod_ref, mf_ref, mb_ref, rf_ref, rb_ref, o_ref_in, rg_ref, mg_ref, rng_ref, rnb_ref,
                    w_ref, g_ref, b_ref, out_ref):
    hm = mf_ref[...].astype(F32) + mb_ref[...].astype(F32)
    hr = rf_ref[...].astype(F32) + rb_ref[...].astype(F32)
    o_gate = jax.nn.sigmoid(o_ref_in[...].astype(F32))
    rg = rg_ref[...].astype(F32)
    r_gate = rg * jax.nn.sigmoid(rg)
    parts_m, parts_r = [], []
    for h in range(MLSTM_HEADS):
        cols = slice(h * LANES, (h + 1) * LANES)
        parts_m.append(_layer_norm(hm[:, cols], mg_ref[:, cols], None) * o_gate[:, cols])
        parts_r.append(_layer_norm(hr[:, cols], rng_ref[:, cols], rnb_ref[:, cols]) * r_gate[:, cols])
    width = MLSTM_HEADS * MLSTM_HEAD_DIM
    y = (_dot(jnp.concatenate(parts_m, axis=1).astype(BF16), w_ref[0:width, :])
         + _dot(jnp.concatenate(parts_r, axis=1).astype(BF16), w_ref[width:, :]))
    z = ALPHA * x_ref[...] + mod_ref[2:3, :] * y
    out_ref[...] = _layer_norm(z, g_ref[...], b_ref[...])


def _rec_out(x2, mod4, tokens_per_row, mf, mb, rf, rb, o, rg, mg, rng, rnb, w, g, b):
    n = x2.shape[0]
    tm = _token_tile(n, tokens_per_row)
    tile = lambda width: pl.BlockSpec((tm, width), lambda i: (i, 0))
    half = tile(512)
    return pl.pallas_call(
        _rec_out_kernel,
        out_shape=jax.ShapeDtypeStruct((n, D_MODEL), F32),
        grid=(n // tm,),
        in_specs=[tile(D_MODEL), _mod_spec(1, tokens_per_row // tm), half, half, half, half, half, half,
                  _resident((1, 512)), _resident((1, 512)), _resident((1, 512)),
                  _resident(w.shape), _resident((1, D_MODEL)), _resident((1, D_MODEL))],
        out_specs=tile(D_MODEL),
        compiler_params=_params(("parallel",)),
        name="rec_out_proj",
    )(x2, mod4, mf, mb, rf, rb, o, rg, mg, rng, rnb, w, g, b)


def _pad_heads(w, n_heads, axis):
    shape = list(w.shape)
    shape[axis:axis + 1] = [n_heads, shape[axis] // n_heads]
    w = w.reshape(shape)
    pad = [(0, 0)] * w.ndim
    pad[axis + 1] = (0, LANES - shape[axis + 1])
    w = jnp.pad(w, pad)
    shape[axis:axis + 2] = [n_heads * LANES]
    return w.reshape(shape)


def _attn_weights(w_in, w_out):
    aq, ak, av, bq, bk, bv = jnp.split(w_in, [512, 1024, 1536, 2048, 2176], axis=1)
    bk_swapped = jnp.concatenate([bk[:, WIN_HEAD_DIM:], bk[:, :WIN_HEAD_DIM]], axis=1)
    w = jnp.concatenate([aq, ak, av, bq, bk, bk_swapped, bv], axis=1)
    wa, wb = w_out[:DIFF_HEADS * DIFF_V_DIM], w_out[DIFF_HEADS * DIFF_V_DIM:]
    return w.astype(BF16), wa.astype(BF16), wb.astype(BF16)


def _rec_weights(w_in):
    mq, mk, mv, mo, mg, rq, rk, rv, rg = jnp.split(w_in, [512, 1024, 1536, 2048, 2064, 2320, 2576, 3088], axis=1)
    gi, gf = _split_gates(mg)
    w = jnp.concatenate([mq, mk, mv, mo, _pad_heads(rq, RET_HEADS, 1), _pad_heads(rk, RET_HEADS, 1), rv, rg, gi, gf],
                        axis=1)
    return w.astype(BF16)


def _split_gates(g):
    g = g.reshape(*g.shape[:-1], 2, 2, MLSTM_HEADS)
    pad = [(0, 0)] * (g.ndim - 3) + [(0, LANES - _N_DIR_HEADS)]
    return tuple(jnp.pad(g[..., io, :].reshape(*g.shape[:-3], _N_DIR_HEADS), pad) for io in range(2))


def _rope_tables(t):
    rows = t // GRID_W
    row = jnp.repeat(jnp.arange(rows), GRID_W)
    col = jnp.tile(jnp.arange(GRID_W), rows)
    inv = ROPE_BASE ** (-jnp.arange(ROPE_FREQS, dtype=F32) / ROPE_FREQS)
    ang_r, ang_c = row[:, None] * inv, col[:, None] * inv
    cos = jnp.concatenate([jnp.cos(ang_r)] * 2 + [jnp.cos(ang_c)] * 2, axis=1)
    sin = jnp.concatenate([-jnp.sin(ang_r), jnp.sin(ang_r), -jnp.sin(ang_c), jnp.sin(ang_c)], axis=1)
    return jnp.tile(cos, (1, 2)), jnp.tile(sin, (1, 2))


def kernel(x, c, ctx, c_ctx, ada_w, ada_b, ln_g, ln_b, ffn_w_in, ffn_w_out, attn_w_in, attn_w_out,
           diff_lambda, diff_norm_g, sink_logits, rec_w_in, rec_w_out, mlstm_gate_b, mlstm_norm_g,
           ret_decay_logit, ret_norm_g, ret_norm_b):
    bsz, t, d = x.shape
    lc = ctx.shape[1]
    assert d == D_MODEL and t % ATTN_TK == 0 and lc % CHUNK == 0
    out_dtype = x.dtype
    x2 = x.reshape(bsz * t, d).astype(F32)
    c2 = ctx.reshape(bsz * lc, d).astype(F32)
    mod_rows = 16
    cc = jnp.zeros((mod_rows, d), F32).at[:bsz].set(c).at[bsz].set(c_ctx)
    row2 = lambda v: v.reshape(1, -1).astype(F32)
    ada_b3 = ada_b.reshape(DEPTH, 1, -1)
    ffn_w_in16, ffn_w_out16 = ffn_w_in.astype(BF16), ffn_w_out.astype(BF16)

    for l in range(DEPTH):
        last = l == DEPTH - 1
        i = l // 2
        mod = _modulation(cc, ada_w, ada_b3, l).reshape(mod_rows, 3, 3, d)
        mod_x, mod_c = mod[:bsz], mod[bsz:bsz + 1]
        ffn = lambda z, m, sub, tpr, j: _ffn(z, m, sub, tpr, ffn_w_in16, ffn_w_out16, l, j,
                                             row2(ln_g[l, sub]), row2(ln_b[l, sub]))
        x2 = ffn(x2, mod_x, 0, t, 0)
        c2 = ffn(c2, mod_c, 0, bsz * lc, 0)
        g1, b1 = row2(ln_g[l, 1]), row2(ln_b[l, 1])
        if l % 2 == 0:
            lambda_init = 0.8 - 0.6 * math.exp(-0.3 * l)
            w, wa, wb = _attn_weights(attn_w_in[i], attn_w_out[i])
            aq, ak, av, bq, bk, bv = [a.reshape(bsz, t, -1) for a in _attn_proj(x2, mod_x, t, w, _rope_tables(t))]
            aqc, akc, avc, bqc, bkc, bvc = [a.reshape(bsz, lc, -1) for a in _attn_proj(c2, mod_c, bsz * lc, w, None)]
            avt, avct = jnp.swapaxes(av, 1, 2), jnp.swapaxes(avc, 1, 2)
            bvt, bvct = jnp.swapaxes(bv, 1, 2), jnp.swapaxes(bvc, 1, 2)
            lam_vec, sub_g = diff_lambda[i].astype(F32), row2(diff_norm_g[i])
            sink = sink_logits[i].astype(F32)
            a_x = _diff_attn(aq, ak, avt, akc, avct, lam_vec, sub_g, lambda_init)
            b_x = _win_attn(bq, bk, bvt, bkc, bvct, sink, True)
            flat = lambda a: a.reshape(-1, a.shape[-1])
            if not last:
                a_c = _diff_attn(aqc, None, None, akc, avct, lam_vec, sub_g, lambda_init)
                b_c = _win_attn(bqc, None, None, bkc, bvct, sink, False)
                c2_mix = _attn_out(c2, mod_c, bsz * lc, flat(a_c), flat(b_c), wa, wb, g1, b1)
            x2 = _attn_out(x2, mod_x, t, flat(a_x), flat(b_x), wa, wb, g1, b1)
        else:
            w = _rec_weights(rec_w_in[i])
            gate_b = jnp.stack(_split_gates(mlstm_gate_b[i].reshape(-1).astype(F32)))
            px = [a.reshape(bsz, t, -1) for a in _rec_proj(x2, mod_x, t, w, gate_b)]
            pc = [a.reshape(bsz, lc, -1) for a in _rec_proj(c2, mod_c, bsz * lc, w, gate_b)]
            mq, mk, mv, mo, rq, rk, rv, rg, gi, gf = px
            _, mkc, mvc, _, _, rkc, rvc, _, gic, gfc = pc
            assert last
            logit = jnp.zeros((8, LANES), F32).at[:2, :RET_HEADS].set(ret_decay_logit[i].astype(F32))
            m_f, m_b, r_f, r_b = _rec_scan(mq, mk, mv, gi, gf, rq, rk, rv, mkc, mvc, gic, gfc, rkc, rvc, logit)
            flat = lambda a: a.reshape(-1, a.shape[-1])
            x2 = _rec_out(x2, mod_x, t, flat(m_f), flat(m_b), flat(r_f), flat(r_b), flat(mo), flat(rg),
                          row2(mlstm_norm_g[i]), row2(ret_norm_g[i]), row2(ret_norm_b[i]),
                          rec_w_out[i].astype(BF16), g1, b1)
        x2 = ffn(x2, mod_x, 2, t, 1)
        if not last:
            c2 = ffn(c2_mix, mod_c, 2, bsz * lc, 1)
    return x2.reshape(bsz, t, d).astype(out_dtype)
```

```python
import functools
import math

import jax
import jax.numpy as jnp
from jax import lax
from jax.experimental import pallas as pl
from jax.experimental.pallas import tpu as pltpu

F32 = jnp.float32
BF16 = jnp.bfloat16

D_MODEL = 1024
DEPTH = 2
GRID_W = 64
D_FF = 2816
DIFF_HEADS = 4
DIFF_HEAD_DIM = 64
DIFF_V_DIM = 128
WIN_HEADS = 8
WIN_KV_HEADS = 2
WIN_GROUP = WIN_HEADS // WIN_KV_HEADS
WIN_HEAD_DIM = 64
WINDOW = 128
Q_BLOCK = 128
ROPE_BASE = 10000.0
ROPE_FREQS = 16
MLSTM_HEADS = 4
MLSTM_HEAD_DIM = 128
RET_HEADS = 4
RET_QK_DIM = 64
RET_V_DIM = 128
CHUNK = 128
ALPHA = (2.0 * DEPTH) ** 0.25
EPS = 1e-5
LOG2_E = 1.4426950408889634
NEG_BIG = -1e30

LANES = 128
BF16_SUBLANES = 16
V7X_VMEM_LIMIT_BYTES = 56 * 1024 * 1024

TOKEN_TILE = 1024
FFN_TILE = 512
FF_CHUNK = 256
FFN_SUB_TILES = 2
ATTN_TQ = 512
ATTN_TK = 1024
ATTN_PIECE = 256
ATTN_HEADS_PER_STEP = 2
WIN_TQ = 256
SCAN_CHUNKS = 2


def _params(semantics):
    return pltpu.CompilerParams(dimension_semantics=semantics, vmem_limit_bytes=V7X_VMEM_LIMIT_BYTES)


def _resident(shape):
    return pl.BlockSpec(shape, lambda *_: (0,) * len(shape), pipeline_mode=pl.Buffered(1))


def _layer_norm(y, g, b):
    mu = jnp.mean(y, axis=-1, keepdims=True)
    yc = y - mu
    var = jnp.mean(yc * yc, axis=-1, keepdims=True)
    out = yc * lax.rsqrt(var + EPS) * g
    return out if b is None else out + b


def _log_sigmoid(x):
    return jnp.minimum(x, 0.0) - jnp.log1p(jnp.exp(-jnp.abs(x)))


def _dot(a, b):
    return jnp.dot(a, b, preferred_element_type=F32)


def _dot_nt(a, b):
    return lax.dot_general(a, b, (((1,), (1,)), ((), ())), preferred_element_type=F32)


def _dot_tn(a, b):
    return lax.dot_general(a, b, (((0,), (0,)), ((), ())), preferred_element_type=F32)


def _mod_kernel(c_ref, w_ref, b_ref, o_ref):
    c = c_ref[...]
    h = c * jax.nn.sigmoid(c)
    o_ref[...] = jnp.dot(h, w_ref[...], preferred_element_type=F32,
                         precision=lax.Precision.HIGHEST) + b_ref[...]


def _modulation(cc, w, b, layer):
    rows, d = cc.shape
    n = w.shape[2]
    tn = 1024
    return pl.pallas_call(
        _mod_kernel,
        out_shape=jax.ShapeDtypeStruct((rows, n), F32),
        grid=(n // tn,),
        in_specs=[pl.BlockSpec((rows, d), lambda j: (0, 0)),
                  pl.BlockSpec((None, d, tn), lambda j: (layer, 0, j)),
                  pl.BlockSpec((None, 1, tn), lambda j: (layer, 0, j))],
        out_specs=pl.BlockSpec((rows, tn), lambda j: (0, j)),
        compiler_params=_params(("parallel",)),
        name="modulation",
    )(cc, w, b)


def _token_tile(n_tokens, tokens_per_row, tile=None):
    tm = min(tile or TOKEN_TILE, tokens_per_row)
    assert tokens_per_row % tm == 0 and n_tokens % tm == 0
    return tm


def _mod_spec(sub, tiles_per_row):
    return pl.BlockSpec((None, None, 3, D_MODEL), lambda i: (i // tiles_per_row, sub, 0, 0))


def _ffn_kernel(x_ref, mod_ref, w_in_ref, w_out_ref, g_ref, b_ref, o_ref, act_ref, *, n_sub):
    shift, scale, gate = mod_ref[0:1, :], mod_ref[1:2, :], mod_ref[2:3, :]
    rows = x_ref.shape[0] // n_sub

    def finish(s):
        x = x_ref[s * rows:(s + 1) * rows, :]
        y = ALPHA * x + 0.5 * gate * _dot(act_ref[s], w_out_ref[...])
        o_ref[s * rows:(s + 1) * rows, :] = _layer_norm(y, g_ref[...], b_ref[...])

    for s in range(n_sub):
        h = (x_ref[s * rows:(s + 1) * rows, :] * (1.0 + scale) + shift).astype(BF16)
        for c in range(D_FF // FF_CHUNK):
            lo = c * FF_CHUNK
            gt = _dot(h, w_in_ref[:, lo:lo + FF_CHUNK])
            up = _dot(h, w_in_ref[:, D_FF + lo:D_FF + lo + FF_CHUNK])
            act_ref[s, :, lo:lo + FF_CHUNK] = (gt * jax.nn.sigmoid(gt) * up).astype(BF16)
            if c == 0 and s > 0:
                finish(s - 1)
    finish(n_sub - 1)


def _ffn(x2, mod4, sub, tokens_per_row, w_in, w_out, layer, which, g, b):
    n = x2.shape[0]
    tm = _token_tile(n, tokens_per_row, FFN_TILE)
    n_sub = FFN_SUB_TILES if (tokens_per_row % (FFN_SUB_TILES * tm) == 0) else 1
    tb = tm * n_sub
    tile = pl.BlockSpec((tb, D_MODEL), lambda i: (i, 0))
    weight = lambda shape: pl.BlockSpec((None, None) + shape, lambda i: (layer, which, 0, 0),
                                        pipeline_mode=pl.Buffered(1))
    return pl.pallas_call(
        functools.partial(_ffn_kernel, n_sub=n_sub),
        out_shape=jax.ShapeDtypeStruct((n, D_MODEL), F32),
        grid=(n // tb,),
        in_specs=[tile, _mod_spec(sub, tokens_per_row // tb),
                  weight((D_MODEL, 2 * D_FF)), weight((D_FF, D_MODEL)),
                  _resident((1, D_MODEL)), _resident((1, D_MODEL))],
        out_specs=tile,
        scratch_shapes=[pltpu.VMEM((n_sub, tm, D_FF), BF16)],
        compiler_params=_params(("parallel",)),
        name="ffn_sublayer",
    )(x2, mod4, w_in, w_out, g, b)


def _rope(p, cos, sin):
    lane = lax.broadcasted_iota(jnp.int32, (p.shape[0], LANES), 1)
    first_half = (lane & ROPE_FREQS) == 0
    outs = []
    for j in range(p.shape[1] // LANES):
        xs = p[:, j * LANES:(j + 1) * LANES]
        partner = jnp.where(first_half, pltpu.roll(xs, LANES - ROPE_FREQS, 1), pltpu.roll(xs, ROPE_FREQS, 1))
        outs.append(xs * cos + partner * sin)
    return jnp.concatenate(outs, axis=1)


_ATTN_COLS = (("aq", 512), ("ak", 512), ("av", 512), ("bq", 512), ("bk", 256), ("bv", 128))
_ATTN_GROUPS = (("aq",), ("ak",), ("av",), ("bq",), ("bk", "bv"))


def _attn_proj_kernel(*refs, rope):
    if rope:
        x_ref, mod_ref, w_ref, cos_ref, sin_ref = refs[:5]
        outs = refs[5:]
        cos, sin = cos_ref[...], sin_ref[...]
    else:
        x_ref, mod_ref, w_ref = refs[:3]
        outs = refs[3:]
    x = x_ref[...]
    shift, scale = mod_ref[0:1, :], mod_ref[1:2, :]
    h = (x * (1.0 + scale) + shift).astype(BF16)
    widths = dict(_ATTN_COLS)
    out_refs = dict(zip(widths, outs))
    lo = 0
    for group in _ATTN_GROUPS:
        total = sum(widths[name] for name in group)
        pg = _dot(h, w_ref[:, lo:lo + total])
        lo += total
        off = 0
        for name in group:
            p = pg[:, off:off + widths[name]]
            off += widths[name]
            if rope and name in ("aq", "ak", "bq", "bk"):
                p = _rope(p, cos, sin)
            if name in ("aq", "bq"):
                p = p * (DIFF_HEAD_DIM ** -0.5 * LOG2_E)
            out_refs[name][...] = p.astype(BF16)


def _attn_proj(x2, mod4, tokens_per_row, w, tables):
    n = x2.shape[0]
    tm = _token_tile(n, tokens_per_row)
    rope = tables is not None
    tile = lambda width: pl.BlockSpec((tm, width), lambda i: (i, 0))
    in_specs = [tile(D_MODEL), _mod_spec(1, tokens_per_row // tm), _resident(w.shape)]
    args = [x2, mod4, w]
    if rope:
        tpr = tokens_per_row // tm
        tab = pl.BlockSpec((tm, LANES), lambda i: (i % tpr, 0))
        in_specs += [tab, tab]
        args += list(tables)
    return pl.pallas_call(
        functools.partial(_attn_proj_kernel, rope=rope),
        out_shape=[jax.ShapeDtypeStruct((n, width), BF16) for _, width in _ATTN_COLS],
        grid=(n // tm,),
        in_specs=in_specs,
        out_specs=[tile(width) for _, width in _ATTN_COLS],
        compiler_params=_params(("parallel",)),
        name="attn_in_proj",
    )(*args)


def _attn_out_kernel(x_ref, mod_ref, a_ref, b_ref, wa_ref, wb_ref, g_ref, bb_ref, o_ref):
    y = _dot(a_ref[...], wa_ref[...]) + _dot(b_ref[...], wb_ref[...])
    z = ALPHA * x_ref[...] + mod_ref[2:3, :] * y
    o_ref[...] = _layer_norm(z, g_ref[...], bb_ref[...])


def _attn_out(x2, mod4, tokens_per_row, a, b, wa, wb, g, bb):
    n = x2.shape[0]
    tm = _token_tile(n, tokens_per_row)
    tile = lambda width: pl.BlockSpec((tm, width), lambda i: (i, 0))
    return pl.pallas_call(
        _attn_out_kernel,
        out_shape=jax.ShapeDtypeStruct((n, D_MODEL), F32),
        grid=(n // tm,),
        in_specs=[tile(D_MODEL), _mod_spec(1, tokens_per_row // tm), tile(a.shape[1]), tile(b.shape[1]),
                  _resident(wa.shape), _resident(wb.shape), _resident((1, D_MODEL)), _resident((1, D_MODEL))],
        out_specs=tile(D_MODEL),
        compiler_params=_params(("parallel",)),
        name="attn_out_proj",
    )(x2, mod4, a, b, wa, wb, g, bb)


def _diff_attn_kernel(*refs, n_lat, lc, lambda_init):
    if n_lat:
        q_ref, k_ref, vt_ref, kc_ref, vct_ref, lam_ref, g_ref, o_ref = refs
    else:
        q_ref, kc_ref, vct_ref, lam_ref, g_ref, o_ref = refs
    t = lam_ref[...]
    lam = (jnp.exp(jnp.sum(t[0:1, :] * t[1:2, :], axis=1, keepdims=True))
           - jnp.exp(jnp.sum(t[2:3, :] * t[3:4, :], axis=1, keepdims=True)) + lambda_init)
    n_heads = q_ref.shape[1] // LANES
    stages = []
    for hd in range(n_heads):
        chunks = [(k_ref, vt_ref, c * ATTN_TK, ATTN_TK) for c in range(n_lat)] + [(kc_ref, vct_ref, 0, lc)]
        stages += [(hd, idx, idx == len(chunks) - 1, chunk) for idx, chunk in enumerate(chunks)]

    def n_pieces(stage):
        return stage[3][3] // ATTN_PIECE

    def scores(stage, piece, sub):
        hd, _, _, (kr, _, lo, _) = stage
        q = q_ref[:, hd * LANES:(hd + 1) * LANES]
        kblk = kr[lo + piece * ATTN_PIECE:lo + (piece + 1) * ATTN_PIECE, hd * LANES:(hd + 1) * LANES]
        lane = lax.broadcasted_iota(jnp.int32, kblk.shape, 1)
        keep = (lane < DIFF_HEAD_DIM) if sub == 0 else (lane >= DIFF_HEAD_DIM)
        return _dot_nt(jnp.where(keep, kblk, jnp.zeros_like(kblk)), q)

    def finish(hd, acc):
        num = [a[:DIFF_V_DIM, :] * (1.0 / a[DIFF_V_DIM:DIFF_V_DIM + 1, :]) for a in acc]
        o = (num[0] - lam * num[1]).T
        o = o * lax.rsqrt(jnp.mean(o * o, axis=-1, keepdims=True) + EPS) * g_ref[...]
        o_ref[:, hd * LANES:(hd + 1) * LANES] = (o * (1.0 - lambda_init)).astype(BF16)

    m, acc = [None, None], [None, None]
    s_next = [[scores(stages[0], j, sub) for j in range(n_pieces(stages[0]))] for sub in range(2)]
    for si, stage in enumerate(stages):
        hd, idx, is_last, (_, vr, lo, size) = stage
        vt = jnp.concatenate([vr[hd * LANES:(hd + 1) * LANES, lo:lo + size], jnp.ones((BF16_SUBLANES, size), BF16)],
                             axis=0)
        nxt = stages[si + 1] if si + 1 < len(stages) else None
        for sub in range(2):
            s_cur, s_new = s_next[sub], []
            smax = functools.reduce(jnp.maximum, [jnp.max(s, axis=0, keepdims=True) for s in s_cur])
            m_new = smax if idx == 0 else jnp.maximum(m[sub], smax)
            ps = []
            for j, s in enumerate(s_cur):
                if nxt is not None and j < n_pieces(nxt):
                    s_new.append(scores(nxt, j, sub))
                ps.append(jnp.exp2(s - m_new).astype(BF16))
            if nxt is not None:
                s_new += [scores(nxt, j, sub) for j in range(len(s_new), n_pieces(nxt))]
            part = _dot(vt, jnp.concatenate(ps, axis=0))
            acc[sub] = part if idx == 0 else jnp.exp2(m[sub] - m_new) * acc[sub] + part
            m[sub], s_next[sub] = m_new, s_new
        if is_last:
            finish(hd, acc)


def _diff_attn(q, k, vt, kc, vct, lam_vec, g, lambda_init):
    bsz, tq_total, _ = q.shape
    lc = kc.shape[1]
    tq = min(ATTN_TQ, tq_total)
    n_lat = 0 if k is None else k.shape[1] // ATTN_TK
    assert tq_total % tq == 0 and (k is None or k.shape[1] % ATTN_TK == 0)
    width = ATTN_HEADS_PER_STEP * LANES
    qspec = pl.BlockSpec((None, tq, width), lambda b, h, i: (b, i, h))
    in_specs, args = [qspec], [q]
    if n_lat:
        t = k.shape[1]
        in_specs += [pl.BlockSpec((None, t, width), lambda b, h, i: (b, 0, h)),
                     pl.BlockSpec((None, width, t), lambda b, h, i: (b, h, 0))]
        args += [k, vt]
    in_specs += [pl.BlockSpec((None, lc, width), lambda b, h, i: (b, 0, h)),
                 pl.BlockSpec((None, width, lc), lambda b, h, i: (b, h, 0)),
                 pl.BlockSpec((4, DIFF_HEAD_DIM), lambda b, h, i: (0, 0)),
                 pl.BlockSpec((1, DIFF_V_DIM), lambda b, h, i: (0, 0))]
    args += [kc, vct, lam_vec, g]
    return pl.pallas_call(
        functools.partial(_diff_attn_kernel, n_lat=n_lat, lc=lc, lambda_init=lambda_init),
        out_shape=jax.ShapeDtypeStruct((bsz, tq_total, DIFF_HEADS * DIFF_V_DIM), BF16),
        grid=(bsz, DIFF_HEADS // ATTN_HEADS_PER_STEP, tq_total // tq),
        in_specs=in_specs,
        out_specs=qspec,
        compiler_params=_params(("parallel", "parallel", "parallel")),
        name="diff_attention",
    )(*args)


def _win_attn_kernel(*refs, has_window, lc, tq):
    hd = WIN_HEAD_DIM
    if has_window:
        n_blk = tq // Q_BLOCK + 2
        q_ref = refs[0]
        k_refs, v_refs = refs[1:1 + n_blk], refs[1 + n_blk:1 + 2 * n_blk]
        kc_ref, vc_ref, sink_ref, o_ref = refs[1 + 2 * n_blk:]
        kw = jnp.concatenate([ref[...] for ref in k_refs] + [kc_ref[...]], axis=0)
        vt = jnp.concatenate([ref[...] for ref in v_refs] + [vc_ref[...]], axis=1)
        span = tq + 2 * WINDOW
        nk = span + lc
        i = pl.program_id(1)
        last = pl.num_programs(1) - 1
        c = lax.broadcasted_iota(jnp.int32, (nk, tq), 0)
        r = lax.broadcasted_iota(jnp.int32, (nk, tq), 1)
        lo_valid = jnp.where(i == 0, WINDOW, 0)
        hi_valid = jnp.where(i == last, WINDOW + tq, span)
        in_win = (c >= jnp.maximum(r, lo_valid)) & (c <= r + 2 * WINDOW) & (c < hi_valid)
        bias = jnp.where(in_win | (c >= span), 0.0, NEG_BIG)
        bias = jnp.concatenate([bias, bias], axis=1)
    else:
        q_ref, kc_ref, vc_ref, sink_ref, o_ref = refs
        kw, vt = kc_ref[...], vc_ref[...]
        nk = lc
        bias = None
    lane = lax.broadcasted_iota(jnp.int32, (nk, LANES), 1)
    zero = jnp.zeros((nk, LANES), BF16)
    ones = jnp.ones((BF16_SUBLANES, nk), BF16)
    units = [(kh, p) for kh in range(WIN_KV_HEADS) for p in range(2)]
    heads = {(kh, p): [kh * WIN_GROUP + 2 * j + p for j in range(2)] for kh, p in units}

    def scores(kh, p):
        ksrc = kw[:, (0 if p == kh else 1) * LANES:(1 if p == kh else 2) * LANES]
        kmat = jnp.where((lane < hd) if p == 0 else (lane >= hd), ksrc, zero)
        qcat = jnp.concatenate([q_ref[:, (h // 2) * LANES:(h // 2 + 1) * LANES] for h in heads[kh, p]], axis=0)
        return _dot_nt(kmat, qcat)

    out_t = {}
    s_next = scores(*units[0])
    for idx, (kh, p) in enumerate(units):
        s = s_next if bias is None else s_next + bias
        if idx + 1 < len(units):
            s_next = scores(*units[idx + 1])
        sink = jnp.concatenate([jnp.full((1, tq), sink_ref[h] * LOG2_E, F32) for h in heads[kh, p]], axis=1)
        m = jnp.maximum(jnp.max(s, axis=0, keepdims=True), sink)
        vt_ext = jnp.concatenate([vt[kh * hd:(kh + 1) * hd, :], ones], axis=0)
        acc = _dot(vt_ext, jnp.exp2(s - m).astype(BF16))
        denom = acc[hd:hd + 1, :] + jnp.exp2(sink - m)
        out_t[kh, p] = acc[:hd, :] * (1.0 / denom)
    for kh in range(WIN_KV_HEADS):
        for j in range(2):
            pair = kh * 2 + j
            pair_t = jnp.concatenate([out_t[kh, p][:, j * tq:(j + 1) * tq] for p in range(2)], axis=0)
            o_ref[:, pair * LANES:(pair + 1) * LANES] = pair_t.T.astype(BF16)


def _win_attn(q, k, vt, kc, vct, sink, has_window):
    bsz, tq_total, width = q.shape
    lc = kc.shape[1]
    tq = min(WIN_TQ, tq_total)
    assert tq_total % tq == 0 and tq % Q_BLOCK == 0
    nq = tq_total // tq
    per = tq // Q_BLOCK
    n_kblk = tq_total // Q_BLOCK
    qspec = pl.BlockSpec((None, tq, width), lambda b, i: (b, i, 0))
    in_specs, args = [qspec], [q]
    if has_window:
        def kblock(j):
            return lambda b, i: (b, jnp.clip(i * per + j - 1, 0, n_kblk - 1))
        blk = [kblock(j) for j in range(per + 2)]
        in_specs += [pl.BlockSpec((None, Q_BLOCK, 2 * LANES), lambda b, i, f=f: (*f(b, i), 0)) for f in blk]
        in_specs += [pl.BlockSpec((None, LANES, Q_BLOCK), lambda b, i, f=f: (f(b, i)[0], 0, f(b, i)[1])) for f in blk]
        args += [k] * len(blk) + [vt] * len(blk)
    in_specs += [pl.BlockSpec((None, lc, 2 * LANES), lambda b, i: (b, 0, 0)),
                 pl.BlockSpec((None, LANES, lc), lambda b, i: (b, 0, 0)),
                 pl.BlockSpec(memory_space=pltpu.SMEM)]
    args += [kc, vct, sink]
    return pl.pallas_call(
        functools.partial(_win_attn_kernel, has_window=has_window, lc=lc, tq=tq),
        out_shape=jax.ShapeDtypeStruct((bsz, tq_total, width), BF16),
        grid=(bsz, nq),
        in_specs=in_specs,
        out_specs=qspec,
        compiler_params=_params(("parallel", "parallel")),
        name="window_attention",
    )(*args)


_REC_COLS = (("mq", 512, BF16), ("mk", 512, BF16), ("mv", 512, BF16), ("mo", 512, BF16),
             ("rq", 256, BF16), ("rk", 256, BF16), ("rv", 512, BF16), ("rg", 512, BF16),
             ("gi", 128, F32), ("gf", 128, F32))
_N_DIR_HEADS = 2 * MLSTM_HEADS


def _rec_proj_kernel(x_ref, mod_ref, w_ref, gate_b_ref, *outs):
    x = x_ref[...]
    shift, scale = mod_ref[0:1, :], mod_ref[1:2, :]
    h = (x * (1.0 + scale) + shift).astype(BF16)
    lo = 0
    for (name, width, dtype), o_ref in zip(_REC_COLS, outs):
        p = _dot(h, w_ref[:, lo:lo + width])
        lo += width
        if name == "mk":
            p = p * (MLSTM_HEAD_DIM ** -0.5)
        elif name == "rk":
            p = p * (RET_QK_DIM ** -0.5)
        elif name == "gi":
            p = p + gate_b_ref[0:1, :]
        elif name == "gf":
            p = _log_sigmoid(p + gate_b_ref[1:2, :])
        o_ref[...] = p.astype(dtype)


def _rec_proj(x2, mod4, tokens_per_row, w, gate_b):
    n = x2.shape[0]
    tm = _token_tile(n, tokens_per_row)
    tile = lambda width: pl.BlockSpec((tm, width), lambda i: (i, 0))
    return pl.pallas_call(
        _rec_proj_kernel,
        out_shape=[jax.ShapeDtypeStruct((n, width), dtype) for _, width, dtype in _REC_COLS],
        grid=(n // tm,),
        in_specs=[tile(D_MODEL), _mod_spec(1, tokens_per_row // tm), _resident(w.shape), _resident((2, LANES))],
        out_specs=[tile(width) for _, width, _ in _REC_COLS],
        compiler_params=_params(("parallel",)),
        name="rec_in_proj",
    )(x2, mod4, w, gate_b)


def _scan_specs(ncc, ncl, width, rows):
    ctx_f = pl.BlockSpec((None, rows, width), lambda b, s: (b, jnp.minimum(s, ncc - 1), 0))
    ctx_b = pl.BlockSpec((None, rows, width), lambda b, s: (b, jnp.maximum(ncc - 1 - s, 0), 0))
    lat_f = pl.BlockSpec((None, rows, width), lambda b, s: (b, jnp.maximum(s - ncc, 0), 0))
    lat_b = pl.BlockSpec((None, rows, width), lambda b, s: (b, jnp.minimum(ncl - 1 - (s - ncc), ncl - 1), 0))
    return ctx_f, ctx_b, lat_f, lat_b


def _tri_masks():
    r = lax.broadcasted_iota(jnp.int32, (CHUNK, CHUNK), 0)
    c = lax.broadcasted_iota(jnp.int32, (CHUNK, CHUNK), 1)
    return r, c


def _head_cols(ref, h):
    return ref[:, h * LANES:(h + 1) * LANES]


def _mlstm_step(srcs, masks, tri, c_ref, m_ref, with_output):
    units = [(d, h) for d in range(2) for h in range(MLSTM_HEADS)]
    dv = MLSTM_HEAD_DIM
    ones = jnp.ones((CHUNK, LANES), BF16)
    v_ext = {(d, h): jnp.concatenate([_head_cols(srcs[d][2], h), ones], axis=1) for d, h in units}
    gate = []
    for d in range(2):
        gi, gf = srcs[d][3][...], srcs[d][4][...]
        cum = jnp.dot(tri[d], gf, preferred_element_type=F32, precision=lax.Precision.HIGHEST)
        b_end = cum[CHUNK - 1:CHUNK, :] if d == 0 else cum[0:1, :]
        m_prev = m_ref[d:d + 1, :]
        w_end = b_end - cum + gi
        m_new = jnp.maximum(b_end + m_prev, jnp.max(w_end, axis=0, keepdims=True))
        decay = jnp.exp(b_end + m_prev - m_new)
        w = jnp.exp(w_end - m_new)
        m_ref[d:d + 1, :] = m_new
        log_inter = cum + m_prev if with_output else None
        key_term = (gi - cum).T if with_output else None
        gate.append((cum, decay, w, log_inter, key_term))
    if with_output:
        qk, qcn, log_d, li, rmax, m_t, sm, w_inter, pv = {}, {}, {}, {}, {}, {}, {}, {}, {}
        for d, h in units:
            q = _head_cols(srcs[d][0], h)
            qk[d, h] = _dot_nt(q, _head_cols(srcs[d][1], h))
            qcn[d, h] = _dot(q, c_ref[d * MLSTM_HEADS + h].astype(BF16))
        for d, h in units:
            lane = d * MLSTM_HEADS + h
            cum, _, _, log_inter, key_term = gate[d]
            log_d[d, h] = jnp.where(masks[d], cum[:, lane:lane + 1] + key_term[lane:lane + 1, :], NEG_BIG)
            li[d, h] = log_inter[:, lane:lane + 1]
        for u in units:
            rmax[u] = jnp.max(log_d[u], axis=1, keepdims=True)
        for u in units:
            m_t[u] = jnp.maximum(li[u], rmax[u])
            sm[u] = (qk[u] * jnp.exp(log_d[u] - m_t[u])).astype(BF16)
            w_inter[u] = jnp.exp(li[u] - m_t[u])
        for u in units:
            pv[u] = _dot(sm[u], v_ext[u])
        for d, h in units:
            u = (d, h)
            both = pv[u] + w_inter[u] * qcn[u]
            den = jnp.maximum(jnp.abs(both[:, dv:]), jnp.exp(-m_t[u]))
            srcs[d][5][:, h * LANES:(h + 1) * LANES] = (both[:, :dv] * (1.0 / den)).astype(BF16)
    for d, h in units:
        row = d * MLSTM_HEADS + h
        _, decay, w, _, _ = gate[d]
        kw = (_head_cols(srcs[d][1], h).astype(F32) * w[:, row:row + 1]).astype(BF16)
        c_ref[row] = decay[:, row:row + 1] * c_ref[row] + _dot_tn(kw, v_ext[d, h])


def _ret_tables(logit_ref, intra_ref, to_end_ref, from_start_ref):
    log_gamma = _log_sigmoid(logit_ref[...])
    r, c = _tri_masks()
    rf, cf = r.astype(F32), c.astype(F32)
    for d in range(2):
        for h in range(RET_HEADS):
            row = d * RET_HEADS + h
            lg = log_gamma[d:d + 1, h:h + 1]
            rel = (rf - cf) if d == 0 else (cf - rf)
            intra_ref[row] = jnp.where(rel >= 0.0, jnp.exp(lg * jnp.maximum(rel, 0.0)), 0.0)
            to_end_ref[row] = jnp.exp(lg * ((CHUNK - 1.0 - rf) if d == 0 else rf))
            from_start_ref[row] = jnp.exp(lg * ((rf + 1.0) if d == 0 else (CHUNK - rf)))


def _ret_step(srcs, s_ref, intra_ref, to_end_ref, from_start_ref, with_output):
    units = [(d, h) for d in range(2) for h in range(RET_HEADS)]
    lane = lax.broadcasted_iota(jnp.int32, (CHUNK, LANES), 1)

    def k_head(d, h):
        pair = _head_cols(srcs[d][1], h // 2)
        keep = (lane < RET_QK_DIM) if h % 2 == 0 else (lane >= RET_QK_DIM)
        return jnp.where(keep, pair, jnp.zeros_like(pair))

    ks = {u: k_head(*u) for u in units}
    qk, qs = {}, {}
    if with_output:
        for d, h in units:
            q = _head_cols(srcs[d][0], h // 2)
            qk[d, h] = _dot_nt(q, ks[d, h])
            qs[d, h] = _dot(q, s_ref[d * RET_HEADS + h].astype(BF16))
        for d, h in units:
            row = d * RET_HEADS + h
            sc = (qk[d, h] * intra_ref[row]).astype(BF16)
            out = _dot(sc, _head_cols(srcs[d][2], h)) + from_start_ref[row] * qs[d, h]
            srcs[d][3][:, h * LANES:(h + 1) * LANES] = out.astype(BF16)
    for d, h in units:
        row = d * RET_HEADS + h
        kd = (ks[d, h].astype(F32) * to_end_ref[row]).astype(BF16)
        chunk_decay = from_start_ref[row, CHUNK - 1:CHUNK, :] if d == 0 else from_start_ref[row, 0:1, :]
        s_ref[row] = chunk_decay * s_ref[row] + _dot_tn(kd, _head_cols(srcs[d][2], h))


def _rec_scan_kernel(mkc_f, mvc_f, gic_f, gfc_f, mkc_b, mvc_b, gic_b, gfc_b,
                     mq_f, mk_f, mv_f, gi_f, gf_f, mq_b, mk_b, mv_b, gi_b, gf_b,
                     rkc_f, rvc_f, rkc_b, rvc_b, rq_f, rk_f, rv_f, rq_b, rk_b, rv_b, logit_ref,
                     mo_f, mo_b, ro_f, ro_b,
                     c_ref, m_ref, s_ref, intra_ref, to_end_ref, from_start_ref, *, ncc):
    s = pl.program_id(1)

    @pl.when(s == 0)
    def _():
        c_ref[...] = jnp.zeros_like(c_ref)
        m_ref[...] = jnp.zeros_like(m_ref)
        s_ref[...] = jnp.zeros_like(s_ref)
        _ret_tables(logit_ref, intra_ref, to_end_ref, from_start_ref)

    r, c = _tri_masks()
    masks = (c <= r, c >= r)
    tri = (jnp.where(masks[0], 1.0, 0.0), jnp.where(masks[1], 1.0, 0.0))

    def run(m_f, m_b, r_f, r_b, with_output):
        for sub in range(SCAN_CHUNKS):
            lo = (sub * CHUNK, (SCAN_CHUNKS - 1 - sub) * CHUNK)
            rows = lambda refs, d: tuple(None if ref is None else _Rows(ref, lo[d]) for ref in refs)
            _mlstm_step((rows(m_f, 0), rows(m_b, 1)), masks, tri, c_ref, m_ref, with_output)
            _ret_step((rows(r_f, 0), rows(r_b, 1)), s_ref, intra_ref, to_end_ref, from_start_ref, with_output)

    @pl.when(s < ncc)
    def _():
        run((None, mkc_f, mvc_f, gic_f, gfc_f, None), (None, mkc_b, mvc_b, gic_b, gfc_b, None),
            (None, rkc_f, rvc_f, None), (None, rkc_b, rvc_b, None), False)

    @pl.when(s >= ncc)
    def _():
        run((mq_f, mk_f, mv_f, gi_f, gf_f, mo_f), (mq_b, mk_b, mv_b, gi_b, gf_b, mo_b),
            (rq_f, rk_f, rv_f, ro_f), (rq_b, rk_b, rv_b, ro_b), True)


class _Rows:
    def __init__(self, ref, lo):
        self.ref, self.rows = ref, slice(lo, lo + CHUNK)

    def __getitem__(self, idx):
        return self.ref[self.rows, :] if idx is Ellipsis else self.ref[self.rows, idx[1]]

    def __setitem__(self, idx, value):
        self.ref[self.rows, idx[1]] = value


def _rec_scan(mq, mk, mv, gi, gf, rq, rk, rv, mkc, mvc, gic, gfc, rkc, rvc, logit):
    bsz, t, width = mq.shape
    rows = CHUNK * SCAN_CHUNKS
    assert t % rows == 0 and mkc.shape[1] % rows == 0
    ncl, ncc = t // rows, mkc.shape[1] // rows
    cf, cb, lf, lb = _scan_specs(ncc, ncl, width, rows)
    gcf, gcb, glf, glb = _scan_specs(ncc, ncl, LANES, rows)
    kcf, kcb, klf, klb = _scan_specs(ncc, ncl, rq.shape[2], rows)
    return pl.pallas_call(
        functools.partial(_rec_scan_kernel, ncc=ncc),
        out_shape=[jax.ShapeDtypeStruct((bsz, t, width), BF16)] * 4,
        grid=(bsz, ncc + ncl),
        in_specs=[cf, cf, gcf, gcf, cb, cb, gcb, gcb, lf, lf, lf, glf, glf, lb, lb, lb, glb, glb,
                  kcf, cf, kcb, cb, klf, klf, lf, klb, klb, lb, pl.BlockSpec((8, LANES), lambda b, s: (0, 0))],
        out_specs=[lf, lb, lf, lb],
        scratch_shapes=[pltpu.VMEM((_N_DIR_HEADS, MLSTM_HEAD_DIM, 2 * MLSTM_HEAD_DIM), F32),
                        pltpu.VMEM((8, LANES), F32),
                        pltpu.VMEM((2 * RET_HEADS, LANES, RET_V_DIM), F32)]
        + [pltpu.VMEM((2 * RET_HEADS, CHUNK, LANES), F32)] * 3,
        compiler_params=_params(("parallel", "arbitrary")),
        name="rec_scan",
    )(mkc, mvc, gic, gfc, mkc, mvc, gic, gfc, mq, mk, mv, gi, gf, mq, mk, mv, gi, gf,
      rkc, rvc, rkc, rvc, rq, rk, rv, rq, rk, rv, logit)


def _rec_out_kernel(x_ref, mod_ref, mf_ref, mb_ref, rf_ref, rb_ref, o_ref_in, rg_ref, mg_ref, rng_ref, rnb_ref,
                    w_ref, g_ref, b_ref, out_ref):
    hm = mf_ref[...].astype(F32) + mb_ref[...].astype(F32)
    hr = rf_ref[...].astype(F32) + rb_ref[...].astype(F32)
    o_gate = jax.nn.sigmoid(o_ref_in[...].astype(F32))
    rg = rg_ref[...].astype(F32)
    r_gate = rg * jax.nn.sigmoid(rg)
    parts_m, parts_r = [], []
    for h in range(MLSTM_HEADS):
        cols = slice(h * LANES, (h + 1) * LANES)
        parts_m.append(_layer_norm(hm[:, cols], mg_ref[:, cols], None) * o_gate[:, cols])
        parts_r.append(_layer_norm(hr[:, cols], rng_ref[:, cols], rnb_ref[:, cols]) * r_gate[:, cols])
    width = MLSTM_HEADS * MLSTM_HEAD_DIM
    y = (_dot(jnp.concatenate(parts_m, axis=1).astype(BF16), w_ref[0:width, :])
         + _dot(jnp.concatenate(parts_r, axis=1).astype(BF16), w_ref[width:, :]))
    z = ALPHA * x_ref[...] + mod_ref[2:3, :] * y
    out_ref[...] = _layer_norm(z, g_ref[...], b_ref[...])


def _rec_out(x2, mod4, tokens_per_row, mf, mb, rf, rb, o, rg, mg, rng, rnb, w, g, b):
    n = x2.shape[0]
    tm = _token_tile(n, tokens_per_row)
    tile = lambda width: pl.BlockSpec((tm, width), lambda i: (i, 0))
    half = tile(512)
    return pl.pallas_call(
        _rec_out_kernel,
        out_shape=jax.ShapeDtypeStruct((n, D_MODEL), F32),
        grid=(n // tm,),
        in_specs=[tile(D_MODEL), _mod_spec(1, tokens_per_row // tm), half, half, half, half, half, half,
                  _resident((1, 512)), _resident((1, 512)), _resident((1, 512)),
                  _resident(w.shape), _resident((1, D_MODEL)), _resident((1, D_MODEL))],
        out_specs=tile(D_MODEL),
        compiler_params=_params(("parallel",)),
        name="rec_out_proj",
    )(x2, mod4, mf, mb, rf, rb, o, rg, mg, rng, rnb, w, g, b)


def _attn_weights(w_in, w_out):
    aq, ak, av, bq, bk, bv = jnp.split(w_in, [512, 1024, 1536, 2048, 2176], axis=1)
    bk_swapped = jnp.concatenate([bk[:, WIN_HEAD_DIM:], bk[:, :WIN_HEAD_DIM]], axis=1)
    w = jnp.concatenate([aq, ak, av, bq, bk, bk_swapped, bv], axis=1)
    wa, wb = w_out[:DIFF_HEADS * DIFF_V_DIM], w_out[DIFF_HEADS * DIFF_V_DIM:]
    return w.astype(BF16), wa.astype(BF16), wb.astype(BF16)


def _rec_weights(w_in):
    mq, mk, mv, mo, mg, rq, rk, rv, rg = jnp.split(w_in, [512, 1024, 1536, 2048, 2064, 2320, 2576, 3088], axis=1)
    gi, gf = _split_gates(mg)
    w = jnp.concatenate([mq, mk, mv, mo, rq, rk, rv, rg, gi, gf], axis=1)
    return w.astype(BF16)


def _split_gates(g):
    g = g.reshape(*g.shape[:-1], 2, 2, MLSTM_HEADS)
    pad = [(0, 0)] * (g.ndim - 3) + [(0, LANES - _N_DIR_HEADS)]
    return tuple(jnp.pad(g[..., io, :].reshape(*g.shape[:-3], _N_DIR_HEADS), pad) for io in range(2))


def _rope_tables(t):
    rows = t // GRID_W
    row = jnp.repeat(jnp.arange(rows), GRID_W)
    col = jnp.tile(jnp.arange(GRID_W), rows)
    inv = ROPE_BASE ** (-jnp.arange(ROPE_FREQS, dtype=F32) / ROPE_FREQS)
    ang_r, ang_c = row[:, None] * inv, col[:, None] * inv
    cos = jnp.concatenate([jnp.cos(ang_r)] * 2 + [jnp.cos(ang_c)] * 2, axis=1)
    sin = jnp.concatenate([-jnp.sin(ang_r), jnp.sin(ang_r), -jnp.sin(ang_c), jnp.sin(ang_c)], axis=1)
    return jnp.tile(cos, (1, 2)), jnp.tile(sin, (1, 2))


def kernel(x, c, ctx, c_ctx, ada_w, ada_b, ln_g, ln_b, ffn_w_in, ffn_w_out, attn_w_in, attn_w_out,
           diff_lambda, diff_norm_g, sink_logits, rec_w_in, rec_w_out, mlstm_gate_b, mlstm_norm_g,
           ret_decay_logit, ret_norm_g, ret_norm_b):
    bsz, t, d = x.shape
    lc = ctx.shape[1]
    assert d == D_MODEL and t % ATTN_TK == 0 and lc % CHUNK == 0
    out_dtype = x.dtype
    x2 = x.reshape(bsz * t, d).astype(F32)
    c2 = ctx.reshape(bsz * lc, d).astype(F32)
    mod_rows = 16
    cc = jnp.zeros((mod_rows, d), F32).at[:bsz].set(c).at[bsz].set(c_ctx)
    row2 = lambda v: v.reshape(1, -1).astype(F32)
    ada_b3 = ada_b.reshape(DEPTH, 1, -1)
    ffn_w_in16, ffn_w_out16 = ffn_w_in.astype(BF16), ffn_w_out.astype(BF16)

    for l in range(DEPTH):
        last = l == DEPTH - 1
        i = l // 2
        mod = _modulation(cc, ada_w, ada_b3, l).reshape(mod_rows, 3, 3, d)
        mod_x, mod_c = mod[:bsz], mod[bsz:bsz + 1]
        ffn = lambda z, m, sub, tpr, j: _ffn(z, m, sub, tpr, ffn_w_in16, ffn_w_out16, l, j,
                                             row2(ln_g[l, sub]), row2(ln_b[l, sub]))
        x2 = ffn(x2, mod_x, 0, t, 0)
        c2 = ffn(c2, mod_c, 0, bsz * lc, 0)
        g1, b1 = row2(ln_g[l, 1]), row2(ln_b[l, 1])
        if l % 2 == 0:
            lambda_init = 0.8 - 0.6 * math.exp(-0.3 * l)
            w, wa, wb = _attn_weights(attn_w_in[i], attn_w_out[i])
            aq, ak, av, bq, bk, bv = [a.reshape(bsz, t, -1) for a in _attn_proj(x2, mod_x, t, w, _rope_tables(t))]
            aqc, akc, avc, bqc, bkc, bvc = [a.reshape(bsz, lc, -1) for a in _attn_proj(c2, mod_c, bsz * lc, w, None)]
            avt, avct = jnp.swapaxes(av, 1, 2), jnp.swapaxes(avc, 1, 2)
            bvt, bvct = jnp.swapaxes(bv, 1, 2), jnp.swapaxes(bvc, 1, 2)
            lam_vec, sub_g = diff_lambda[i].astype(F32), row2(diff_norm_g[i])
            sink = sink_logits[i].astype(F32)
            a_x = _diff_attn(aq, ak, avt, akc, avct, lam_vec, sub_g, lambda_init)
            b_x = _win_attn(bq, bk, bvt, bkc, bvct, sink, True)
            flat = lambda a: a.reshape(-1, a.shape[-1])
            if not last:
                a_c = _diff_attn(aqc, None, None, akc, avct, lam_vec, sub_g, lambda_init)
                b_c = _win_attn(bqc, None, None, bkc, bvct, sink, False)
                c2_mix = _attn_out(c2, mod_c, bsz * lc, flat(a_c), flat(b_c), wa, wb, g1, b1)
            x2 = _attn_out(x2, mod_x, t, flat(a_x), flat(b_x), wa, wb, g1, b1)
        else:
            w = _rec_weights(rec_w_in[i])
            gate_b = jnp.stack(_split_gates(mlstm_gate_b[i].reshape(-1).astype(F32)))
            px = [a.reshape(bsz, t, -1) for a in _rec_proj(x2, mod_x, t, w, gate_b)]
            pc = [a.reshape(bsz, lc, -1) for a in _rec_proj(c2, mod_c, bsz * lc, w, gate_b)]
            mq, mk, mv, mo, rq, rk, rv, rg, gi, gf = px
            _, mkc, mvc, _, _, rkc, rvc, _, gic, gfc = pc
            assert last
            logit = jnp.zeros((8, LANES), F32).at[:2, :RET_HEADS].set(ret_decay_logit[i].astype(F32))
            m_f, m_b, r_f, r_b = _rec_scan(mq, mk, mv, gi, gf, rq, rk, rv, mkc, mvc, gic, gfc, rkc, rvc, logit)
            flat = lambda a: a.reshape(-1, a.shape[-1])
            x2 = _rec_out(x2, mod_x, t, flat(m_f), flat(m_b), flat(r_f), flat(r_b), flat(mo), flat(rg),
                          row2(mlstm_norm_g[i]), row2(ret_norm_g[i]), row2(ret_norm_b[i]),
                          rec_w_out[i].astype(BF16), g1, b1)
        x2 = ffn(x2, mod_x, 2, t, 1)
        if not last:
            c2 = ffn(c2_mix, mod_c, 2, bsz * lc, 1)
    return x2.reshape(bsz, t, d).astype(out_dtype)
```

```python
import functools
import math

import jax
import jax.numpy as jnp
from jax import lax
from jax.experimental import pallas as pl
from jax.experimental.pallas import tpu as pltpu

F32 = jnp.float32
BF16 = jnp.bfloat16

D_MODEL = 1024
DEPTH = 2
GRID_W = 64
D_FF = 2816
DIFF_HEADS = 4
DIFF_HEAD_DIM = 64
DIFF_V_DIM = 128
WIN_HEADS = 8
WIN_KV_HEADS = 2
WIN_GROUP = WIN_HEADS // WIN_KV_HEADS
WIN_HEAD_DIM = 64
WINDOW = 128
Q_BLOCK = 128
ROPE_BASE = 10000.0
ROPE_FREQS = 16
MLSTM_HEADS = 4
MLSTM_HEAD_DIM = 128
RET_HEADS = 4
RET_QK_DIM = 64
RET_V_DIM = 128
CHUNK = 128
ALPHA = (2.0 * DEPTH) ** 0.25
EPS = 1e-5
LOG2_E = 1.4426950408889634
NEG_BIG = -1e30

LANES = 128
BF16_SUBLANES = 16
V7X_VMEM_LIMIT_BYTES = 56 * 1024 * 1024

TOKEN_TILE = 1024
FFN_TILE = 512
FF_CHUNK = 256
FFN_SUB_TILES = 2
ATTN_TQ = 512
ATTN_TK = 1024
ATTN_PIECE = 256
ATTN_HEADS_PER_STEP = 2
WIN_TQ = 256
SCAN_CHUNKS = 2
SCAN_BATCH = 2


def _params(semantics):
    return pltpu.CompilerParams(dimension_semantics=semantics, vmem_limit_bytes=V7X_VMEM_LIMIT_BYTES)


def _resident(shape):
    return pl.BlockSpec(shape, lambda *_: (0,) * len(shape), pipeline_mode=pl.Buffered(1))


def _layer_norm(y, g, b):
    mu = jnp.mean(y, axis=-1, keepdims=True)
    yc = y - mu
    var = jnp.mean(yc * yc, axis=-1, keepdims=True)
    out = yc * lax.rsqrt(var + EPS) * g
    return out if b is None else out + b


def _log_sigmoid(x):
    return jnp.minimum(x, 0.0) - jnp.log1p(jnp.exp(-jnp.abs(x)))


def _dot(a, b):
    return jnp.dot(a, b, preferred_element_type=F32)


def _dot_nt(a, b):
    return lax.dot_general(a, b, (((1,), (1,)), ((), ())), preferred_element_type=F32)


def _dot_tn(a, b):
    return lax.dot_general(a, b, (((0,), (0,)), ((), ())), preferred_element_type=F32)


def _split_bf16(a):
    hi = a.astype(BF16)
    return hi, (a - hi.astype(F32)).astype(BF16)


def _mod_kernel(c_ref, w_ref, b_ref, o_ref):
    c = c_ref[...]
    h_hi, h_lo = _split_bf16(c * jax.nn.sigmoid(c))
    w_hi, w_lo = _split_bf16(w_ref[...])
    o_ref[...] = _dot(h_hi, w_hi) + (_dot(h_hi, w_lo) + _dot(h_lo, w_hi) + _dot(h_lo, w_lo)) + b_ref[...]


def _modulation(cc, w, b, layer):
    rows, d = cc.shape
    n = w.shape[2]
    tn = 1024
    return pl.pallas_call(
        _mod_kernel,
        out_shape=jax.ShapeDtypeStruct((rows, n), F32),
        grid=(n // tn,),
        in_specs=[pl.BlockSpec((rows, d), lambda j: (0, 0)),
                  pl.BlockSpec((None, d, tn), lambda j: (layer, 0, j)),
                  pl.BlockSpec((None, 1, tn), lambda j: (layer, 0, j))],
        out_specs=pl.BlockSpec((rows, tn), lambda j: (0, j)),
        compiler_params=_params(("parallel",)),
        name="modulation",
    )(cc, w, b)


def _token_tile(n_tokens, tokens_per_row, tile=None):
    tm = min(tile or TOKEN_TILE, tokens_per_row)
    assert tokens_per_row % tm == 0 and n_tokens % tm == 0
    return tm


def _mod_spec(sub, tiles_per_row):
    return pl.BlockSpec((None, None, 3, D_MODEL), lambda i: (i // tiles_per_row, sub, 0, 0))


def _ffn_kernel(x_ref, mod_ref, w_in_ref, w_out_ref, g_ref, b_ref, o_ref, act_ref, *, n_sub):
    shift, scale, gate = mod_ref[0:1, :], mod_ref[1:2, :], mod_ref[2:3, :]
    rows = x_ref.shape[0] // n_sub

    def finish(s):
        x = x_ref[s * rows:(s + 1) * rows, :]
        y = ALPHA * x + 0.5 * gate * _dot(act_ref[s], w_out_ref[...])
        o_ref[s * rows:(s + 1) * rows, :] = _layer_norm(y, g_ref[...], b_ref[...])

    for s in range(n_sub):
        h = (x_ref[s * rows:(s + 1) * rows, :] * (1.0 + scale) + shift).astype(BF16)
        for c in range(D_FF // FF_CHUNK):
            lo = c * FF_CHUNK
            gt = _dot(h, w_in_ref[:, lo:lo + FF_CHUNK])
            up = _dot(h, w_in_ref[:, D_FF + lo:D_FF + lo + FF_CHUNK])
            act_ref[s, :, lo:lo + FF_CHUNK] = (gt * jax.nn.sigmoid(gt) * up).astype(BF16)
            if c == 0 and s > 0:
                finish(s - 1)
    finish(n_sub - 1)


def _ffn(x2, mod4, sub, tokens_per_row, w_in, w_out, layer, which, g, b):
    n = x2.shape[0]
    tm = _token_tile(n, tokens_per_row, FFN_TILE)
    n_sub = FFN_SUB_TILES if (tokens_per_row % (FFN_SUB_TILES * tm) == 0) else 1
    tb = tm * n_sub
    tile = pl.BlockSpec((tb, D_MODEL), lambda i: (i, 0))
    weight = lambda shape: pl.BlockSpec((None, None) + shape, lambda i: (layer, which, 0, 0),
                                        pipeline_mode=pl.Buffered(1))
    return pl.pallas_call(
        functools.partial(_ffn_kernel, n_sub=n_sub),
        out_shape=jax.ShapeDtypeStruct((n, D_MODEL), F32),
        grid=(n // tb,),
        in_specs=[tile, _mod_spec(sub, tokens_per_row // tb),
                  weight((D_MODEL, 2 * D_FF)), weight((D_FF, D_MODEL)),
                  _resident((1, D_MODEL)), _resident((1, D_MODEL))],
        out_specs=tile,
        scratch_shapes=[pltpu.VMEM((n_sub, tm, D_FF), BF16)],
        compiler_params=_params(("parallel",)),
        name="ffn_sublayer",
    )(x2, mod4, w_in, w_out, g, b)


def _rope(p, cos, sin):
    lane = lax.broadcasted_iota(jnp.int32, (p.shape[0], LANES), 1)
    first_half = (lane & ROPE_FREQS) == 0
    outs = []
    for j in range(p.shape[1] // LANES):
        xs = p[:, j * LANES:(j + 1) * LANES]
        partner = jnp.where(first_half, pltpu.roll(xs, LANES - ROPE_FREQS, 1), pltpu.roll(xs, ROPE_FREQS, 1))
        outs.append(xs * cos + partner * sin)
    return jnp.concatenate(outs, axis=1)


_ATTN_COLS = (("aq", 512), ("ak", 512), ("av", 512), ("bq", 512), ("bk", 256), ("bv", 128))
_ATTN_GROUPS = (("aq",), ("ak",), ("av",), ("bq",), ("bk", "bv"))


def _attn_proj_kernel(*refs, rope):
    if rope:
        x_ref, mod_ref, w_ref, cos_ref, sin_ref = refs[:5]
        outs = refs[5:]
        cos, sin = cos_ref[...], sin_ref[...]
    else:
        x_ref, mod_ref, w_ref = refs[:3]
        outs = refs[3:]
    x = x_ref[...]
    shift, scale = mod_ref[0:1, :], mod_ref[1:2, :]
    h = (x * (1.0 + scale) + shift).astype(BF16)
    widths = dict(_ATTN_COLS)
    out_refs = dict(zip(widths, outs))
    lo = 0
    for group in _ATTN_GROUPS:
        total = sum(widths[name] for name in group)
        pg = _dot(h, w_ref[:, lo:lo + total])
        lo += total
        off = 0
        for name in group:
            p = pg[:, off:off + widths[name]]
            off += widths[name]
            if rope and name in ("aq", "ak", "bq", "bk"):
                p = _rope(p, cos, sin)
            if name in ("aq", "bq"):
                p = p * (DIFF_HEAD_DIM ** -0.5 * LOG2_E)
            out_refs[name][...] = p.astype(BF16)


def _attn_proj(x2, mod4, tokens_per_row, w, tables):
    n = x2.shape[0]
    tm = _token_tile(n, tokens_per_row)
    rope = tables is not None
    tile = lambda width: pl.BlockSpec((tm, width), lambda i: (i, 0))
    in_specs = [tile(D_MODEL), _mod_spec(1, tokens_per_row // tm), _resident(w.shape)]
    args = [x2, mod4, w]
    if rope:
        tpr = tokens_per_row // tm
        tab = pl.BlockSpec((tm, LANES), lambda i: (i % tpr, 0))
        in_specs += [tab, tab]
        args += list(tables)
    return pl.pallas_call(
        functools.partial(_attn_proj_kernel, rope=rope),
        out_shape=[jax.ShapeDtypeStruct((n, width), BF16) for _, width in _ATTN_COLS],
        grid=(n // tm,),
        in_specs=in_specs,
        out_specs=[tile(width) for _, width in _ATTN_COLS],
        compiler_params=_params(("parallel",)),
        name="attn_in_proj",
    )(*args)


def _attn_out_kernel(x_ref, mod_ref, a_ref, b_ref, wa_ref, wb_ref, g_ref, bb_ref, o_ref):
    y = _dot(a_ref[...], wa_ref[...]) + _dot(b_ref[...], wb_ref[...])
    z = ALPHA * x_ref[...] + mod_ref[2:3, :] * y
    o_ref[...] = _layer_norm(z, g_ref[...], bb_ref[...])


def _attn_out(x2, mod4, tokens_per_row, a, b, wa, wb, g, bb):
    n = x2.shape[0]
    tm = _token_tile(n, tokens_per_row)
    tile = lambda width: pl.BlockSpec((tm, width), lambda i: (i, 0))
    return pl.pallas_call(
        _attn_out_kernel,
        out_shape=jax.ShapeDtypeStruct((n, D_MODEL), F32),
        grid=(n // tm,),
        in_specs=[tile(D_MODEL), _mod_spec(1, tokens_per_row // tm), tile(a.shape[1]), tile(b.shape[1]),
                  _resident(wa.shape), _resident(wb.shape), _resident((1, D_MODEL)), _resident((1, D_MODEL))],
        out_specs=tile(D_MODEL),
        compiler_params=_params(("parallel",)),
        name="attn_out_proj",
    )(x2, mod4, a, b, wa, wb, g, bb)


def _diff_attn_kernel(*refs, n_lat, lc, lambda_init):
    if n_lat:
        q_ref, k_ref, vt_ref, kc_ref, vct_ref, lam_ref, g_ref, o_ref = refs
    else:
        q_ref, kc_ref, vct_ref, lam_ref, g_ref, o_ref = refs
    t = lam_ref[...]
    lam = (jnp.exp(jnp.sum(t[0:1, :] * t[1:2, :], axis=1, keepdims=True))
           - jnp.exp(jnp.sum(t[2:3, :] * t[3:4, :], axis=1, keepdims=True)) + lambda_init)
    n_heads = q_ref.shape[1] // LANES
    stages = []
    for hd in range(n_heads):
        chunks = [(k_ref, vt_ref, c * ATTN_TK, ATTN_TK) for c in range(n_lat)] + [(kc_ref, vct_ref, 0, lc)]
        stages += [(hd, idx, idx == len(chunks) - 1, chunk) for idx, chunk in enumerate(chunks)]

    def n_pieces(stage):
        return stage[3][3] // ATTN_PIECE

    def scores(stage, piece, sub):
        hd, _, _, (kr, _, lo, _) = stage
        q = q_ref[:, hd * LANES:(hd + 1) * LANES]
        kblk = kr[lo + piece * ATTN_PIECE:lo + (piece + 1) * ATTN_PIECE, hd * LANES:(hd + 1) * LANES]
        lane = lax.broadcasted_iota(jnp.int32, kblk.shape, 1)
        keep = (lane < DIFF_HEAD_DIM) if sub == 0 else (lane >= DIFF_HEAD_DIM)
        return _dot_nt(jnp.where(keep, kblk, jnp.zeros_like(kblk)), q)

    def finish(hd, acc):
        num = [a[:DIFF_V_DIM, :] * (1.0 / a[DIFF_V_DIM:DIFF_V_DIM + 1, :]) for a in acc]
        o = (num[0] - lam * num[1]).T
        o = o * lax.rsqrt(jnp.mean(o * o, axis=-1, keepdims=True) + EPS) * g_ref[...]
        o_ref[:, hd * LANES:(hd + 1) * LANES] = (o * (1.0 - lambda_init)).astype(BF16)

    m, acc = [None, None], [None, None]
    s_next = [[scores(stages[0], j, sub) for j in range(n_pieces(stages[0]))] for sub in range(2)]
    for si, stage in enumerate(stages):
        hd, idx, is_last, (_, vr, lo, size) = stage
        vt = jnp.concatenate([vr[hd * LANES:(hd + 1) * LANES, lo:lo + size], jnp.ones((BF16_SUBLANES, size), BF16)],
                             axis=0)
        nxt = stages[si + 1] if si + 1 < len(stages) else None
        for sub in range(2):
            s_cur, s_new = s_next[sub], []
            smax = functools.reduce(jnp.maximum, [jnp.max(s, axis=0, keepdims=True) for s in s_cur])
            m_new = smax if idx == 0 else jnp.maximum(m[sub], smax)
            ps = []
            for j, s in enumerate(s_cur):
                if nxt is not None and j < n_pieces(nxt):
                    s_new.append(scores(nxt, j, sub))
                ps.append(jnp.exp2(s - m_new).astype(BF16))
            if nxt is not None:
                s_new += [scores(nxt, j, sub) for j in range(len(s_new), n_pieces(nxt))]
            part = _dot(vt, jnp.concatenate(ps, axis=0))
            acc[sub] = part if idx == 0 else jnp.exp2(m[sub] - m_new) * acc[sub] + part
            m[sub], s_next[sub] = m_new, s_new
        if is_last:
            finish(hd, acc)


def _diff_attn(q, k, vt, kc, vct, lam_vec, g, lambda_init):
    bsz, tq_total, _ = q.shape
    lc = kc.shape[1]
    tq = min(ATTN_TQ, tq_total)
    n_lat = 0 if k is None else k.shape[1] // ATTN_TK
    assert tq_total % tq == 0 and (k is None or k.shape[1] % ATTN_TK == 0)
    width = ATTN_HEADS_PER_STEP * LANES
    qspec = pl.BlockSpec((None, tq, width), lambda b, h, i: (b, i, h))
    in_specs, args = [qspec], [q]
    if n_lat:
        t = k.shape[1]
        in_specs += [pl.BlockSpec((None, t, width), lambda b, h, i: (b, 0, h)),
                     pl.BlockSpec((None, width, t), lambda b, h, i: (b, h, 0))]
        args += [k, vt]
    in_specs += [pl.BlockSpec((None, lc, width), lambda b, h, i: (b, 0, h)),
                 pl.BlockSpec((None, width, lc), lambda b, h, i: (b, h, 0)),
                 pl.BlockSpec((4, DIFF_HEAD_DIM), lambda b, h, i: (0, 0)),
                 pl.BlockSpec((1, DIFF_V_DIM), lambda b, h, i: (0, 0))]
    args += [kc, vct, lam_vec, g]
    return pl.pallas_call(
        functools.partial(_diff_attn_kernel, n_lat=n_lat, lc=lc, lambda_init=lambda_init),
        out_shape=jax.ShapeDtypeStruct((bsz, tq_total, DIFF_HEADS * DIFF_V_DIM), BF16),
        grid=(bsz, DIFF_HEADS // ATTN_HEADS_PER_STEP, tq_total // tq),
        in_specs=in_specs,
        out_specs=qspec,
        compiler_params=_params(("parallel", "parallel", "parallel")),
        name="diff_attention",
    )(*args)


def _win_attn_kernel(*refs, has_window, lc, tq):
    hd = WIN_HEAD_DIM
    if has_window:
        n_blk = tq // Q_BLOCK + 2
        q_ref = refs[0]
        k_refs, v_refs = refs[1:1 + n_blk], refs[1 + n_blk:1 + 2 * n_blk]
        kc_ref, vc_ref, sink_ref, o_ref = refs[1 + 2 * n_blk:]
        kw = jnp.concatenate([ref[...] for ref in k_refs] + [kc_ref[...]], axis=0)
        vt = jnp.concatenate([ref[...] for ref in v_refs] + [vc_ref[...]], axis=1)
        span = tq + 2 * WINDOW
        nk = span + lc
        i = pl.program_id(1)
        last = pl.num_programs(1) - 1
        c = lax.broadcasted_iota(jnp.int32, (nk, tq), 0)
        r = lax.broadcasted_iota(jnp.int32, (nk, tq), 1)
        lo_valid = jnp.where(i == 0, WINDOW, 0)
        hi_valid = jnp.where(i == last, WINDOW + tq, span)
        in_win = (c >= jnp.maximum(r, lo_valid)) & (c <= r + 2 * WINDOW) & (c < hi_valid)
        bias = jnp.where(in_win | (c >= span), 0.0, NEG_BIG)
        bias = jnp.concatenate([bias, bias], axis=1)
    else:
        q_ref, kc_ref, vc_ref, sink_ref, o_ref = refs
        kw, vt = kc_ref[...], vc_ref[...]
        nk = lc
        bias = None
    lane = lax.broadcasted_iota(jnp.int32, (nk, LANES), 1)
    zero = jnp.zeros((nk, LANES), BF16)
    ones = jnp.ones((BF16_SUBLANES, nk), BF16)
    units = [(kh, p) for kh in range(WIN_KV_HEADS) for p in range(2)]
    heads = {(kh, p): [kh * WIN_GROUP + 2 * j + p for j in range(2)] for kh, p in units}

    def scores(kh, p):
        ksrc = kw[:, (0 if p == kh else 1) * LANES:(1 if p == kh else 2) * LANES]
        kmat = jnp.where((lane < hd) if p == 0 else (lane >= hd), ksrc, zero)
        qcat = jnp.concatenate([q_ref[:, (h // 2) * LANES:(h // 2 + 1) * LANES] for h in heads[kh, p]], axis=0)
        return _dot_nt(kmat, qcat)

    out_t = {}
    s_next = scores(*units[0])
    for idx, (kh, p) in enumerate(units):
        s = s_next if bias is None else s_next + bias
        if idx + 1 < len(units):
            s_next = scores(*units[idx + 1])
        sink = jnp.concatenate([jnp.full((1, tq), sink_ref[h] * LOG2_E, F32) for h in heads[kh, p]], axis=1)
        m = jnp.maximum(jnp.max(s, axis=0, keepdims=True), sink)
        vt_ext = jnp.concatenate([vt[kh * hd:(kh + 1) * hd, :], ones], axis=0)
        acc = _dot(vt_ext, jnp.exp2(s - m).astype(BF16))
        denom = acc[hd:hd + 1, :] + jnp.exp2(sink - m)
        out_t[kh, p] = acc[:hd, :] * (1.0 / denom)
    for kh in range(WIN_KV_HEADS):
        for j in range(2):
            pair = kh * 2 + j
            pair_t = jnp.concatenate([out_t[kh, p][:, j * tq:(j + 1) * tq] for p in range(2)], axis=0)
            o_ref[:, pair * LANES:(pair + 1) * LANES] = pair_t.T.astype(BF16)


def _win_attn(q, k, vt, kc, vct, sink, has_window):
    bsz, tq_total, width = q.shape
    lc = kc.shape[1]
    tq = min(WIN_TQ, tq_total)
    assert tq_total % tq == 0 and tq % Q_BLOCK == 0
    nq = tq_total // tq
    per = tq // Q_BLOCK
    n_kblk = tq_total // Q_BLOCK
    qspec = pl.BlockSpec((None, tq, width), lambda b, i: (b, i, 0))
    in_specs, args = [qspec], [q]
    if has_window:
        def kblock(j):
            return lambda b, i: (b, jnp.clip(i * per + j - 1, 0, n_kblk - 1))
        blk = [kblock(j) for j in range(per + 2)]
        in_specs += [pl.BlockSpec((None, Q_BLOCK, 2 * LANES), lambda b, i, f=f: (*f(b, i), 0)) for f in blk]
        in_specs += [pl.BlockSpec((None, LANES, Q_BLOCK), lambda b, i, f=f: (f(b, i)[0], 0, f(b, i)[1])) for f in blk]
        args += [k] * len(blk) + [vt] * len(blk)
    in_specs += [pl.BlockSpec((None, lc, 2 * LANES), lambda b, i: (b, 0, 0)),
                 pl.BlockSpec((None, LANES, lc), lambda b, i: (b, 0, 0)),
                 pl.BlockSpec(memory_space=pltpu.SMEM)]
    args += [kc, vct, sink]
    return pl.pallas_call(
        functools.partial(_win_attn_kernel, has_window=has_window, lc=lc, tq=tq),
        out_shape=jax.ShapeDtypeStruct((bsz, tq_total, width), BF16),
        grid=(bsz, nq),
        in_specs=in_specs,
        out_specs=qspec,
        compiler_params=_params(("parallel", "parallel")),
        name="window_attention",
    )(*args)


_REC_COLS = (("mq", 512, BF16), ("mk", 512, BF16), ("mv", 512, BF16), ("mo", 512, BF16),
             ("rq", 256, BF16), ("rk", 256, BF16), ("rv", 512, BF16), ("rg", 512, BF16),
             ("gi", 128, F32), ("gf", 128, F32))
_N_DIR_HEADS = 2 * MLSTM_HEADS


def _rec_proj_kernel(x_ref, mod_ref, w_ref, gate_b_ref, *outs):
    x = x_ref[...]
    shift, scale = mod_ref[0:1, :], mod_ref[1:2, :]
    h = (x * (1.0 + scale) + shift).astype(BF16)
    lo = 0
    for (name, width, dtype), o_ref in zip(_REC_COLS, outs):
        p = _dot(h, w_ref[:, lo:lo + width])
        lo += width
        if name == "mk":
            p = p * (MLSTM_HEAD_DIM ** -0.5)
        elif name == "rk":
            p = p * (RET_QK_DIM ** -0.5)
        elif name == "gi":
            p = p + gate_b_ref[0:1, :]
        elif name == "gf":
            p = _log_sigmoid(p + gate_b_ref[1:2, :])
        o_ref[...] = p.astype(dtype)


def _rec_proj(x2, mod4, tokens_per_row, w, gate_b):
    n = x2.shape[0]
    tm = _token_tile(n, tokens_per_row)
    tile = lambda width: pl.BlockSpec((tm, width), lambda i: (i, 0))
    return pl.pallas_call(
        _rec_proj_kernel,
        out_shape=[jax.ShapeDtypeStruct((n, width), dtype) for _, width, dtype in _REC_COLS],
        grid=(n // tm,),
        in_specs=[tile(D_MODEL), _mod_spec(1, tokens_per_row // tm), _resident(w.shape), _resident((2, LANES))],
        out_specs=[tile(width) for _, width, _ in _REC_COLS],
        compiler_params=_params(("parallel",)),
        name="rec_in_proj",
    )(x2, mod4, w, gate_b)


def _scan_specs(ncc, ncl, width, rows, nb):
    ctx_f = pl.BlockSpec((nb, rows, width), lambda b, s: (b, jnp.minimum(s, ncc - 1), 0))
    ctx_b = pl.BlockSpec((nb, rows, width), lambda b, s: (b, jnp.maximum(ncc - 1 - s, 0), 0))
    lat_f = pl.BlockSpec((nb, rows, width), lambda b, s: (b, jnp.maximum(s - ncc, 0), 0))
    lat_b = pl.BlockSpec((nb, rows, width), lambda b, s: (b, jnp.minimum(ncl - 1 - (s - ncc), ncl - 1), 0))
    return ctx_f, ctx_b, lat_f, lat_b


def _tri_masks():
    r = lax.broadcasted_iota(jnp.int32, (CHUNK, CHUNK), 0)
    c = lax.broadcasted_iota(jnp.int32, (CHUNK, CHUNK), 1)
    return r, c


def _head_cols(ref, h):
    return ref[:, h * LANES:(h + 1) * LANES]


def _mlstm_step(srcs, masks, tri, c_ref, m_ref, with_output):
    units = [(d, h) for d in range(len(srcs)) for h in range(MLSTM_HEADS)]
    dv = MLSTM_HEAD_DIM
    ones = jnp.ones((CHUNK, LANES), BF16)
    v_ext = {(d, h): jnp.concatenate([_head_cols(srcs[d][2], h), ones], axis=1) for d, h in units}
    gate = []
    for d in range(len(srcs)):
        gi, gf = srcs[d][3][...], srcs[d][4][...]
        cum = jnp.dot(tri[d % 2], gf, preferred_element_type=F32, precision=lax.Precision.HIGHEST)
        b_end = cum[CHUNK - 1:CHUNK, :] if d % 2 == 0 else cum[0:1, :]
        m_prev = m_ref[d:d + 1, :]
        w_end = b_end - cum + gi
        m_new = jnp.maximum(b_end + m_prev, jnp.max(w_end, axis=0, keepdims=True))
        decay = jnp.exp(b_end + m_prev - m_new)
        w = jnp.exp(w_end - m_new)
        m_ref[d:d + 1, :] = m_new
        log_inter = cum + m_prev if with_output else None
        key_term = (gi - cum).T if with_output else None
        gate.append((cum, decay, w, log_inter, key_term))
    if with_output:
        qk, qcn, log_d, li, rmax, m_t, sm, w_inter, pv = {}, {}, {}, {}, {}, {}, {}, {}, {}
        for d, h in units:
            q = _head_cols(srcs[d][0], h)
            qk[d, h] = _dot_nt(q, _head_cols(srcs[d][1], h))
            qcn[d, h] = _dot(q, c_ref[d * MLSTM_HEADS + h].astype(BF16))
        for d, h in units:
            lane = (d % 2) * MLSTM_HEADS + h
            cum, _, _, log_inter, key_term = gate[d]
            log_d[d, h] = jnp.where(masks[d % 2], cum[:, lane:lane + 1] + key_term[lane:lane + 1, :], NEG_BIG)
            li[d, h] = log_inter[:, lane:lane + 1]
        for u in units:
            rmax[u] = jnp.max(log_d[u], axis=1, keepdims=True)
        for u in units:
            m_t[u] = jnp.maximum(li[u], rmax[u])
            sm[u] = (qk[u] * jnp.exp(log_d[u] - m_t[u])).astype(BF16)
            w_inter[u] = jnp.exp(li[u] - m_t[u])
        for u in units:
            pv[u] = _dot(sm[u], v_ext[u])
        for d, h in units:
            u = (d, h)
            both = pv[u] + w_inter[u] * qcn[u]
            den = jnp.maximum(jnp.abs(both[:, dv:]), jnp.exp(-m_t[u]))
            srcs[d][5][:, h * LANES:(h + 1) * LANES] = (both[:, :dv] * (1.0 / den)).astype(BF16)
    for d, h in units:
        row, lane = d * MLSTM_HEADS + h, (d % 2) * MLSTM_HEADS + h
        _, decay, w, _, _ = gate[d]
        kw = (_head_cols(srcs[d][1], h).astype(F32) * w[:, lane:lane + 1]).astype(BF16)
        c_ref[row] = decay[:, lane:lane + 1] * c_ref[row] + _dot_tn(kw, v_ext[d, h])


def _ret_tables(logit_ref, intra_ref, to_end_ref, from_start_ref):
    log_gamma = _log_sigmoid(logit_ref[...])
    r, c = _tri_masks()
    rf, cf = r.astype(F32), c.astype(F32)
    for d in range(2):
        for h in range(RET_HEADS):
            row = d * RET_HEADS + h
            lg = log_gamma[d:d + 1, h:h + 1]
            rel = (rf - cf) if d == 0 else (cf - rf)
            intra_ref[row] = jnp.where(rel >= 0.0, jnp.exp(lg * jnp.maximum(rel, 0.0)), 0.0)
            to_end_ref[row] = jnp.exp(lg * ((CHUNK - 1.0 - rf) if d == 0 else rf))
            from_start_ref[row] = jnp.exp(lg * ((rf + 1.0) if d == 0 else (CHUNK - rf)))


def _ret_step(srcs, s_ref, intra_ref, to_end_ref, from_start_ref, with_output):
    units = [(d, h) for d in range(len(srcs)) for h in range(RET_HEADS)]
    lane = lax.broadcasted_iota(jnp.int32, (CHUNK, LANES), 1)

    def k_head(d, h):
        pair = _head_cols(srcs[d][1], h // 2)
        keep = (lane < RET_QK_DIM) if h % 2 == 0 else (lane >= RET_QK_DIM)
        return jnp.where(keep, pair, jnp.zeros_like(pair))

    ks = {u: k_head(*u) for u in units}
    qk, qs = {}, {}
    if with_output:
        for d, h in units:
            q = _head_cols(srcs[d][0], h // 2)
            qk[d, h] = _dot_nt(q, ks[d, h])
            qs[d, h] = _dot(q, s_ref[d * RET_HEADS + h].astype(BF16))
        for d, h in units:
            tab = (d % 2) * RET_HEADS + h
            sc = (qk[d, h] * intra_ref[tab]).astype(BF16)
            out = _dot(sc, _head_cols(srcs[d][2], h)) + from_start_ref[tab] * qs[d, h]
            srcs[d][3][:, h * LANES:(h + 1) * LANES] = out.astype(BF16)
    for d, h in units:
        row, tab = d * RET_HEADS + h, (d % 2) * RET_HEADS + h
        kd = (ks[d, h].astype(F32) * to_end_ref[tab]).astype(BF16)
        chunk_decay = from_start_ref[tab, CHUNK - 1:CHUNK, :] if d % 2 == 0 else from_start_ref[tab, 0:1, :]
        s_ref[row] = chunk_decay * s_ref[row] + _dot_tn(kd, _head_cols(srcs[d][2], h))


def _rec_scan_kernel(mkc_f, mvc_f, gic_f, gfc_f, mkc_b, mvc_b, gic_b, gfc_b,
                     mq_f, mk_f, mv_f, gi_f, gf_f, mq_b, mk_b, mv_b, gi_b, gf_b,
                     rkc_f, rvc_f, rkc_b, rvc_b, rq_f, rk_f, rv_f, rq_b, rk_b, rv_b, logit_ref,
                     mo_f, mo_b, ro_f, ro_b,
                     c_ref, m_ref, s_ref, intra_ref, to_end_ref, from_start_ref, *, ncc):
    s = pl.program_id(1)

    @pl.when(s == 0)
    def _():
        c_ref[...] = jnp.zeros_like(c_ref)
        m_ref[...] = jnp.zeros_like(m_ref)
        s_ref[...] = jnp.zeros_like(s_ref)
        _ret_tables(logit_ref, intra_ref, to_end_ref, from_start_ref)

    r, c = _tri_masks()
    masks = (c <= r, c >= r)
    tri = (jnp.where(masks[0], 1.0, 0.0), jnp.where(masks[1], 1.0, 0.0))

    def run(m_f, m_b, r_f, r_b, with_output):
        n_batch = mkc_f.shape[0]
        for sub in range(SCAN_CHUNKS):
            lo = (sub * CHUNK, (SCAN_CHUNKS - 1 - sub) * CHUNK)
            view = lambda refs, bb, d: tuple(None if ref is None else _Rows(ref, bb, lo[d]) for ref in refs)
            m_srcs = [view(refs, bb, d) for bb in range(n_batch) for d, refs in enumerate((m_f, m_b))]
            r_srcs = [view(refs, bb, d) for bb in range(n_batch) for d, refs in enumerate((r_f, r_b))]
            _mlstm_step(m_srcs, masks, tri, c_ref, m_ref, with_output)
            _ret_step(r_srcs, s_ref, intra_ref, to_end_ref, from_start_ref, with_output)

    @pl.when(s < ncc)
    def _():
        run((None, mkc_f, mvc_f, gic_f, gfc_f, None), (None, mkc_b, mvc_b, gic_b, gfc_b, None),
            (None, rkc_f, rvc_f, None), (None, rkc_b, rvc_b, None), False)

    @pl.when(s >= ncc)
    def _():
        run((mq_f, mk_f, mv_f, gi_f, gf_f, mo_f), (mq_b, mk_b, mv_b, gi_b, gf_b, mo_b),
            (rq_f, rk_f, rv_f, ro_f), (rq_b, rk_b, rv_b, ro_b), True)


class _Rows:
    def __init__(self, ref, bb, lo):
        self.ref, self.bb, self.rows = ref, bb, slice(lo, lo + CHUNK)

    def __getitem__(self, idx):
        return self.ref[self.bb, self.rows, :] if idx is Ellipsis else self.ref[self.bb, self.rows, idx[1]]

    def __setitem__(self, idx, value):
        self.ref[self.bb, self.rows, idx[1]] = value


def _rec_scan(mq, mk, mv, gi, gf, rq, rk, rv, mkc, mvc, gic, gfc, rkc, rvc, logit):
    bsz, t, width = mq.shape
    rows = CHUNK * SCAN_CHUNKS
    assert t % rows == 0 and mkc.shape[1] % rows == 0
    ncl, ncc = t // rows, mkc.shape[1] // rows
    nb = SCAN_BATCH if bsz % SCAN_BATCH == 0 else 1
    cf, cb, lf, lb = _scan_specs(ncc, ncl, width, rows, nb)
    gcf, gcb, glf, glb = _scan_specs(ncc, ncl, LANES, rows, nb)
    kcf, kcb, klf, klb = _scan_specs(ncc, ncl, rq.shape[2], rows, nb)
    return pl.pallas_call(
        functools.partial(_rec_scan_kernel, ncc=ncc),
        out_shape=[jax.ShapeDtypeStruct((bsz, t, width), BF16)] * 4,
        grid=(bsz // nb, ncc + ncl),
        in_specs=[cf, cf, gcf, gcf, cb, cb, gcb, gcb, lf, lf, lf, glf, glf, lb, lb, lb, glb, glb,
                  kcf, cf, kcb, cb, klf, klf, lf, klb, klb, lb, pl.BlockSpec((8, LANES), lambda b, s: (0, 0))],
        out_specs=[lf, lb, lf, lb],
        scratch_shapes=[pltpu.VMEM((nb * _N_DIR_HEADS, MLSTM_HEAD_DIM, 2 * MLSTM_HEAD_DIM), F32),
                        pltpu.VMEM((8, LANES), F32),
                        pltpu.VMEM((nb * 2 * RET_HEADS, LANES, RET_V_DIM), F32)]
        + [pltpu.VMEM((2 * RET_HEADS, CHUNK, LANES), F32)] * 3,
        compiler_params=_params(("parallel", "arbitrary")),
        name="rec_scan",
    )(mkc, mvc, gic, gfc, mkc, mvc, gic, gfc, mq, mk, mv, gi, gf, mq, mk, mv, gi, gf,
      rkc, rvc, rkc, rvc, rq, rk, rv, rq, rk, rv, logit)


def _rec_out_kernel(x_ref, mod_ref, mf_ref, mb_ref, rf_ref, rb_ref, o_ref_in, rg_ref, mg_ref, rng_ref, rnb_ref,
                    w_ref, g_ref, b_ref, out_ref):
    hm = mf_ref[...].astype(F32) + mb_ref[...].astype(F32)
    hr = rf_ref[...].astype(F32) + rb_ref[...].astype(F32)
    o_gate = jax.nn.sigmoid(o_ref_in[...].astype(F32))
    rg = rg_ref[...].astype(F32)
    r_gate = rg * jax.nn.sigmoid(rg)
    parts_m, parts_r = [], []
    for h in range(MLSTM_HEADS):
        cols = slice(h * LANES, (h + 1) * LANES)
        parts_m.append(_layer_norm(hm[:, cols], mg_ref[:, cols], None) * o_gate[:, cols])
        parts_r.append(_layer_norm(hr[:, cols], rng_ref[:, cols], rnb_ref[:, cols]) * r_gate[:, cols])
    width = MLSTM_HEADS * MLSTM_HEAD_DIM
    y = (_dot(jnp.concatenate(parts_m, axis=1).astype(BF16), w_ref[0:width, :])
         + _dot(jnp.concatenate(parts_r, axis=1).astype(BF16), w_ref[width:, :]))
    z = ALPHA * x_ref[...] + mod_ref[2:3, :] * y
    out_ref[...] = _layer_norm(z, g_ref[...], b_ref[...])


def _rec_out(x2, mod4, tokens_per_row, mf, mb, rf, rb, o, rg, mg, rng, rnb, w, g, b):
    n = x2.shape[0]
    tm = _token_tile(n, tokens_per_row)
    tile = lambda width: pl.BlockSpec((tm, width), lambda i: (i, 0))
    half = tile(512)
    return pl.pallas_call(
        _rec_out_kernel,
        out_shape=jax.ShapeDtypeStruct((n, D_MODEL), F32),
        grid=(n // tm,),
        in_specs=[tile(D_MODEL), _mod_spec(1, tokens_per_row // tm), half, half, half, half, half, half,
                  _resident((1, 512)), _resident((1, 512)), _resident((1, 512)),
                  _resident(w.shape), _resident((1, D_MODEL)), _resident((1, D_MODEL))],
        out_specs=tile(D_MODEL),
        compiler_params=_params(("parallel",)),
        name="rec_out_proj",
    )(x2, mod4, mf, mb, rf, rb, o, rg, mg, rng, rnb, w, g, b)


def _attn_weights(w_in, w_out):
    aq, ak, av, bq, bk, bv = jnp.split(w_in, [512, 1024, 1536, 2048, 2176], axis=1)
    bk_swapped = jnp.concatenate([bk[:, WIN_HEAD_DIM:], bk[:, :WIN_HEAD_DIM]], axis=1)
    w = jnp.concatenate([aq, ak, av, bq, bk, bk_swapped, bv], axis=1)
    wa, wb = w_out[:DIFF_HEADS * DIFF_V_DIM], w_out[DIFF_HEADS * DIFF_V_DIM:]
    return w.astype(BF16), wa.astype(BF16), wb.astype(BF16)


def _rec_weights(w_in):
    mq, mk, mv, mo, mg, rq, rk, rv, rg = jnp.split(w_in, [512, 1024, 1536, 2048, 2064, 2320, 2576, 3088], axis=1)
    gi, gf = _split_gates(mg)
    w = jnp.concatenate([mq, mk, mv, mo, rq, rk, rv, rg, gi, gf], axis=1)
    return w.astype(BF16)


def _split_gates(g):
    g = g.reshape(*g.shape[:-1], 2, 2, MLSTM_HEADS)
    pad = [(0, 0)] * (g.ndim - 3) + [(0, LANES - _N_DIR_HEADS)]
    return tuple(jnp.pad(g[..., io, :].reshape(*g.shape[:-3], _N_DIR_HEADS), pad) for io in range(2))


def _rope_tables(t):
    rows = t // GRID_W
    row = jnp.repeat(jnp.arange(rows), GRID_W)
    col = jnp.tile(jnp.arange(GRID_W), rows)
    inv = ROPE_BASE ** (-jnp.arange(ROPE_FREQS, dtype=F32) / ROPE_FREQS)
    ang_r, ang_c = row[:, None] * inv, col[:, None] * inv
    cos = jnp.concatenate([jnp.cos(ang_r)] * 2 + [jnp.cos(ang_c)] * 2, axis=1)
    sin = jnp.concatenate([-jnp.sin(ang_r), jnp.sin(ang_r), -jnp.sin(ang_c), jnp.sin(ang_c)], axis=1)
    return jnp.tile(cos, (1, 2)), jnp.tile(sin, (1, 2))


def kernel(x, c, ctx, c_ctx, ada_w, ada_b, ln_g, ln_b, ffn_w_in, ffn_w_out, attn_w_in, attn_w_out,
           diff_lambda, diff_norm_g, sink_logits, rec_w_in, rec_w_out, mlstm_gate_b, mlstm_norm_g,
           ret_decay_logit, ret_norm_g, ret_norm_b):
    bsz, t, d = x.shape
    lc = ctx.shape[1]
    assert d == D_MODEL and t % ATTN_TK == 0 and lc % CHUNK == 0
    out_dtype = x.dtype
    x2 = x.reshape(bsz * t, d).astype(F32)
    c2 = ctx.reshape(bsz * lc, d).astype(F32)
    mod_rows = 16
    cc = jnp.zeros((mod_rows, d), F32).at[:bsz].set(c).at[bsz].set(c_ctx)
    row2 = lambda v: v.reshape(1, -1).astype(F32)
    ada_b3 = ada_b.reshape(DEPTH, 1, -1)
    ffn_w_in16, ffn_w_out16 = ffn_w_in.astype(BF16), ffn_w_out.astype(BF16)

    for l in range(DEPTH):
        last = l == DEPTH - 1
        i = l // 2
        mod = _modulation(cc, ada_w, ada_b3, l).reshape(mod_rows, 3, 3, d)
        mod_x, mod_c = mod[:bsz], mod[bsz:bsz + 1]
        ffn = lambda z, m, sub, tpr, j: _ffn(z, m, sub, tpr, ffn_w_in16, ffn_w_out16, l, j,
                                             row2(ln_g[l, sub]), row2(ln_b[l, sub]))
        x2 = ffn(x2, mod_x, 0, t, 0)
        c2 = ffn(c2, mod_c, 0, bsz * lc, 0)
        g1, b1 = row2(ln_g[l, 1]), row2(ln_b[l, 1])
        if l % 2 == 0:
            lambda_init = 0.8 - 0.6 * math.exp(-0.3 * l)
            w, wa, wb = _attn_weights(attn_w_in[i], attn_w_out[i])
            aq, ak, av, bq, bk, bv = [a.reshape(bsz, t, -1) for a in _attn_proj(x2, mod_x, t, w, _rope_tables(t))]
            aqc, akc, avc, bqc, bkc, bvc = [a.reshape(bsz, lc, -1) for a in _attn_proj(c2, mod_c, bsz * lc, w, None)]
            avt, avct = jnp.swapaxes(av, 1, 2), jnp.swapaxes(avc, 1, 2)
            bvt, bvct = jnp.swapaxes(bv, 1, 2), jnp.swapaxes(bvc, 1, 2)
            lam_vec, sub_g = diff_lambda[i].astype(F32), row2(diff_norm_g[i])
            sink = sink_logits[i].astype(F32)
            a_x = _diff_attn(aq, ak, avt, akc, avct, lam_vec, sub_g, lambda_init)
            b_x = _win_attn(bq, bk, bvt, bkc, bvct, sink, True)
            flat = lambda a: a.reshape(-1, a.shape[-1])
            if not last:
                a_c = _diff_attn(aqc, None, None, akc, avct, lam_vec, sub_g, lambda_init)
                b_c = _win_attn(bqc, None, None, bkc, bvct, sink, False)
                c2_mix = _attn_out(c2, mod_c, bsz * lc, flat(a_c), flat(b_c), wa, wb, g1, b1)
            x2 = _attn_out(x2, mod_x, t, flat(a_x), flat(b_x), wa, wb, g1, b1)
        else:
            w = _rec_weights(rec_w_in[i])
            gate_b = jnp.stack(_split_gates(mlstm_gate_b[i].reshape(-1).astype(F32)))
            px = [a.reshape(bsz, t, -1) for a in _rec_proj(x2, mod_x, t, w, gate_b)]
            pc = [a.reshape(bsz, lc, -1) for a in _rec_proj(c2, mod_c, bsz * lc, w, gate_b)]
            mq, mk, mv, mo, rq, rk, rv, rg, gi, gf = px
            _, mkc, mvc, _, _, rkc, rvc, _, gic, gfc = pc
            assert last
            logit = jnp.zeros((8, LANES), F32).at[:2, :RET_HEADS].set(ret_decay_logit[i].astype(F32))
            m_f, m_b, r_f, r_b = _rec_scan(mq, mk, mv, gi, gf, rq, rk, rv, mkc, mvc, gic, gfc, rkc, rvc, logit)
            flat = lambda a: a.reshape(-1, a.shape[-1])
            x2 = _rec_out(x2, mod_x, t, flat(m_f), flat(m_b), flat(r_f), flat(r_b), flat(mo), flat(rg),
                          row2(mlstm_norm_g[i]), row2(ret_norm_g[i]), row2(ret_norm_b[i]),
                          rec_w_out[i].astype(BF16), g1, b1)
        x2 = ffn(x2, mod_x, 2, t, 1)
        if not last:
            c2 = ffn(c2_mix, mod_c, 2, bsz * lc, 1)
    return x2.reshape(bsz, t, d).astype(out_dtype)
```

```python
import functools
import math

import jax
import jax.numpy as jnp
from jax import lax
from jax.experimental import pallas as pl
from jax.experimental.pallas import tpu as pltpu

F32 = jnp.float32
BF16 = jnp.bfloat16

D_MODEL = 1024
DEPTH = 2
GRID_W = 64
D_FF = 2816
DIFF_HEADS = 4
DIFF_HEAD_DIM = 64
DIFF_V_DIM = 128
WIN_HEADS = 8
WIN_KV_HEADS = 2
WIN_GROUP = WIN_HEADS // WIN_KV_HEADS
WIN_HEAD_DIM = 64
WINDOW = 128
Q_BLOCK = 128
ROPE_BASE = 10000.0
ROPE_FREQS = 16
MLSTM_HEADS = 4
MLSTM_HEAD_DIM = 128
RET_HEADS = 4
RET_QK_DIM = 64
RET_V_DIM = 128
CHUNK = 128
ALPHA = (2.0 * DEPTH) ** 0.25
EPS = 1e-5
LOG2_E = 1.4426950408889634
NEG_BIG = -1e30

LANES = 128
BF16_SUBLANES = 16
V7X_VMEM_LIMIT_BYTES = 56 * 1024 * 1024

TOKEN_TILE = 1024
FFN_TILE = 256
FF_CHUNK = 256
FFN_SUB_TILES = 4
ATTN_TQ = 512
ATTN_TK = 1024
ATTN_PIECE = 256
ATTN_HEADS_PER_STEP = 2
WIN_TQ = 256
SCAN_CHUNKS = 1
SCAN_BATCH = 4


def _params(semantics):
    return pltpu.CompilerParams(dimension_semantics=semantics, vmem_limit_bytes=V7X_VMEM_LIMIT_BYTES)


def _resident(shape):
    return pl.BlockSpec(shape, lambda *_: (0,) * len(shape), pipeline_mode=pl.Buffered(1))


def _layer_norm(y, g, b):
    mu = jnp.mean(y, axis=-1, keepdims=True)
    yc = y - mu
    var = jnp.mean(yc * yc, axis=-1, keepdims=True)
    out = yc * lax.rsqrt(var + EPS) * g
    return out if b is None else out + b


def _log_sigmoid(x):
    return jnp.minimum(x, 0.0) - jnp.log1p(jnp.exp(-jnp.abs(x)))


def _dot(a, b):
    return jnp.dot(a, b, preferred_element_type=F32)


def _dot_nt(a, b):
    return lax.dot_general(a, b, (((1,), (1,)), ((), ())), preferred_element_type=F32)


def _dot_tn(a, b):
    return lax.dot_general(a, b, (((0,), (0,)), ((), ())), preferred_element_type=F32)


def _split_bf16(a):
    hi = a.astype(BF16)
    return hi, (a - hi.astype(F32)).astype(BF16)


def _mod_kernel(c_ref, w_ref, b_ref, o_ref):
    c = c_ref[...]
    h_hi, h_lo = _split_bf16(c * jax.nn.sigmoid(c))
    w_hi, w_lo = _split_bf16(w_ref[...])
    o_ref[...] = _dot(h_hi, w_hi) + (_dot(h_hi, w_lo) + _dot(h_lo, w_hi) + _dot(h_lo, w_lo)) + b_ref[...]


def _modulation(cc, w, b, layer):
    rows, d = cc.shape
    n = w.shape[2]
    tn = 1024
    return pl.pallas_call(
        _mod_kernel,
        out_shape=jax.ShapeDtypeStruct((rows, n), F32),
        grid=(n // tn,),
        in_specs=[pl.BlockSpec((rows, d), lambda j: (0, 0)),
                  pl.BlockSpec((None, d, tn), lambda j: (layer, 0, j)),
                  pl.BlockSpec((None, 1, tn), lambda j: (layer, 0, j))],
        out_specs=pl.BlockSpec((rows, tn), lambda j: (0, j)),
        compiler_params=_params(("parallel",)),
        name="modulation",
    )(cc, w, b)


def _token_tile(n_tokens, tokens_per_row, tile=None):
    tm = min(tile or TOKEN_TILE, tokens_per_row)
    assert tokens_per_row % tm == 0 and n_tokens % tm == 0
    return tm


def _mod_spec(sub, tiles_per_row):
    return pl.BlockSpec((None, None, 3, D_MODEL), lambda i: (i // tiles_per_row, sub, 0, 0))


def _ffn_kernel(x_ref, mod_ref, w_in_ref, w_out_ref, g_ref, b_ref, o_ref, act_ref, *, n_sub):
    shift, scale, gate = mod_ref[0:1, :], mod_ref[1:2, :], mod_ref[2:3, :]
    rows = x_ref.shape[0] // n_sub

    def finish(s):
        x = x_ref[s * rows:(s + 1) * rows, :]
        y = ALPHA * x + 0.5 * gate * _dot(act_ref[s], w_out_ref[...])
        o_ref[s * rows:(s + 1) * rows, :] = _layer_norm(y, g_ref[...], b_ref[...])

    for s in range(n_sub):
        h = (x_ref[s * rows:(s + 1) * rows, :] * (1.0 + scale) + shift).astype(BF16)
        for c in range(D_FF // FF_CHUNK):
            lo = c * FF_CHUNK
            gt = _dot(h, w_in_ref[:, lo:lo + FF_CHUNK])
            up = _dot(h, w_in_ref[:, D_FF + lo:D_FF + lo + FF_CHUNK])
            act_ref[s, :, lo:lo + FF_CHUNK] = (gt * jax.nn.sigmoid(gt) * up).astype(BF16)
            if c == 0 and s > 0:
                finish(s - 1)
    finish(n_sub - 1)


def _ffn(x2, mod4, sub, tokens_per_row, w_in, w_out, layer, which, g, b):
    n = x2.shape[0]
    tm = _token_tile(n, tokens_per_row, FFN_TILE)
    n_sub = FFN_SUB_TILES if (tokens_per_row % (FFN_SUB_TILES * tm) == 0) else 1
    tb = tm * n_sub
    tile = pl.BlockSpec((tb, D_MODEL), lambda i: (i, 0))
    weight = lambda shape: pl.BlockSpec((None, None) + shape, lambda i: (layer, which, 0, 0),
                                        pipeline_mode=pl.Buffered(1))
    return pl.pallas_call(
        functools.partial(_ffn_kernel, n_sub=n_sub),
        out_shape=jax.ShapeDtypeStruct((n, D_MODEL), F32),
        grid=(n // tb,),
        in_specs=[tile, _mod_spec(sub, tokens_per_row // tb),
                  weight((D_MODEL, 2 * D_FF)), weight((D_FF, D_MODEL)),
                  _resident((1, D_MODEL)), _resident((1, D_MODEL))],
        out_specs=tile,
        scratch_shapes=[pltpu.VMEM((n_sub, tm, D_FF), BF16)],
        compiler_params=_params(("parallel",)),
        name="ffn_sublayer",
    )(x2, mod4, w_in, w_out, g, b)


def _rope(p, cos, sin):
    lane = lax.broadcasted_iota(jnp.int32, (p.shape[0], LANES), 1)
    first_half = (lane & ROPE_FREQS) == 0
    outs = []
    for j in range(p.shape[1] // LANES):
        xs = p[:, j * LANES:(j + 1) * LANES]
        partner = jnp.where(first_half, pltpu.roll(xs, LANES - ROPE_FREQS, 1), pltpu.roll(xs, ROPE_FREQS, 1))
        outs.append(xs * cos + partner * sin)
    return jnp.concatenate(outs, axis=1)


_ATTN_COLS = (("aq", 512), ("ak", 512), ("av", 512), ("bq", 512), ("bk", 256), ("bv", 128))
_ATTN_GROUPS = (("aq",), ("ak",), ("av",), ("bq",), ("bk", "bv"))


def _attn_proj_kernel(*refs, rope):
    if rope:
        x_ref, mod_ref, w_ref, cos_ref, sin_ref = refs[:5]
        outs = refs[5:]
        cos, sin = cos_ref[...], sin_ref[...]
    else:
        x_ref, mod_ref, w_ref = refs[:3]
        outs = refs[3:]
    x = x_ref[...]
    shift, scale = mod_ref[0:1, :], mod_ref[1:2, :]
    h = (x * (1.0 + scale) + shift).astype(BF16)
    widths = dict(_ATTN_COLS)
    out_refs = dict(zip(widths, outs))
    lo = 0
    for group in _ATTN_GROUPS:
        total = sum(widths[name] for name in group)
        pg = _dot(h, w_ref[:, lo:lo + total])
        lo += total
        off = 0
        for name in group:
            p = pg[:, off:off + widths[name]]
            off += widths[name]
            if rope and name in ("aq", "ak", "bq", "bk"):
                p = _rope(p, cos, sin)
            if name in ("aq", "bq"):
                p = p * (DIFF_HEAD_DIM ** -0.5 * LOG2_E)
            out_refs[name][...] = p.astype(BF16)


def _attn_proj(x2, mod4, tokens_per_row, w, tables):
    n = x2.shape[0]
    tm = _token_tile(n, tokens_per_row)
    rope = tables is not None
    tile = lambda width: pl.BlockSpec((tm, width), lambda i: (i, 0))
    in_specs = [tile(D_MODEL), _mod_spec(1, tokens_per_row // tm), _resident(w.shape)]
    args = [x2, mod4, w]
    if rope:
        tpr = tokens_per_row // tm
        tab = pl.BlockSpec((tm, LANES), lambda i: (i % tpr, 0))
        in_specs += [tab, tab]
        args += list(tables)
    return pl.pallas_call(
        functools.partial(_attn_proj_kernel, rope=rope),
        out_shape=[jax.ShapeDtypeStruct((n, width), BF16) for _, width in _ATTN_COLS],
        grid=(n // tm,),
        in_specs=in_specs,
        out_specs=[tile(width) for _, width in _ATTN_COLS],
        compiler_params=_params(("parallel",)),
        name="attn_in_proj",
    )(*args)


def _attn_out_kernel(x_ref, mod_ref, a_ref, b_ref, wa_ref, wb_ref, g_ref, bb_ref, o_ref):
    y = _dot(a_ref[...], wa_ref[...]) + _dot(b_ref[...], wb_ref[...])
    z = ALPHA * x_ref[...] + mod_ref[2:3, :] * y
    o_ref[...] = _layer_norm(z, g_ref[...], bb_ref[...])


def _attn_out(x2, mod4, tokens_per_row, a, b, wa, wb, g, bb):
    n = x2.shape[0]
    tm = _token_tile(n, tokens_per_row)
    tile = lambda width: pl.BlockSpec((tm, width), lambda i: (i, 0))
    return pl.pallas_call(
        _attn_out_kernel,
        out_shape=jax.ShapeDtypeStruct((n, D_MODEL), F32),
        grid=(n // tm,),
        in_specs=[tile(D_MODEL), _mod_spec(1, tokens_per_row // tm), tile(a.shape[1]), tile(b.shape[1]),
                  _resident(wa.shape), _resident(wb.shape), _resident((1, D_MODEL)), _resident((1, D_MODEL))],
        out_specs=tile(D_MODEL),
        compiler_params=_params(("parallel",)),
        name="attn_out_proj",
    )(x2, mod4, a, b, wa, wb, g, bb)


def _diff_attn_kernel(*refs, n_lat, lc, lambda_init):
    if n_lat:
        q_ref, k_ref, vt_ref, kc_ref, vct_ref, lam_ref, g_ref, o_ref = refs
    else:
        q_ref, kc_ref, vct_ref, lam_ref, g_ref, o_ref = refs
    t = lam_ref[...]
    lam = (jnp.exp(jnp.sum(t[0:1, :] * t[1:2, :], axis=1, keepdims=True))
           - jnp.exp(jnp.sum(t[2:3, :] * t[3:4, :], axis=1, keepdims=True)) + lambda_init)
    n_heads = q_ref.shape[1] // LANES
    ctx_pieces = [(kc_ref, vct_ref, j * ATTN_PIECE) for j in range(lc // ATTN_PIECE)]
    chunks = [[(k_ref, vt_ref, c * ATTN_TK + j * ATTN_PIECE) for j in range(ATTN_TK // ATTN_PIECE)]
              for c in range(n_lat)] + [ctx_pieces]
    stages = [(hd, idx, idx == len(chunks) - 1, chunk) for hd in range(n_heads) for idx, chunk in enumerate(chunks)]

    def n_pieces(stage):
        return len(stage[3])

    def scores(stage, piece, sub):
        hd, _, _, pieces = stage
        kr, _, lo = pieces[piece]
        q = q_ref[:, hd * LANES:(hd + 1) * LANES]
        kblk = kr[lo:lo + ATTN_PIECE, hd * LANES:(hd + 1) * LANES]
        lane = lax.broadcasted_iota(jnp.int32, kblk.shape, 1)
        keep = (lane < DIFF_HEAD_DIM) if sub == 0 else (lane >= DIFF_HEAD_DIM)
        return _dot_nt(jnp.where(keep, kblk, jnp.zeros_like(kblk)), q)

    def finish(hd, acc):
        num = [a[:DIFF_V_DIM, :] * (1.0 / a[DIFF_V_DIM:DIFF_V_DIM + 1, :]) for a in acc]
        o = (num[0] - lam * num[1]).T
        o = o * lax.rsqrt(jnp.mean(o * o, axis=-1, keepdims=True) + EPS) * g_ref[...]
        o_ref[:, hd * LANES:(hd + 1) * LANES] = (o * (1.0 - lambda_init)).astype(BF16)

    m, acc = [None, None], [None, None]
    s_next = [[scores(stages[0], j, sub) for j in range(n_pieces(stages[0]))] for sub in range(2)]
    for si, stage in enumerate(stages):
        hd, idx, is_last, pieces = stage
        values = jnp.concatenate([vr[hd * LANES:(hd + 1) * LANES, lo:lo + ATTN_PIECE] for _, vr, lo in pieces], axis=1)
        vt = jnp.concatenate([values, jnp.ones((BF16_SUBLANES, values.shape[1]), BF16)], axis=0)
        nxt = stages[si + 1] if si + 1 < len(stages) else None
        for sub in range(2):
            s_cur, s_new = s_next[sub], []
            smax = functools.reduce(jnp.maximum, [jnp.max(s, axis=0, keepdims=True) for s in s_cur])
            m_new = smax if idx == 0 else jnp.maximum(m[sub], smax)
            ps = []
            for j, s in enumerate(s_cur):
                if nxt is not None and j < n_pieces(nxt):
                    s_new.append(scores(nxt, j, sub))
                ps.append(jnp.exp2(s - m_new).astype(BF16))
            if nxt is not None:
                s_new += [scores(nxt, j, sub) for j in range(len(s_new), n_pieces(nxt))]
            part = _dot(vt, jnp.concatenate(ps, axis=0))
            acc[sub] = part if idx == 0 else jnp.exp2(m[sub] - m_new) * acc[sub] + part
            m[sub], s_next[sub] = m_new, s_new
        if is_last:
            finish(hd, acc)


def _diff_attn(q, k, vt, kc, vct, lam_vec, g, lambda_init):
    bsz, tq_total, _ = q.shape
    lc = kc.shape[1]
    tq = min(ATTN_TQ, tq_total)
    n_lat = 0 if k is None else k.shape[1] // ATTN_TK
    assert tq_total % tq == 0 and (k is None or k.shape[1] % ATTN_TK == 0)
    width = ATTN_HEADS_PER_STEP * LANES
    qspec = pl.BlockSpec((None, tq, width), lambda b, h, i: (b, i, h))
    in_specs, args = [qspec], [q]
    if n_lat:
        t = k.shape[1]
        in_specs += [pl.BlockSpec((None, t, width), lambda b, h, i: (b, 0, h)),
                     pl.BlockSpec((None, width, t), lambda b, h, i: (b, h, 0))]
        args += [k, vt]
    in_specs += [pl.BlockSpec((None, lc, width), lambda b, h, i: (b, 0, h)),
                 pl.BlockSpec((None, width, lc), lambda b, h, i: (b, h, 0)),
                 pl.BlockSpec((4, DIFF_HEAD_DIM), lambda b, h, i: (0, 0)),
                 pl.BlockSpec((1, DIFF_V_DIM), lambda b, h, i: (0, 0))]
    args += [kc, vct, lam_vec, g]
    return pl.pallas_call(
        functools.partial(_diff_attn_kernel, n_lat=n_lat, lc=lc, lambda_init=lambda_init),
        out_shape=jax.ShapeDtypeStruct((bsz, tq_total, DIFF_HEADS * DIFF_V_DIM), BF16),
        grid=(bsz, DIFF_HEADS // ATTN_HEADS_PER_STEP, tq_total // tq),
        in_specs=in_specs,
        out_specs=qspec,
        compiler_params=_params(("parallel", "parallel", "parallel")),
        name="diff_attention",
    )(*args)


def _win_attn_kernel(*refs, has_window, lc, tq):
    hd = WIN_HEAD_DIM
    if has_window:
        n_blk = tq // Q_BLOCK + 2
        q_ref = refs[0]
        k_refs, v_refs = refs[1:1 + n_blk], refs[1 + n_blk:1 + 2 * n_blk]
        kc_ref, vc_ref, sink_ref, o_ref = refs[1 + 2 * n_blk:]
        kw = jnp.concatenate([ref[...] for ref in k_refs] + [kc_ref[...]], axis=0)
        vt = jnp.concatenate([ref[...] for ref in v_refs] + [vc_ref[...]], axis=1)
        span = tq + 2 * WINDOW
        nk = span + lc
        i = pl.program_id(1)
        last = pl.num_programs(1) - 1
        c = lax.broadcasted_iota(jnp.int32, (nk, tq), 0)
        r = lax.broadcasted_iota(jnp.int32, (nk, tq), 1)
        lo_valid = jnp.where(i == 0, WINDOW, 0)
        hi_valid = jnp.where(i == last, WINDOW + tq, span)
        in_win = (c >= jnp.maximum(r, lo_valid)) & (c <= r + 2 * WINDOW) & (c < hi_valid)
        bias = jnp.where(in_win | (c >= span), 0.0, NEG_BIG)
        bias = jnp.concatenate([bias, bias], axis=1)
    else:
        q_ref, kc_ref, vc_ref, sink_ref, o_ref = refs
        kw, vt = kc_ref[...], vc_ref[...]
        nk = lc
        bias = None
    lane = lax.broadcasted_iota(jnp.int32, (nk, LANES), 1)
    zero = jnp.zeros((nk, LANES), BF16)
    ones = jnp.ones((BF16_SUBLANES, nk), BF16)
    units = [(kh, p) for kh in range(WIN_KV_HEADS) for p in range(2)]
    heads = {(kh, p): [kh * WIN_GROUP + 2 * j + p for j in range(2)] for kh, p in units}

    def scores(kh, p):
        ksrc = kw[:, (0 if p == kh else 1) * LANES:(1 if p == kh else 2) * LANES]
        kmat = jnp.where((lane < hd) if p == 0 else (lane >= hd), ksrc, zero)
        qcat = jnp.concatenate([q_ref[:, (h // 2) * LANES:(h // 2 + 1) * LANES] for h in heads[kh, p]], axis=0)
        return _dot_nt(kmat, qcat)

    out_t = {}
    s_next = scores(*units[0])
    for idx, (kh, p) in enumerate(units):
        s = s_next if bias is None else s_next + bias
        if idx + 1 < len(units):
            s_next = scores(*units[idx + 1])
        sink = jnp.concatenate([jnp.full((1, tq), sink_ref[h] * LOG2_E, F32) for h in heads[kh, p]], axis=1)
        m = jnp.maximum(jnp.max(s, axis=0, keepdims=True), sink)
        vt_ext = jnp.concatenate([vt[kh * hd:(kh + 1) * hd, :], ones], axis=0)
        acc = _dot(vt_ext, jnp.exp2(s - m).astype(BF16))
        denom = acc[hd:hd + 1, :] + jnp.exp2(sink - m)
        out_t[kh, p] = acc[:hd, :] * (1.0 / denom)
    for kh in range(WIN_KV_HEADS):
        for j in range(2):
            pair = kh * 2 + j
            pair_t = jnp.concatenate([out_t[kh, p][:, j * tq:(j + 1) * tq] for p in range(2)], axis=0)
            o_ref[:, pair * LANES:(pair + 1) * LANES] = pair_t.T.astype(BF16)


def _win_attn(q, k, vt, kc, vct, sink, has_window):
    bsz, tq_total, width = q.shape
    lc = kc.shape[1]
    tq = min(WIN_TQ, tq_total)
    assert tq_total % tq == 0 and tq % Q_BLOCK == 0
    nq = tq_total // tq
    per = tq // Q_BLOCK
    n_kblk = tq_total // Q_BLOCK
    qspec = pl.BlockSpec((None, tq, width), lambda b, i: (b, i, 0))
    in_specs, args = [qspec], [q]
    if has_window:
        def kblock(j):
            return lambda b, i: (b, jnp.clip(i * per + j - 1, 0, n_kblk - 1))
        blk = [kblock(j) for j in range(per + 2)]
        in_specs += [pl.BlockSpec((None, Q_BLOCK, 2 * LANES), lambda b, i, f=f: (*f(b, i), 0)) for f in blk]
        in_specs += [pl.BlockSpec((None, LANES, Q_BLOCK), lambda b, i, f=f: (f(b, i)[0], 0, f(b, i)[1])) for f in blk]
        args += [k] * len(blk) + [vt] * len(blk)
    in_specs += [pl.BlockSpec((None, lc, 2 * LANES), lambda b, i: (b, 0, 0)),
                 pl.BlockSpec((None, LANES, lc), lambda b, i: (b, 0, 0)),
                 pl.BlockSpec(memory_space=pltpu.SMEM)]
    args += [kc, vct, sink]
    return pl.pallas_call(
        functools.partial(_win_attn_kernel, has_window=has_window, lc=lc, tq=tq),
        out_shape=jax.ShapeDtypeStruct((bsz, tq_total, width), BF16),
        grid=(bsz, nq),
        in_specs=in_specs,
        out_specs=qspec,
        compiler_params=_params(("parallel", "parallel")),
        name="window_attention",
    )(*args)


_REC_COLS = (("mq", 512, BF16), ("mk", 512, BF16), ("mv", 512, BF16), ("mo", 512, BF16),
             ("rq", 256, BF16), ("rk", 256, BF16), ("rv", 512, BF16), ("rg", 512, BF16),
             ("gi", 128, F32), ("gf", 128, F32))
_N_DIR_HEADS = 2 * MLSTM_HEADS


def _rec_proj_kernel(x_ref, mod_ref, w_ref, gate_b_ref, *outs):
    x = x_ref[...]
    shift, scale = mod_ref[0:1, :], mod_ref[1:2, :]
    h = (x * (1.0 + scale) + shift).astype(BF16)
    lo = 0
    for (name, width, dtype), o_ref in zip(_REC_COLS, outs):
        p = _dot(h, w_ref[:, lo:lo + width])
        lo += width
        if name == "mk":
            p = p * (MLSTM_HEAD_DIM ** -0.5)
        elif name == "rk":
            p = p * (RET_QK_DIM ** -0.5)
        elif name == "gi":
            p = p + gate_b_ref[0:1, :]
        elif name == "gf":
            p = _log_sigmoid(p + gate_b_ref[1:2, :])
        o_ref[...] = p.astype(dtype)


def _rec_proj(x2, mod4, tokens_per_row, w, gate_b):
    n = x2.shape[0]
    tm = _token_tile(n, tokens_per_row)
    tile = lambda width: pl.BlockSpec((tm, width), lambda i: (i, 0))
    return pl.pallas_call(
        _rec_proj_kernel,
        out_shape=[jax.ShapeDtypeStruct((n, width), dtype) for _, width, dtype in _REC_COLS],
        grid=(n // tm,),
        in_specs=[tile(D_MODEL), _mod_spec(1, tokens_per_row // tm), _resident(w.shape), _resident((2, LANES))],
        out_specs=[tile(width) for _, width, _ in _REC_COLS],
        compiler_params=_params(("parallel",)),
        name="rec_in_proj",
    )(x2, mod4, w, gate_b)


def _scan_specs(ncc, ncl, width, rows, nb):
    ctx_f = pl.BlockSpec((nb, rows, width), lambda b, s: (b, jnp.minimum(s, ncc - 1), 0))
    ctx_b = pl.BlockSpec((nb, rows, width), lambda b, s: (b, jnp.maximum(ncc - 1 - s, 0), 0))
    lat_f = pl.BlockSpec((nb, rows, width), lambda b, s: (b, jnp.maximum(s - ncc, 0), 0))
    lat_b = pl.BlockSpec((nb, rows, width), lambda b, s: (b, jnp.minimum(ncl - 1 - (s - ncc), ncl - 1), 0))
    return ctx_f, ctx_b, lat_f, lat_b


def _tri_masks():
    r = lax.broadcasted_iota(jnp.int32, (CHUNK, CHUNK), 0)
    c = lax.broadcasted_iota(jnp.int32, (CHUNK, CHUNK), 1)
    return r, c


def _head_cols(ref, h):
    return ref[:, h * LANES:(h + 1) * LANES]


def _mlstm_step(srcs, masks, tri, c_ref, m_ref, with_output):
    units = [(d, h) for d in range(len(srcs)) for h in range(MLSTM_HEADS)]
    dv = MLSTM_HEAD_DIM
    ones = jnp.ones((CHUNK, LANES), BF16)
    v_ext = {(d, h): jnp.concatenate([_head_cols(srcs[d][2], h), ones], axis=1) for d, h in units}
    gate = []
    for d in range(len(srcs)):
        gi, gf = srcs[d][3][...], srcs[d][4][...]
        cum = jnp.dot(tri[d % 2], gf, preferred_element_type=F32, precision=lax.Precision.HIGHEST)
        b_end = cum[CHUNK - 1:CHUNK, :] if d % 2 == 0 else cum[0:1, :]
        m_prev = m_ref[d:d + 1, :]
        w_end = b_end - cum + gi
        m_new = jnp.maximum(b_end + m_prev, jnp.max(w_end, axis=0, keepdims=True))
        decay = jnp.exp(b_end + m_prev - m_new)
        w = jnp.exp(w_end - m_new)
        m_ref[d:d + 1, :] = m_new
        log_inter = cum + m_prev if with_output else None
        key_term = (gi - cum).T if with_output else None
        gate.append((cum, decay, w, log_inter, key_term))
    if with_output:
        qk, qcn, log_d, li, rmax, m_t, sm, w_inter, pv = {}, {}, {}, {}, {}, {}, {}, {}, {}
        for d, h in units:
            q = _head_cols(srcs[d][0], h)
            qk[d, h] = _dot_nt(q, _head_cols(srcs[d][1], h))
            qcn[d, h] = _dot(q, c_ref[d * MLSTM_HEADS + h].astype(BF16))
        for d, h in units:
            lane = (d % 2) * MLSTM_HEADS + h
            cum, _, _, log_inter, key_term = gate[d]
            log_d[d, h] = jnp.where(masks[d % 2], cum[:, lane:lane + 1] + key_term[lane:lane + 1, :], NEG_BIG)
            li[d, h] = log_inter[:, lane:lane + 1]
        for u in units:
            rmax[u] = jnp.max(log_d[u], axis=1, keepdims=True)
        for u in units:
            m_t[u] = jnp.maximum(li[u], rmax[u])
            sm[u] = (qk[u] * jnp.exp(log_d[u] - m_t[u])).astype(BF16)
            w_inter[u] = jnp.exp(li[u] - m_t[u])
        for u in units:
            pv[u] = _dot(sm[u], v_ext[u])
        for d, h in units:
            u = (d, h)
            both = pv[u] + w_inter[u] * qcn[u]
            den = jnp.maximum(jnp.abs(both[:, dv:]), jnp.exp(-m_t[u]))
            srcs[d][5][:, h * LANES:(h + 1) * LANES] = (both[:, :dv] * (1.0 / den)).astype(BF16)
    for d, h in units:
        row, lane = d * MLSTM_HEADS + h, (d % 2) * MLSTM_HEADS + h
        _, decay, w, _, _ = gate[d]
        kw = (_head_cols(srcs[d][1], h).astype(F32) * w[:, lane:lane + 1]).astype(BF16)
        c_ref[row] = decay[:, lane:lane + 1] * c_ref[row] + _dot_tn(kw, v_ext[d, h])


def _ret_tables(logit_ref, intra_ref, to_end_ref, from_start_ref):
    log_gamma = _log_sigmoid(logit_ref[...])
    r, c = _tri_masks()
    rf, cf = r.astype(F32), c.astype(F32)
    for d in range(2):
        for h in range(RET_HEADS):
            row = d * RET_HEADS + h
            lg = log_gamma[d:d + 1, h:h + 1]
            rel = (rf - cf) if d == 0 else (cf - rf)
            intra_ref[row] = jnp.where(rel >= 0.0, jnp.exp(lg * jnp.maximum(rel, 0.0)), 0.0)
            to_end_ref[row] = jnp.exp(lg * ((CHUNK - 1.0 - rf) if d == 0 else rf))
            from_start_ref[row] = jnp.exp(lg * ((rf + 1.0) if d == 0 else (CHUNK - rf)))


def _ret_step(srcs, s_ref, intra_ref, to_end_ref, from_start_ref, with_output):
    units = [(d, h) for d in range(len(srcs)) for h in range(RET_HEADS)]
    lane = lax.broadcasted_iota(jnp.int32, (CHUNK, LANES), 1)

    def k_head(d, h):
        pair = _head_cols(srcs[d][1], h // 2)
        keep = (lane < RET_QK_DIM) if h % 2 == 0 else (lane >= RET_QK_DIM)
        return jnp.where(keep, pair, jnp.zeros_like(pair))

    ks = {u: k_head(*u) for u in units}
    qk, qs = {}, {}
    if with_output:
        for d, h in units:
            q = _head_cols(srcs[d][0], h // 2)
            qk[d, h] = _dot_nt(q, ks[d, h])
            qs[d, h] = _dot(q, s_ref[d * RET_HEADS + h].astype(BF16))
        for d, h in units:
            tab = (d % 2) * RET_HEADS + h
            sc = (qk[d, h] * intra_ref[tab]).astype(BF16)
            out = _dot(sc, _head_cols(srcs[d][2], h)) + from_start_ref[tab] * qs[d, h]
            srcs[d][3][:, h * LANES:(h + 1) * LANES] = out.astype(BF16)
    for d, h in units:
        row, tab = d * RET_HEADS + h, (d % 2) * RET_HEADS + h
        kd = (ks[d, h].astype(F32) * to_end_ref[tab]).astype(BF16)
        chunk_decay = from_start_ref[tab, CHUNK - 1:CHUNK, :] if d % 2 == 0 else from_start_ref[tab, 0:1, :]
        s_ref[row] = chunk_decay * s_ref[row] + _dot_tn(kd, _head_cols(srcs[d][2], h))


def _rec_scan_kernel(mkc_f, mvc_f, gic_f, gfc_f, mkc_b, mvc_b, gic_b, gfc_b,
                     mq_f, mk_f, mv_f, gi_f, gf_f, mq_b, mk_b, mv_b, gi_b, gf_b,
                     rkc_f, rvc_f, rkc_b, rvc_b, rq_f, rk_f, rv_f, rq_b, rk_b, rv_b, logit_ref,
                     mo_f, mo_b, ro_f, ro_b,
                     c_ref, m_ref, s_ref, intra_ref, to_end_ref, from_start_ref, *, ncc):
    s = pl.program_id(1)

    @pl.when(s == 0)
    def _():
        c_ref[...] = jnp.zeros_like(c_ref)
        m_ref[...] = jnp.zeros_like(m_ref)
        s_ref[...] = jnp.zeros_like(s_ref)
        _ret_tables(logit_ref, intra_ref, to_end_ref, from_start_ref)

    r, c = _tri_masks()
    masks = (c <= r, c >= r)
    tri = (jnp.where(masks[0], 1.0, 0.0), jnp.where(masks[1], 1.0, 0.0))

    def run(m_f, m_b, r_f, r_b, with_output):
        n_batch = mkc_f.shape[0]
        for sub in range(SCAN_CHUNKS):
            lo = (sub * CHUNK, (SCAN_CHUNKS - 1 - sub) * CHUNK)
            view = lambda refs, bb, d: tuple(None if ref is None else _Rows(ref, bb, lo[d]) for ref in refs)
            m_srcs = [view(refs, bb, d) for bb in range(n_batch) for d, refs in enumerate((m_f, m_b))]
            r_srcs = [view(refs, bb, d) for bb in range(n_batch) for d, refs in enumerate((r_f, r_b))]
            _mlstm_step(m_srcs, masks, tri, c_ref, m_ref, with_output)
            _ret_step(r_srcs, s_ref, intra_ref, to_end_ref, from_start_ref, with_output)

    @pl.when(s < ncc)
    def _():
        run((None, mkc_f, mvc_f, gic_f, gfc_f, None), (None, mkc_b, mvc_b, gic_b, gfc_b, None),
            (None, rkc_f, rvc_f, None), (None, rkc_b, rvc_b, None), False)

    @pl.when(s >= ncc)
    def _():
        run((mq_f, mk_f, mv_f, gi_f, gf_f, mo_f), (mq_b, mk_b, mv_b, gi_b, gf_b, mo_b),
            (rq_f, rk_f, rv_f, ro_f), (rq_b, rk_b, rv_b, ro_b), True)


class _Rows:
    def __init__(self, ref, bb, lo):
        self.ref, self.bb, self.rows = ref, bb, slice(lo, lo + CHUNK)

    def __getitem__(self, idx):
        return self.ref[self.bb, self.rows, :] if idx is Ellipsis else self.ref[self.bb, self.rows, idx[1]]

    def __setitem__(self, idx, value):
        self.ref[self.bb, self.rows, idx[1]] = value


def _rec_scan(mq, mk, mv, gi, gf, rq, rk, rv, mkc, mvc, gic, gfc, rkc, rvc, logit):
    bsz, t, width = mq.shape
    rows = CHUNK * SCAN_CHUNKS
    assert t % rows == 0 and mkc.shape[1] % rows == 0
    ncl, ncc = t // rows, mkc.shape[1] // rows
    nb = SCAN_BATCH if bsz % SCAN_BATCH == 0 else 1
    cf, cb, lf, lb = _scan_specs(ncc, ncl, width, rows, nb)
    gcf, gcb, glf, glb = _scan_specs(ncc, ncl, LANES, rows, nb)
    kcf, kcb, klf, klb = _scan_specs(ncc, ncl, rq.shape[2], rows, nb)
    return pl.pallas_call(
        functools.partial(_rec_scan_kernel, ncc=ncc),
        out_shape=[jax.ShapeDtypeStruct((bsz, t, width), BF16)] * 4,
        grid=(bsz // nb, ncc + ncl),
        in_specs=[cf, cf, gcf, gcf, cb, cb, gcb, gcb, lf, lf, lf, glf, glf, lb, lb, lb, glb, glb,
                  kcf, cf, kcb, cb, klf, klf, lf, klb, klb, lb, pl.BlockSpec((8, LANES), lambda b, s: (0, 0))],
        out_specs=[lf, lb, lf, lb],
        scratch_shapes=[pltpu.VMEM((nb * _N_DIR_HEADS, MLSTM_HEAD_DIM, 2 * MLSTM_HEAD_DIM), F32),
                        pltpu.VMEM((8, LANES), F32),
                        pltpu.VMEM((nb * 2 * RET_HEADS, LANES, RET_V_DIM), F32)]
        + [pltpu.VMEM((2 * RET_HEADS, CHUNK, LANES), F32)] * 3,
        compiler_params=_params(("parallel", "arbitrary")),
        name="rec_scan",
    )(mkc, mvc, gic, gfc, mkc, mvc, gic, gfc, mq, mk, mv, gi, gf, mq, mk, mv, gi, gf,
      rkc, rvc, rkc, rvc, rq, rk, rv, rq, rk, rv, logit)


def _rec_out_kernel(x_ref, mod_ref, mf_ref, mb_ref, rf_ref, rb_ref, o_ref_in, rg_ref, mg_ref, rng_ref, rnb_ref,
                    w_ref, g_ref, b_ref, out_ref):
    hm = mf_ref[...].astype(F32) + mb_ref[...].astype(F32)
    hr = rf_ref[...].astype(F32) + rb_ref[...].astype(F32)
    o_gate = jax.nn.sigmoid(o_ref_in[...].astype(F32))
    rg = rg_ref[...].astype(F32)
    r_gate = rg * jax.nn.sigmoid(rg)
    parts_m, parts_r = [], []
    for h in range(MLSTM_HEADS):
        cols = slice(h * LANES, (h + 1) * LANES)
        parts_m.append(_layer_norm(hm[:, cols], mg_ref[:, cols], None) * o_gate[:, cols])
        parts_r.append(_layer_norm(hr[:, cols], rng_ref[:, cols], rnb_ref[:, cols]) * r_gate[:, cols])
    width = MLSTM_HEADS * MLSTM_HEAD_DIM
    y = (_dot(jnp.concatenate(parts_m, axis=1).astype(BF16), w_ref[0:width, :])
         + _dot(jnp.concatenate(parts_r, axis=1).astype(BF16), w_ref[width:, :]))
    z = ALPHA * x_ref[...] + mod_ref[2:3, :] * y
    out_ref[...] = _layer_norm(z, g_ref[...], b_ref[...])


def _rec_out(x2, mod4, tokens_per_row, mf, mb, rf, rb, o, rg, mg, rng, rnb, w, g, b):
    n = x2.shape[0]
    tm = _token_tile(n, tokens_per_row)
    tile = lambda width: pl.BlockSpec((tm, width), lambda i: (i, 0))
    half = tile(512)
    return pl.pallas_call(
        _rec_out_kernel,
        out_shape=jax.ShapeDtypeStruct((n, D_MODEL), F32),
        grid=(n // tm,),
        in_specs=[tile(D_MODEL), _mod_spec(1, tokens_per_row // tm), half, half, half, half, half, half,
                  _resident((1, 512)), _resident((1, 512)), _resident((1, 512)),
                  _resident(w.shape), _resident((1, D_MODEL)), _resident((1, D_MODEL))],
        out_specs=tile(D_MODEL),
        compiler_params=_params(("parallel",)),
        name="rec_out_proj",
    )(x2, mod4, mf, mb, rf, rb, o, rg, mg, rng, rnb, w, g, b)


def _attn_weights(w_in, w_out):
    aq, ak, av, bq, bk, bv = jnp.split(w_in, [512, 1024, 1536, 2048, 2176], axis=1)
    bk_swapped = jnp.concatenate([bk[:, WIN_HEAD_DIM:], bk[:, :WIN_HEAD_DIM]], axis=1)
    w = jnp.concatenate([aq, ak, av, bq, bk, bk_swapped, bv], axis=1)
    wa, wb = w_out[:DIFF_HEADS * DIFF_V_DIM], w_out[DIFF_HEADS * DIFF_V_DIM:]
    return w.astype(BF16), wa.astype(BF16), wb.astype(BF16)


def _rec_weights(w_in):
    mq, mk, mv, mo, mg, rq, rk, rv, rg = jnp.split(w_in, [512, 1024, 1536, 2048, 2064, 2320, 2576, 3088], axis=1)
    gi, gf = _split_gates(mg)
    w = jnp.concatenate([mq, mk, mv, mo, rq, rk, rv, rg, gi, gf], axis=1)
    return w.astype(BF16)


def _split_gates(g):
    g = g.reshape(*g.shape[:-1], 2, 2, MLSTM_HEADS)
    pad = [(0, 0)] * (g.ndim - 3) + [(0, LANES - _N_DIR_HEADS)]
    return tuple(jnp.pad(g[..., io, :].reshape(*g.shape[:-3], _N_DIR_HEADS), pad) for io in range(2))


def _rope_tables(t):
    rows = t // GRID_W
    row = jnp.repeat(jnp.arange(rows), GRID_W)
    col = jnp.tile(jnp.arange(GRID_W), rows)
    inv = ROPE_BASE ** (-jnp.arange(ROPE_FREQS, dtype=F32) / ROPE_FREQS)
    ang_r, ang_c = row[:, None] * inv, col[:, None] * inv
    cos = jnp.concatenate([jnp.cos(ang_r)] * 2 + [jnp.cos(ang_c)] * 2, axis=1)
    sin = jnp.concatenate([-jnp.sin(ang_r), jnp.sin(ang_r), -jnp.sin(ang_c), jnp.sin(ang_c)], axis=1)
    return jnp.tile(cos, (1, 2)), jnp.tile(sin, (1, 2))


def kernel(x, c, ctx, c_ctx, ada_w, ada_b, ln_g, ln_b, ffn_w_in, ffn_w_out, attn_w_in, attn_w_out,
           diff_lambda, diff_norm_g, sink_logits, rec_w_in, rec_w_out, mlstm_gate_b, mlstm_norm_g,
           ret_decay_logit, ret_norm_g, ret_norm_b):
    bsz, t, d = x.shape
    lc = ctx.shape[1]
    assert d == D_MODEL and t % ATTN_TK == 0 and lc % CHUNK == 0
    out_dtype = x.dtype
    x2 = x.reshape(bsz * t, d).astype(F32)
    c2 = ctx.reshape(bsz * lc, d).astype(F32)
    mod_rows = 16
    cc = jnp.zeros((mod_rows, d), F32).at[:bsz].set(c).at[bsz].set(c_ctx)
    row2 = lambda v: v.reshape(1, -1).astype(F32)
    ada_b3 = ada_b.reshape(DEPTH, 1, -1)
    ffn_w_in16, ffn_w_out16 = ffn_w_in.astype(BF16), ffn_w_out.astype(BF16)

    for l in range(DEPTH):
        last = l == DEPTH - 1
        i = l // 2
        mod = _modulation(cc, ada_w, ada_b3, l).reshape(mod_rows, 3, 3, d)
        mod_x, mod_c = mod[:bsz], mod[bsz:bsz + 1]
        ffn = lambda z, m, sub, tpr, j: _ffn(z, m, sub, tpr, ffn_w_in16, ffn_w_out16, l, j,
                                             row2(ln_g[l, sub]), row2(ln_b[l, sub]))
        x2 = ffn(x2, mod_x, 0, t, 0)
        c2 = ffn(c2, mod_c, 0, bsz * lc, 0)
        g1, b1 = row2(ln_g[l, 1]), row2(ln_b[l, 1])
        if l % 2 == 0:
            lambda_init = 0.8 - 0.6 * math.exp(-0.3 * l)
            w, wa, wb = _attn_weights(attn_w_in[i], attn_w_out[i])
            aq, ak, av, bq, bk, bv = [a.reshape(bsz, t, -1) for a in _attn_proj(x2, mod_x, t, w, _rope_tables(t))]
            aqc, akc, avc, bqc, bkc, bvc = [a.reshape(bsz, lc, -1) for a in _attn_proj(c2, mod_c, bsz * lc, w, None)]
            avt, avct = jnp.swapaxes(av, 1, 2), jnp.swapaxes(avc, 1, 2)
            bvt, bvct = jnp.swapaxes(bv, 1, 2), jnp.swapaxes(bvc, 1, 2)
            lam_vec, sub_g = diff_lambda[i].astype(F32), row2(diff_norm_g[i])
            sink = sink_logits[i].astype(F32)
            a_x = _diff_attn(aq, ak, avt, akc, avct, lam_vec, sub_g, lambda_init)
            b_x = _win_attn(bq, bk, bvt, bkc, bvct, sink, True)
            flat = lambda a: a.reshape(-1, a.shape[-1])
            if not last:
                a_c = _diff_attn(aqc, None, None, akc, avct, lam_vec, sub_g, lambda_init)
                b_c = _win_attn(bqc, None, None, bkc, bvct, sink, False)
                c2_mix = _attn_out(c2, mod_c, bsz * lc, flat(a_c), flat(b_c), wa, wb, g1, b1)
            x2 = _attn_out(x2, mod_x, t, flat(a_x), flat(b_x), wa, wb, g1, b1)
        else:
            w = _rec_weights(rec_w_in[i])
            gate_b = jnp.stack(_split_gates(mlstm_gate_b[i].reshape(-1).astype(F32)))
            px = [a.reshape(bsz, t, -1) for a in _rec_proj(x2, mod_x, t, w, gate_b)]
            pc = [a.reshape(bsz, lc, -1) for a in _rec_proj(c2, mod_c, bsz * lc, w, gate_b)]
            mq, mk, mv, mo, rq, rk, rv, rg, gi, gf = px
            _, mkc, mvc, _, _, rkc, rvc, _, gic, gfc = pc
            assert last
            logit = jnp.zeros((8, LANES), F32).at[:2, :RET_HEADS].set(ret_decay_logit[i].astype(F32))
            m_f, m_b, r_f, r_b = _rec_scan(mq, mk, mv, gi, gf, rq, rk, rv, mkc, mvc, gic, gfc, rkc, rvc, logit)
            flat = lambda a: a.reshape(-1, a.shape[-1])
            x2 = _rec_out(x2, mod_x, t, flat(m_f), flat(m_b), flat(r_f), flat(r_b), flat(mo), flat(rg),
                          row2(mlstm_norm_g[i]), row2(ret_norm_g[i]), row2(ret_norm_b[i]),
                          rec_w_out[i].astype(BF16), g1, b1)
        x2 = ffn(x2, mod_x, 2, t, 1)
        if not last:
            c2 = ffn(c2_mix, mod_c, 2, bsz * lc, 1)
    return x2.reshape(bsz, t, d).astype(out_dtype)
```

```python
import functools
import math

import jax
import jax.numpy as jnp
from jax import lax
from jax.experimental import pallas as pl
from jax.experimental.pallas import tpu as pltpu

F32 = jnp.float32
BF16 = jnp.bfloat16

D_MODEL = 1024
DEPTH = 2
GRID_W = 64
D_FF = 2816
DIFF_HEADS = 4
DIFF_HEAD_DIM = 64
DIFF_V_DIM = 128
WIN_HEADS = 8
WIN_KV_HEADS = 2
WIN_GROUP = WIN_HEADS // WIN_KV_HEADS
WIN_HEAD_DIM = 64
WINDOW = 128
Q_BLOCK = 128
ROPE_BASE = 10000.0
ROPE_FREQS = 16
MLSTM_HEADS = 4
MLSTM_HEAD_DIM = 128
RET_HEADS = 4
RET_QK_DIM = 64
RET_V_DIM = 128
CHUNK = 128
ALPHA = (2.0 * DEPTH) ** 0.25
EPS = 1e-5
LOG2_E = 1.4426950408889634
NEG_BIG = -1e30

LANES = 128
BF16_SUBLANES = 16
V7X_VMEM_LIMIT_BYTES = 56 * 1024 * 1024

TOKEN_TILE = 1024
FFN_TILE = 256
FF_CHUNK = 256
FFN_SUB_TILES = 4
ATTN_TQ = 512
ATTN_TK = 1024
ATTN_PIECE = 256
ATTN_HEADS_PER_STEP = 2
WIN_TQ = 256
SCAN_CHUNKS = 1
SCAN_BATCH = 4


def _params(semantics):
    return pltpu.CompilerParams(dimension_semantics=semantics, vmem_limit_bytes=V7X_VMEM_LIMIT_BYTES)


def _resident(shape):
    return pl.BlockSpec(shape, lambda *_: (0,) * len(shape), pipeline_mode=pl.Buffered(1))


def _layer_norm(y, g, b):
    mu = jnp.mean(y, axis=-1, keepdims=True)
    yc = y - mu
    var = jnp.mean(yc * yc, axis=-1, keepdims=True)
    out = yc * lax.rsqrt(var + EPS) * g
    return out if b is None else out + b


def _log_sigmoid(x):
    return jnp.minimum(x, 0.0) - jnp.log1p(jnp.exp(-jnp.abs(x)))


def _dot(a, b):
    return jnp.dot(a, b, preferred_element_type=F32)


def _dot_nt(a, b):
    return lax.dot_general(a, b, (((1,), (1,)), ((), ())), preferred_element_type=F32)


def _dot_tn(a, b):
    return lax.dot_general(a, b, (((0,), (0,)), ((), ())), preferred_element_type=F32)


def _split_bf16(a):
    hi = a.astype(BF16)
    return hi, (a - hi.astype(F32)).astype(BF16)


def _mod_kernel(c_ref, w_ref, b_ref, o_ref):
    c = c_ref[...]
    h_hi, h_lo = _split_bf16(c * jax.nn.sigmoid(c))
    w_hi, w_lo = _split_bf16(w_ref[...])
    o_ref[...] = _dot(h_hi, w_hi) + (_dot(h_hi, w_lo) + _dot(h_lo, w_hi) + _dot(h_lo, w_lo)) + b_ref[...]


def _modulation(cc, w, b, layer):
    rows, d = cc.shape
    n = w.shape[2]
    tn = 1024
    return pl.pallas_call(
        _mod_kernel,
        out_shape=jax.ShapeDtypeStruct((rows, n), F32),
        grid=(n // tn,),
        in_specs=[pl.BlockSpec((rows, d), lambda j: (0, 0)),
                  pl.BlockSpec((None, d, tn), lambda j: (layer, 0, j)),
                  pl.BlockSpec((None, 1, tn), lambda j: (layer, 0, j))],
        out_specs=pl.BlockSpec((rows, tn), lambda j: (0, j)),
        compiler_params=_params(("parallel",)),
        name="modulation",
    )(cc, w, b)


def _token_tile(n_tokens, tokens_per_row, tile=None):
    tm = min(tile or TOKEN_TILE, tokens_per_row)
    assert tokens_per_row % tm == 0 and n_tokens % tm == 0
    return tm


def _mod_spec(sub, tiles_per_row):
    return pl.BlockSpec((None, None, 3, D_MODEL), lambda i: (i // tiles_per_row, sub, 0, 0))


def _ffn_kernel(x_ref, mod_ref, w_in_ref, w_out_ref, g_ref, b_ref, o_ref, act_ref, *, n_sub):
    shift, scale, gate = mod_ref[0:1, :], mod_ref[1:2, :], mod_ref[2:3, :]
    rows = x_ref.shape[0] // n_sub

    def finish(s):
        x = x_ref[s * rows:(s + 1) * rows, :]
        y = ALPHA * x + 0.5 * gate * _dot(act_ref[s], w_out_ref[...])
        o_ref[s * rows:(s + 1) * rows, :] = _layer_norm(y, g_ref[...], b_ref[...])

    for s in range(n_sub):
        h = (x_ref[s * rows:(s + 1) * rows, :] * (1.0 + scale) + shift).astype(BF16)
        for c in range(D_FF // FF_CHUNK):
            lo = c * FF_CHUNK
            gt = _dot(h, w_in_ref[:, lo:lo + FF_CHUNK])
            up = _dot(h, w_in_ref[:, D_FF + lo:D_FF + lo + FF_CHUNK])
            act_ref[s, :, lo:lo + FF_CHUNK] = (gt * jax.nn.sigmoid(gt) * up).astype(BF16)
            if c == 0 and s > 0:
                finish(s - 1)
    finish(n_sub - 1)


def _ffn(x2, mod4, sub, tokens_per_row, w_in, w_out, layer, which, g, b):
    n = x2.shape[0]
    tm = _token_tile(n, tokens_per_row, FFN_TILE)
    n_sub = FFN_SUB_TILES if (tokens_per_row % (FFN_SUB_TILES * tm) == 0) else 1
    tb = tm * n_sub
    tile = pl.BlockSpec((tb, D_MODEL), lambda i: (i, 0))
    weight = lambda shape: pl.BlockSpec((None, None) + shape, lambda i: (layer, which, 0, 0),
                                        pipeline_mode=pl.Buffered(1))
    return pl.pallas_call(
        functools.partial(_ffn_kernel, n_sub=n_sub),
        out_shape=jax.ShapeDtypeStruct((n, D_MODEL), F32),
        grid=(n // tb,),
        in_specs=[tile, _mod_spec(sub, tokens_per_row // tb),
                  weight((D_MODEL, 2 * D_FF)), weight((D_FF, D_MODEL)),
                  _resident((1, D_MODEL)), _resident((1, D_MODEL))],
        out_specs=tile,
        scratch_shapes=[pltpu.VMEM((n_sub, tm, D_FF), BF16)],
        compiler_params=_params(("parallel",)),
        name="ffn_sublayer",
    )(x2, mod4, w_in, w_out, g, b)


def _rope(p, cos, sin):
    lane = lax.broadcasted_iota(jnp.int32, (p.shape[0], LANES), 1)
    first_half = (lane & ROPE_FREQS) == 0
    outs = []
    for j in range(p.shape[1] // LANES):
        xs = p[:, j * LANES:(j + 1) * LANES]
        partner = jnp.where(first_half, pltpu.roll(xs, LANES - ROPE_FREQS, 1), pltpu.roll(xs, ROPE_FREQS, 1))
        outs.append(xs * cos + partner * sin)
    return jnp.concatenate(outs, axis=1)


_ATTN_COLS = (("aq", 512), ("ak", 512), ("av", 512), ("bq", 512), ("bk", 256), ("bv", 128))
_ATTN_GROUPS = (("aq",), ("ak",), ("av",), ("bq",), ("bk", "bv"))


def _attn_proj_kernel(*refs, rope):
    if rope:
        x_ref, mod_ref, w_ref, cos_ref, sin_ref = refs[:5]
        outs = refs[5:]
        cos, sin = cos_ref[...], sin_ref[...]
    else:
        x_ref, mod_ref, w_ref = refs[:3]
        outs = refs[3:]
    x = x_ref[...]
    shift, scale = mod_ref[0:1, :], mod_ref[1:2, :]
    h = (x * (1.0 + scale) + shift).astype(BF16)
    widths = dict(_ATTN_COLS)
    out_refs = dict(zip(widths, outs))
    lo = 0
    for group in _ATTN_GROUPS:
        total = sum(widths[name] for name in group)
        pg = _dot(h, w_ref[:, lo:lo + total])
        lo += total
        off = 0
        for name in group:
            p = pg[:, off:off + widths[name]]
            off += widths[name]
            if rope and name in ("aq", "ak", "bq", "bk"):
                p = _rope(p, cos, sin)
            if name in ("aq", "bq"):
                head_dim = DIFF_HEAD_DIM if name == "aq" else WIN_HEAD_DIM
                p = p * (head_dim ** -0.5 * LOG2_E)
            out_refs[name][...] = p.astype(BF16)


def _attn_proj(x2, mod4, tokens_per_row, w, tables):
    n = x2.shape[0]
    tm = _token_tile(n, tokens_per_row)
    rope = tables is not None
    tile = lambda width: pl.BlockSpec((tm, width), lambda i: (i, 0))
    in_specs = [tile(D_MODEL), _mod_spec(1, tokens_per_row // tm), _resident(w.shape)]
    args = [x2, mod4, w]
    if rope:
        tpr = tokens_per_row // tm
        tab = pl.BlockSpec((tm, LANES), lambda i: (i % tpr, 0))
        in_specs += [tab, tab]
        args += list(tables)
    return pl.pallas_call(
        functools.partial(_attn_proj_kernel, rope=rope),
        out_shape=[jax.ShapeDtypeStruct((n, width), BF16) for _, width in _ATTN_COLS],
        grid=(n // tm,),
        in_specs=in_specs,
        out_specs=[tile(width) for _, width in _ATTN_COLS],
        compiler_params=_params(("parallel",)),
        name="attn_in_proj",
    )(*args)


def _attn_out_kernel(x_ref, mod_ref, a_ref, b_ref, wa_ref, wb_ref, g_ref, bb_ref, o_ref):
    y = _dot(a_ref[...], wa_ref[...]) + _dot(b_ref[...], wb_ref[...])
    z = ALPHA * x_ref[...] + mod_ref[2:3, :] * y
    o_ref[...] = _layer_norm(z, g_ref[...], bb_ref[...])


def _attn_out(x2, mod4, tokens_per_row, a, b, wa, wb, g, bb):
    n = x2.shape[0]
    tm = _token_tile(n, tokens_per_row)
    tile = lambda width: pl.BlockSpec((tm, width), lambda i: (i, 0))
    return pl.pallas_call(
        _attn_out_kernel,
        out_shape=jax.ShapeDtypeStruct((n, D_MODEL), F32),
        grid=(n // tm,),
        in_specs=[tile(D_MODEL), _mod_spec(1, tokens_per_row // tm), tile(a.shape[1]), tile(b.shape[1]),
                  _resident(wa.shape), _resident(wb.shape), _resident((1, D_MODEL)), _resident((1, D_MODEL))],
        out_specs=tile(D_MODEL),
        compiler_params=_params(("parallel",)),
        name="attn_out_proj",
    )(x2, mod4, a, b, wa, wb, g, bb)


def _diff_attn_kernel(*refs, n_lat, lc, lambda_init):
    if n_lat:
        q_ref, k_ref, vt_ref, kc_ref, vct_ref, lam_ref, g_ref, o_ref = refs
    else:
        q_ref, kc_ref, vct_ref, lam_ref, g_ref, o_ref = refs
    t = lam_ref[...]
    lam = (jnp.exp(jnp.sum(t[0:1, :] * t[1:2, :], axis=1, keepdims=True))
           - jnp.exp(jnp.sum(t[2:3, :] * t[3:4, :], axis=1, keepdims=True)) + lambda_init)
    n_heads = q_ref.shape[1] // LANES
    ctx_pieces = [(kc_ref, vct_ref, j * ATTN_PIECE) for j in range(lc // ATTN_PIECE)]
    chunks = [[(k_ref, vt_ref, c * ATTN_TK + j * ATTN_PIECE) for j in range(ATTN_TK // ATTN_PIECE)]
              for c in range(n_lat)] + [ctx_pieces]
    stages = [(hd, idx, idx == len(chunks) - 1, chunk) for hd in range(n_heads) for idx, chunk in enumerate(chunks)]

    def n_pieces(stage):
        return len(stage[3])

    def scores(stage, piece, sub):
        hd, _, _, pieces = stage
        kr, _, lo = pieces[piece]
        q = q_ref[:, hd * LANES:(hd + 1) * LANES]
        kblk = kr[lo:lo + ATTN_PIECE, hd * LANES:(hd + 1) * LANES]
        lane = lax.broadcasted_iota(jnp.int32, kblk.shape, 1)
        keep = (lane < DIFF_HEAD_DIM) if sub == 0 else (lane >= DIFF_HEAD_DIM)
        return _dot_nt(jnp.where(keep, kblk, jnp.zeros_like(kblk)), q)

    def finish(hd, acc):
        num = [a[:DIFF_V_DIM, :] * (1.0 / a[DIFF_V_DIM:DIFF_V_DIM + 1, :]) for a in acc]
        o = (num[0] - lam * num[1]).T
        o = o * lax.rsqrt(jnp.mean(o * o, axis=-1, keepdims=True) + EPS) * g_ref[...]
        o_ref[:, hd * LANES:(hd + 1) * LANES] = (o * (1.0 - lambda_init)).astype(BF16)

    m, acc = [None, None], [None, None]
    s_next = [[scores(stages[0], j, sub) for j in range(n_pieces(stages[0]))] for sub in range(2)]
    for si, stage in enumerate(stages):
        hd, idx, is_last, pieces = stage
        values = jnp.concatenate([vr[hd * LANES:(hd + 1) * LANES, lo:lo + ATTN_PIECE] for _, vr, lo in pieces], axis=1)
        vt = jnp.concatenate([values, jnp.ones((BF16_SUBLANES, values.shape[1]), BF16)], axis=0)
        nxt = stages[si + 1] if si + 1 < len(stages) else None
        for sub in range(2):
            s_cur, s_new = s_next[sub], []
            smax = functools.reduce(jnp.maximum, [jnp.max(s, axis=0, keepdims=True) for s in s_cur])
            m_new = smax if idx == 0 else jnp.maximum(m[sub], smax)
            ps = []
            for j, s in enumerate(s_cur):
                if nxt is not None and j < n_pieces(nxt):
                    s_new.append(scores(nxt, j, sub))
                ps.append(jnp.exp2(s - m_new).astype(BF16))
            if nxt is not None:
                s_new += [scores(nxt, j, sub) for j in range(len(s_new), n_pieces(nxt))]
            part = _dot(vt, jnp.concatenate(ps, axis=0))
            acc[sub] = part if idx == 0 else jnp.exp2(m[sub] - m_new) * acc[sub] + part
            m[sub], s_next[sub] = m_new, s_new
        if is_last:
            finish(hd, acc)


def _diff_attn(q, k, vt, kc, vct, lam_vec, g, lambda_init):
    bsz, tq_total, _ = q.shape
    lc = kc.shape[1]
    tq = min(ATTN_TQ, tq_total)
    n_lat = 0 if k is None else k.shape[1] // ATTN_TK
    assert tq_total % tq == 0 and (k is None or k.shape[1] % ATTN_TK == 0)
    width = ATTN_HEADS_PER_STEP * LANES
    qspec = pl.BlockSpec((None, tq, width), lambda b, h, i: (b, i, h))
    in_specs, args = [qspec], [q]
    if n_lat:
        t = k.shape[1]
        in_specs += [pl.BlockSpec((None, t, width), lambda b, h, i: (b, 0, h)),
                     pl.BlockSpec((None, width, t), lambda b, h, i: (b, h, 0))]
        args += [k, vt]
    in_specs += [pl.BlockSpec((None, lc, width), lambda b, h, i: (b, 0, h)),
                 pl.BlockSpec((None, width, lc), lambda b, h, i: (b, h, 0)),
                 pl.BlockSpec((4, DIFF_HEAD_DIM), lambda b, h, i: (0, 0)),
                 pl.BlockSpec((1, DIFF_V_DIM), lambda b, h, i: (0, 0))]
    args += [kc, vct, lam_vec, g]
    return pl.pallas_call(
        functools.partial(_diff_attn_kernel, n_lat=n_lat, lc=lc, lambda_init=lambda_init),
        out_shape=jax.ShapeDtypeStruct((bsz, tq_total, DIFF_HEADS * DIFF_V_DIM), BF16),
        grid=(bsz, DIFF_HEADS // ATTN_HEADS_PER_STEP, tq_total // tq),
        in_specs=in_specs,
        out_specs=qspec,
        compiler_params=_params(("parallel", "parallel", "parallel")),
        name="diff_attention",
    )(*args)


def _win_attn_kernel(*refs, has_window, lc, tq):
    hd = WIN_HEAD_DIM
    if has_window:
        n_blk = tq // Q_BLOCK + 2
        q_ref = refs[0]
        k_refs, v_refs = refs[1:1 + n_blk], refs[1 + n_blk:1 + 2 * n_blk]
        kc_ref, vc_ref, sink_ref, o_ref = refs[1 + 2 * n_blk:]
        kw = jnp.concatenate([ref[...] for ref in k_refs] + [kc_ref[...]], axis=0)
        vt = jnp.concatenate([ref[...] for ref in v_refs] + [vc_ref[...]], axis=1)
        span = tq + 2 * WINDOW
        nk = span + lc
        i = pl.program_id(1)
        last = pl.num_programs(1) - 1
        c = lax.broadcasted_iota(jnp.int32, (span, tq), 0)
        r = lax.broadcasted_iota(jnp.int32, (span, tq), 1)
        lo_valid = jnp.where(i == 0, WINDOW, 0)
        hi_valid = jnp.where(i == last, WINDOW + tq, span)
        as_u32 = lambda v: lax.bitcast_convert_type(v, jnp.uint32)
        band = as_u32(c - r) <= jnp.uint32(2 * WINDOW)
        inside = as_u32(c - lo_valid) < as_u32(hi_valid - lo_valid)
        win_bias = jnp.where(band, jnp.where(inside, 0.0, NEG_BIG), NEG_BIG)
        bias = jnp.concatenate([win_bias, jnp.zeros((lc, tq), F32)], axis=0)
        bias = jnp.concatenate([bias, bias], axis=1)
    else:
        q_ref, kc_ref, vc_ref, sink_ref, o_ref = refs
        kw, vt = kc_ref[...], vc_ref[...]
        nk = lc
        bias = None
    lane = lax.broadcasted_iota(jnp.int32, (nk, LANES), 1)
    zero = jnp.zeros((nk, LANES), BF16)
    ones = jnp.ones((BF16_SUBLANES, nk), BF16)
    units = [(kh, p) for kh in range(WIN_KV_HEADS) for p in range(2)]
    heads = {(kh, p): [kh * WIN_GROUP + 2 * j + p for j in range(2)] for kh, p in units}

    def scores(kh, p):
        ksrc = kw[:, (0 if p == kh else 1) * LANES:(1 if p == kh else 2) * LANES]
        kmat = jnp.where((lane < hd) if p == 0 else (lane >= hd), ksrc, zero)
        qcat = jnp.concatenate([q_ref[:, (h // 2) * LANES:(h // 2 + 1) * LANES] for h in heads[kh, p]], axis=0)
        return _dot_nt(kmat, qcat)

    out_t = {}
    s_next = scores(*units[0])
    for idx, (kh, p) in enumerate(units):
        s = s_next if bias is None else s_next + bias
        if idx + 1 < len(units):
            s_next = scores(*units[idx + 1])
        sink = jnp.concatenate([jnp.full((1, tq), sink_ref[h] * LOG2_E, F32) for h in heads[kh, p]], axis=1)
        m = jnp.maximum(jnp.max(s, axis=0, keepdims=True), sink)
        vt_ext = jnp.concatenate([vt[kh * hd:(kh + 1) * hd, :], ones], axis=0)
        acc = _dot(vt_ext, jnp.exp2(s - m).astype(BF16))
        denom = acc[hd:hd + 1, :] + jnp.exp2(sink - m)
        out_t[kh, p] = acc[:hd, :] * (1.0 / denom)
    for kh in range(WIN_KV_HEADS):
        for j in range(2):
            pair = kh * 2 + j
            pair_t = jnp.concatenate([out_t[kh, p][:, j * tq:(j + 1) * tq] for p in range(2)], axis=0)
            o_ref[:, pair * LANES:(pair + 1) * LANES] = pair_t.T.astype(BF16)


def _win_attn(q, k, vt, kc, vct, sink, has_window):
    bsz, tq_total, width = q.shape
    lc = kc.shape[1]
    tq = min(WIN_TQ, tq_total)
    assert tq_total % tq == 0 and tq % Q_BLOCK == 0
    nq = tq_total // tq
    per = tq // Q_BLOCK
    n_kblk = tq_total // Q_BLOCK
    qspec = pl.BlockSpec((None, tq, width), lambda b, i: (b, i, 0))
    in_specs, args = [qspec], [q]
    if has_window:
        def kblock(j):
            return lambda b, i: (b, jnp.clip(i * per + j - 1, 0, n_kblk - 1))
        blk = [kblock(j) for j in range(per + 2)]
        in_specs += [pl.BlockSpec((None, Q_BLOCK, 2 * LANES), lambda b, i, f=f: (*f(b, i), 0)) for f in blk]
        in_specs += [pl.BlockSpec((None, LANES, Q_BLOCK), lambda b, i, f=f: (f(b, i)[0], 0, f(b, i)[1])) for f in blk]
        args += [k] * len(blk) + [vt] * len(blk)
    in_specs += [pl.BlockSpec((None, lc, 2 * LANES), lambda b, i: (b, 0, 0)),
                 pl.BlockSpec((None, LANES, lc), lambda b, i: (b, 0, 0)),
                 pl.BlockSpec(memory_space=pltpu.SMEM)]
    args += [kc, vct, sink]
    return pl.pallas_call(
        functools.partial(_win_attn_kernel, has_window=has_window, lc=lc, tq=tq),
        out_shape=jax.ShapeDtypeStruct((bsz, tq_total, width), BF16),
        grid=(bsz, nq),
        in_specs=in_specs,
        out_specs=qspec,
        compiler_params=_params(("parallel", "parallel")),
        name="window_attention",
    )(*args)


_REC_COLS = (("mq", 512, BF16), ("mk", 512, BF16), ("mv", 512, BF16), ("mo", 512, BF16),
             ("rq", 256, BF16), ("rk", 256, BF16), ("rv", 512, BF16), ("rg", 512, BF16),
             ("gi", 128, F32), ("gf", 128, F32))
_N_DIR_HEADS = 2 * MLSTM_HEADS


def _rec_proj_kernel(x_ref, mod_ref, w_ref, gate_b_ref, *outs):
    x = x_ref[...]
    shift, scale = mod_ref[0:1, :], mod_ref[1:2, :]
    h = (x * (1.0 + scale) + shift).astype(BF16)
    lo = 0
    for (name, width, dtype), o_ref in zip(_REC_COLS, outs):
        p = _dot(h, w_ref[:, lo:lo + width])
        lo += width
        if name == "mk":
            p = p * (MLSTM_HEAD_DIM ** -0.5)
        elif name == "rk":
            p = p * (RET_QK_DIM ** -0.5)
        elif name == "gi":
            p = p + gate_b_ref[0:1, :]
        elif name == "gf":
            p = _log_sigmoid(p + gate_b_ref[1:2, :])
        o_ref[...] = p.astype(dtype)


def _rec_proj(x2, mod4, tokens_per_row, w, gate_b):
    n = x2.shape[0]
    tm = _token_tile(n, tokens_per_row)
    tile = lambda width: pl.BlockSpec((tm, width), lambda i: (i, 0))
    return pl.pallas_call(
        _rec_proj_kernel,
        out_shape=[jax.ShapeDtypeStruct((n, width), dtype) for _, width, dtype in _REC_COLS],
        grid=(n // tm,),
        in_specs=[tile(D_MODEL), _mod_spec(1, tokens_per_row // tm), _resident(w.shape), _resident((2, LANES))],
        out_specs=[tile(width) for _, width, _ in _REC_COLS],
        compiler_params=_params(("parallel",)),
        name="rec_in_proj",
    )(x2, mod4, w, gate_b)


def _scan_specs(ncc, ncl, width, rows, nb):
    ctx_f = pl.BlockSpec((nb, rows, width), lambda b, s: (b, jnp.minimum(s, ncc - 1), 0))
    ctx_b = pl.BlockSpec((nb, rows, width), lambda b, s: (b, jnp.maximum(ncc - 1 - s, 0), 0))
    lat_f = pl.BlockSpec((nb, rows, width), lambda b, s: (b, jnp.maximum(s - ncc, 0), 0))
    lat_b = pl.BlockSpec((nb, rows, width), lambda b, s: (b, jnp.minimum(ncl - 1 - (s - ncc), ncl - 1), 0))
    return ctx_f, ctx_b, lat_f, lat_b


def _tri_masks():
    r = lax.broadcasted_iota(jnp.int32, (CHUNK, CHUNK), 0)
    c = lax.broadcasted_iota(jnp.int32, (CHUNK, CHUNK), 1)
    return r, c


def _head_cols(ref, h):
    return ref[:, h * LANES:(h + 1) * LANES]


def _mlstm_step(srcs, masks, tri, c_ref, m_ref, with_output):
    units = [(d, h) for d in range(len(srcs)) for h in range(MLSTM_HEADS)]
    dv = MLSTM_HEAD_DIM
    ones = jnp.ones((CHUNK, LANES), BF16)
    v_ext = {(d, h): jnp.concatenate([_head_cols(srcs[d][2], h), ones], axis=1) for d, h in units}
    gate = []
    for d in range(len(srcs)):
        gi, gf = srcs[d][3][...], srcs[d][4][...]
        cum = jnp.dot(tri[d % 2], gf, preferred_element_type=F32, precision=lax.Precision.HIGHEST)
        b_end = cum[CHUNK - 1:CHUNK, :] if d % 2 == 0 else cum[0:1, :]
        m_prev = m_ref[d:d + 1, :]
        w_end = b_end - cum + gi
        m_new = jnp.maximum(b_end + m_prev, jnp.max(w_end, axis=0, keepdims=True))
        decay = jnp.exp(b_end + m_prev - m_new)
        w = jnp.exp(w_end - m_new)
        m_ref[d:d + 1, :] = m_new
        log_inter = cum + m_prev if with_output else None
        key_term = (gi - cum).T if with_output else None
        gate.append((cum, decay, w, log_inter, key_term))
    if with_output:
        qk, qcn, log_d, li, rmax, m_t, sm, w_inter, pv = {}, {}, {}, {}, {}, {}, {}, {}, {}
        for d, h in units:
            q = _head_cols(srcs[d][0], h)
            qk[d, h] = _dot_nt(q, _head_cols(srcs[d][1], h))
            qcn[d, h] = _dot(q, c_ref[d * MLSTM_HEADS + h].astype(BF16))
        for d, h in units:
            lane = (d % 2) * MLSTM_HEADS + h
            cum, _, _, log_inter, key_term = gate[d]
            log_d[d, h] = jnp.where(masks[d % 2], cum[:, lane:lane + 1] + key_term[lane:lane + 1, :], NEG_BIG)
            li[d, h] = log_inter[:, lane:lane + 1]
        for u in units:
            rmax[u] = jnp.max(log_d[u], axis=1, keepdims=True)
        for u in units:
            m_t[u] = jnp.maximum(li[u], rmax[u])
            sm[u] = (qk[u] * jnp.exp(log_d[u] - m_t[u])).astype(BF16)
            w_inter[u] = jnp.exp(li[u] - m_t[u])
        for u in units:
            pv[u] = _dot(sm[u], v_ext[u])
        for d, h in units:
            u = (d, h)
            both = pv[u] + w_inter[u] * qcn[u]
            den = jnp.maximum(jnp.abs(both[:, dv:]), jnp.exp(-m_t[u]))
            srcs[d][5][:, h * LANES:(h + 1) * LANES] = (both[:, :dv] * (1.0 / den)).astype(BF16)
    for d, h in units:
        row, lane = d * MLSTM_HEADS + h, (d % 2) * MLSTM_HEADS + h
        _, decay, w, _, _ = gate[d]
        kw = (_head_cols(srcs[d][1], h).astype(F32) * w[:, lane:lane + 1]).astype(BF16)
        c_ref[row] = decay[:, lane:lane + 1] * c_ref[row] + _dot_tn(kw, v_ext[d, h])


def _ret_tables(logit_ref, intra_ref, to_end_ref, from_start_ref):
    log_gamma = _log_sigmoid(logit_ref[...])
    r, c = _tri_masks()
    rf, cf = r.astype(F32), c.astype(F32)
    for d in range(2):
        for h in range(RET_HEADS):
            row = d * RET_HEADS + h
            lg = log_gamma[d:d + 1, h:h + 1]
            rel = (rf - cf) if d == 0 else (cf - rf)
            intra_ref[row] = jnp.where(rel >= 0.0, jnp.exp(lg * jnp.maximum(rel, 0.0)), 0.0)
            to_end_ref[row] = jnp.exp(lg * ((CHUNK - 1.0 - rf) if d == 0 else rf))
            from_start_ref[row] = jnp.exp(lg * ((rf + 1.0) if d == 0 else (CHUNK - rf)))


def _ret_step(srcs, s_ref, intra_ref, to_end_ref, from_start_ref, with_output):
    units = [(d, h) for d in range(len(srcs)) for h in range(RET_HEADS)]
    lane = lax.broadcasted_iota(jnp.int32, (CHUNK, LANES), 1)

    def k_head(d, h):
        pair = _head_cols(srcs[d][1], h // 2)
        keep = (lane < RET_QK_DIM) if h % 2 == 0 else (lane >= RET_QK_DIM)
        return jnp.where(keep, pair, jnp.zeros_like(pair))

    ks = {u: k_head(*u) for u in units}
    qk, qs = {}, {}
    if with_output:
        for d, h in units:
            q = _head_cols(srcs[d][0], h // 2)
            qk[d, h] = _dot_nt(q, ks[d, h])
            qs[d, h] = _dot(q, s_ref[d * RET_HEADS + h].astype(BF16))
        for d, h in units:
            tab = (d % 2) * RET_HEADS + h
            sc = (qk[d, h] * intra_ref[tab]).astype(BF16)
            out = _dot(sc, _head_cols(srcs[d][2], h)) + from_start_ref[tab] * qs[d, h]
            srcs[d][3][:, h * LANES:(h + 1) * LANES] = out.astype(BF16)
    for d, h in units:
        row, tab = d * RET_HEADS + h, (d % 2) * RET_HEADS + h
        kd = (ks[d, h].astype(F32) * to_end_ref[tab]).astype(BF16)
        chunk_decay = from_start_ref[tab, CHUNK - 1:CHUNK, :] if d % 2 == 0 else from_start_ref[tab, 0:1, :]
        s_ref[row] = chunk_decay * s_ref[row] + _dot_tn(kd, _head_cols(srcs[d][2], h))


def _rec_scan_kernel(mkc_f, mvc_f, gic_f, gfc_f, mkc_b, mvc_b, gic_b, gfc_b,
                     mq_f, mk_f, mv_f, gi_f, gf_f, mq_b, mk_b, mv_b, gi_b, gf_b,
                     rkc_f, rvc_f, rkc_b, rvc_b, rq_f, rk_f, rv_f, rq_b, rk_b, rv_b, logit_ref,
                     mo_f, mo_b, ro_f, ro_b,
                     c_ref, m_ref, s_ref, intra_ref, to_end_ref, from_start_ref, *, ncc):
    s = pl.program_id(1)

    @pl.when(s == 0)
    def _():
        c_ref[...] = jnp.zeros_like(c_ref)
        m_ref[...] = jnp.zeros_like(m_ref)
        s_ref[...] = jnp.zeros_like(s_ref)
        _ret_tables(logit_ref, intra_ref, to_end_ref, from_start_ref)

    r, c = _tri_masks()
    masks = (c <= r, c >= r)
    tri = (jnp.where(masks[0], 1.0, 0.0), jnp.where(masks[1], 1.0, 0.0))

    def run(m_f, m_b, r_f, r_b, with_output):
        n_batch = mkc_f.shape[0]
        for sub in range(SCAN_CHUNKS):
            lo = (sub * CHUNK, (SCAN_CHUNKS - 1 - sub) * CHUNK)
            view = lambda refs, bb, d: tuple(None if ref is None else _Rows(ref, bb, lo[d]) for ref in refs)
            m_srcs = [view(refs, bb, d) for bb in range(n_batch) for d, refs in enumerate((m_f, m_b))]
            r_srcs = [view(refs, bb, d) for bb in range(n_batch) for d, refs in enumerate((r_f, r_b))]
            _mlstm_step(m_srcs, masks, tri, c_ref, m_ref, with_output)
            _ret_step(r_srcs, s_ref, intra_ref, to_end_ref, from_start_ref, with_output)

    @pl.when(s < ncc)
    def _():
        run((None, mkc_f, mvc_f, gic_f, gfc_f, None), (None, mkc_b, mvc_b, gic_b, gfc_b, None),
            (None, rkc_f, rvc_f, None), (None, rkc_b, rvc_b, None), False)

    @pl.when(s >= ncc)
    def _():
        run((mq_f, mk_f, mv_f, gi_f, gf_f, mo_f), (mq_b, mk_b, mv_b, gi_b, gf_b, mo_b),
            (rq_f, rk_f, rv_f, ro_f), (rq_b, rk_b, rv_b, ro_b), True)


class _Rows:
    def __init__(self, ref, bb, lo):
        self.ref, self.bb, self.rows = ref, bb, slice(lo, lo + CHUNK)

    def __getitem__(self, idx):
        return self.ref[self.bb, self.rows, :] if idx is Ellipsis else self.ref[self.bb, self.rows, idx[1]]

    def __setitem__(self, idx, value):
        self.ref[self.bb, self.rows, idx[1]] = value


def _rec_scan(mq, mk, mv, gi, gf, rq, rk, rv, mkc, mvc, gic, gfc, rkc, rvc, logit):
    bsz, t, width = mq.shape
    rows = CHUNK * SCAN_CHUNKS
    assert t % rows == 0 and mkc.shape[1] % rows == 0
    ncl, ncc = t // rows, mkc.shape[1] // rows
    nb = SCAN_BATCH if bsz % SCAN_BATCH == 0 else 1
    cf, cb, lf, lb = _scan_specs(ncc, ncl, width, rows, nb)
    gcf, gcb, glf, glb = _scan_specs(ncc, ncl, LANES, rows, nb)
    kcf, kcb, klf, klb = _scan_specs(ncc, ncl, rq.shape[2], rows, nb)
    return pl.pallas_call(
        functools.partial(_rec_scan_kernel, ncc=ncc),
        out_shape=[jax.ShapeDtypeStruct((bsz, t, width), BF16)] * 4,
        grid=(bsz // nb, ncc + ncl),
        in_specs=[cf, cf, gcf, gcf, cb, cb, gcb, gcb, lf, lf, lf, glf, glf, lb, lb, lb, glb, glb,
                  kcf, cf, kcb, cb, klf, klf, lf, klb, klb, lb, pl.BlockSpec((8, LANES), lambda b, s: (0, 0))],
        out_specs=[lf, lb, lf, lb],
        scratch_shapes=[pltpu.VMEM((nb * _N_DIR_HEADS, MLSTM_HEAD_DIM, 2 * MLSTM_HEAD_DIM), F32),
                        pltpu.VMEM((8, LANES), F32),
                        pltpu.VMEM((nb * 2 * RET_HEADS, LANES, RET_V_DIM), F32)]
        + [pltpu.VMEM((2 * RET_HEADS, CHUNK, LANES), F32)] * 3,
        compiler_params=_params(("parallel", "arbitrary")),
        name="rec_scan",
    )(mkc, mvc, gic, gfc, mkc, mvc, gic, gfc, mq, mk, mv, gi, gf, mq, mk, mv, gi, gf,
      rkc, rvc, rkc, rvc, rq, rk, rv, rq, rk, rv, logit)


def _rec_out_kernel(x_ref, mod_ref, mf_ref, mb_ref, rf_ref, rb_ref, o_ref_in, rg_ref, mg_ref, rng_ref, rnb_ref,
                    w_ref, g_ref, b_ref, out_ref):
    hm = mf_ref[...].astype(F32) + mb_ref[...].astype(F32)
    hr = rf_ref[...].astype(F32) + rb_ref[...].astype(F32)
    o_gate = jax.nn.sigmoid(o_ref_in[...].astype(F32))
    rg = rg_ref[...].astype(F32)
    r_gate = rg * jax.nn.sigmoid(rg)
    parts_m, parts_r = [], []
    for h in range(MLSTM_HEADS):
        cols = slice(h * LANES, (h + 1) * LANES)
        parts_m.append(_layer_norm(hm[:, cols], mg_ref[:, cols], None) * o_gate[:, cols])
        parts_r.append(_layer_norm(hr[:, cols], rng_ref[:, cols], rnb_ref[:, cols]) * r_gate[:, cols])
    width = MLSTM_HEADS * MLSTM_HEAD_DIM
    y = (_dot(jnp.concatenate(parts_m, axis=1).astype(BF16), w_ref[0:width, :])
         + _dot(jnp.concatenate(parts_r, axis=1).astype(BF16), w_ref[width:, :]))
    z = ALPHA * x_ref[...] + mod_ref[2:3, :] * y
    out_ref[...] = _layer_norm(z, g_ref[...], b_ref[...])


def _rec_out(x2, mod4, tokens_per_row, mf, mb, rf, rb, o, rg, mg, rng, rnb, w, g, b):
    n = x2.shape[0]
    tm = _token_tile(n, tokens_per_row)
    tile = lambda width: pl.BlockSpec((tm, width), lambda i: (i, 0))
    half = tile(512)
    return pl.pallas_call(
        _rec_out_kernel,
        out_shape=jax.ShapeDtypeStruct((n, D_MODEL), F32),
        grid=(n // tm,),
        in_specs=[tile(D_MODEL), _mod_spec(1, tokens_per_row // tm), half, half, half, half, half, half,
                  _resident((1, 512)), _resident((1, 512)), _resident((1, 512)),
                  _resident(w.shape), _resident((1, D_MODEL)), _resident((1, D_MODEL))],
        out_specs=tile(D_MODEL),
        compiler_params=_params(("parallel",)),
        name="rec_out_proj",
    )(x2, mod4, mf, mb, rf, rb, o, rg, mg, rng, rnb, w, g, b)


def _attn_weights(w_in, w_out):
    aq, ak, av, bq, bk, bv = jnp.split(w_in, [512, 1024, 1536, 2048, 2176], axis=1)
    bk_swapped = jnp.concatenate([bk[:, WIN_HEAD_DIM:], bk[:, :WIN_HEAD_DIM]], axis=1)
    w = jnp.concatenate([aq, ak, av, bq, bk, bk_swapped, bv], axis=1)
    wa, wb = w_out[:DIFF_HEADS * DIFF_V_DIM], w_out[DIFF_HEADS * DIFF_V_DIM:]
    return w.astype(BF16), wa.astype(BF16), wb.astype(BF16)


def _rec_weights(w_in):
    mq, mk, mv, mo, mg, rq, rk, rv, rg = jnp.split(w_in, [512, 1024, 1536, 2048, 2064, 2320, 2576, 3088], axis=1)
    gi, gf = _split_gates(mg)
    w = jnp.concatenate([mq, mk, mv, mo, rq, rk, rv, rg, gi, gf], axis=1)
    return w.astype(BF16)


def _split_gates(g):
    g = g.reshape(*g.shape[:-1], 2, 2, MLSTM_HEADS)
    pad = [(0, 0)] * (g.ndim - 3) + [(0, LANES - _N_DIR_HEADS)]
    return tuple(jnp.pad(g[..., io, :].reshape(*g.shape[:-3], _N_DIR_HEADS), pad) for io in range(2))


def _rope_tables(t):
    rows = t // GRID_W
    row = jnp.repeat(jnp.arange(rows), GRID_W)
    col = jnp.tile(jnp.arange(GRID_W), rows)
    inv = ROPE_BASE ** (-jnp.arange(ROPE_FREQS, dtype=F32) / ROPE_FREQS)
    ang_r, ang_c = row[:, None] * inv, col[:, None] * inv
    cos = jnp.concatenate([jnp.cos(ang_r)] * 2 + [jnp.cos(ang_c)] * 2, axis=1)
    sin = jnp.concatenate([-jnp.sin(ang_r), jnp.sin(ang_r), -jnp.sin(ang_c), jnp.sin(ang_c)], axis=1)
    return jnp.tile(cos, (1, 2)), jnp.tile(sin, (1, 2))


def kernel(x, c, ctx, c_ctx, ada_w, ada_b, ln_g, ln_b, ffn_w_in, ffn_w_out, attn_w_in, attn_w_out,
           diff_lambda, diff_norm_g, sink_logits, rec_w_in, rec_w_out, mlstm_gate_b, mlstm_norm_g,
           ret_decay_logit, ret_norm_g, ret_norm_b):
    bsz, t, d = x.shape
    lc = ctx.shape[1]
    assert d == D_MODEL and t % ATTN_TK == 0 and lc % CHUNK == 0
    out_dtype = x.dtype
    x2 = x.reshape(bsz * t, d).astype(F32)
    c2 = ctx.reshape(bsz * lc, d).astype(F32)
    mod_rows = 16
    cc = jnp.zeros((mod_rows, d), F32).at[:bsz].set(c).at[bsz].set(c_ctx)
    row2 = lambda v: v.reshape(1, -1).astype(F32)
    ada_b3 = ada_b.reshape(DEPTH, 1, -1)
    ffn_w_in16, ffn_w_out16 = ffn_w_in.astype(BF16), ffn_w_out.astype(BF16)

    for l in range(DEPTH):
        last = l == DEPTH - 1
        i = l // 2
        mod = _modulation(cc, ada_w, ada_b3, l).reshape(mod_rows, 3, 3, d)
        mod_x, mod_c = mod[:bsz], mod[bsz:bsz + 1]
        ffn = lambda z, m, sub, tpr, j: _ffn(z, m, sub, tpr, ffn_w_in16, ffn_w_out16, l, j,
                                             row2(ln_g[l, sub]), row2(ln_b[l, sub]))
        x2 = ffn(x2, mod_x, 0, t, 0)
        c2 = ffn(c2, mod_c, 0, bsz * lc, 0)
        g1, b1 = row2(ln_g[l, 1]), row2(ln_b[l, 1])
        if l % 2 == 0:
            lambda_init = 0.8 - 0.6 * math.exp(-0.3 * l)
            w, wa, wb = _attn_weights(attn_w_in[i], attn_w_out[i])
            aq, ak, av, bq, bk, bv = [a.reshape(bsz, t, -1) for a in _attn_proj(x2, mod_x, t, w, _rope_tables(t))]
            aqc, akc, avc, bqc, bkc, bvc = [a.reshape(bsz, lc, -1) for a in _attn_proj(c2, mod_c, bsz * lc, w, None)]
            avt, avct = jnp.swapaxes(av, 1, 2), jnp.swapaxes(avc, 1, 2)
            bvt, bvct = jnp.swapaxes(bv, 1, 2), jnp.swapaxes(bvc, 1, 2)
            lam_vec, sub_g = diff_lambda[i].astype(F32), row2(diff_norm_g[i])
            sink = sink_logits[i].astype(F32)
            a_x = _diff_attn(aq, ak, avt, akc, avct, lam_vec, sub_g, lambda_init)
            b_x = _win_attn(bq, bk, bvt, bkc, bvct, sink, True)
            flat = lambda a: a.reshape(-1, a.shape[-1])
            if not last:
                a_c = _diff_attn(aqc, None, None, akc, avct, lam_vec, sub_g, lambda_init)
                b_c = _win_attn(bqc, None, None, bkc, bvct, sink, False)
                c2_mix = _attn_out(c2, mod_c, bsz * lc, flat(a_c), flat(b_c), wa, wb, g1, b1)
            x2 = _attn_out(x2, mod_x, t, flat(a_x), flat(b_x), wa, wb, g1, b1)
        else:
            w = _rec_weights(rec_w_in[i])
            gate_b = jnp.stack(_split_gates(mlstm_gate_b[i].reshape(-1).astype(F32)))
            px = [a.reshape(bsz, t, -1) for a in _rec_proj(x2, mod_x, t, w, gate_b)]
            pc = [a.reshape(bsz, lc, -1) for a in _rec_proj(c2, mod_c, bsz * lc, w, gate_b)]
            mq, mk, mv, mo, rq, rk, rv, rg, gi, gf = px
            _, mkc, mvc, _, _, rkc, rvc, _, gic, gfc = pc
            assert last
            logit = jnp.zeros((8, LANES), F32).at[:2, :RET_HEADS].set(ret_decay_logit[i].astype(F32))
            m_f, m_b, r_f, r_b = _rec_scan(mq, mk, mv, gi, gf, rq, rk, rv, mkc, mvc, gic, gfc, rkc, rvc, logit)
            flat = lambda a: a.reshape(-1, a.shape[-1])
            x2 = _rec_out(x2, mod_x, t, flat(m_f), flat(m_b), flat(r_f), flat(r_b), flat(mo), flat(rg),
                          row2(mlstm_norm_g[i]), row2(ret_norm_g[i]), row2(ret_norm_b[i]),
                          rec_w_out[i].astype(BF16), g1, b1)
        x2 = ffn(x2, mod_x, 2, t, 1)
        if not last:
            c2 = ffn(c2_mix, mod_c, 2, bsz * lc, 1)
    return x2.reshape(bsz, t, d).astype(out_dtype)
```

```python
import functools
import math

import jax
import jax.numpy as jnp
from jax import lax
from jax.experimental import pallas as pl
from jax.experimental.pallas import tpu as pltpu

F32 = jnp.float32
BF16 = jnp.bfloat16

D_MODEL = 1024
DEPTH = 2
GRID_W = 64
D_FF = 2816
DIFF_HEADS = 4
DIFF_HEAD_DIM = 64
DIFF_V_DIM = 128
WIN_HEADS = 8
WIN_KV_HEADS = 2
WIN_GROUP = WIN_HEADS // WIN_KV_HEADS
WIN_HEAD_DIM = 64
WINDOW = 128
Q_BLOCK = 128
ROPE_BASE = 10000.0
ROPE_FREQS = 16
MLSTM_HEADS = 4
MLSTM_HEAD_DIM = 128
RET_HEADS = 4
RET_QK_DIM = 64
RET_V_DIM = 128
CHUNK = 128
ALPHA = (2.0 * DEPTH) ** 0.25
EPS = 1e-5
LOG2_E = 1.4426950408889634
NEG_BIG = -1e30

LANES = 128
BF16_SUBLANES = 16
V7X_VMEM_LIMIT_BYTES = 56 * 1024 * 1024

TOKEN_TILE = 1024
FFN_TILE = 256
FF_CHUNK = 256
FFN_SUB_TILES = 4
ATTN_TQ = 512
ATTN_TK = 1024
ATTN_PIECE = 256
ATTN_HEADS_PER_STEP = 4
WIN_TQ = 256
SCAN_CHUNKS = 1
SCAN_BATCH = 4


def _params(semantics):
    return pltpu.CompilerParams(dimension_semantics=semantics, vmem_limit_bytes=V7X_VMEM_LIMIT_BYTES)


def _resident(shape):
    return pl.BlockSpec(shape, lambda *_: (0,) * len(shape), pipeline_mode=pl.Buffered(1))


def _layer_norm(y, g, b):
    mu = jnp.mean(y, axis=-1, keepdims=True)
    yc = y - mu
    var = jnp.mean(yc * yc, axis=-1, keepdims=True)
    out = yc * lax.rsqrt(var + EPS) * g
    return out if b is None else out + b


def _log_sigmoid(x):
    return jnp.minimum(x, 0.0) - jnp.log1p(jnp.exp(-jnp.abs(x)))


def _dot(a, b):
    return jnp.dot(a, b, preferred_element_type=F32)


def _dot_nt(a, b):
    return lax.dot_general(a, b, (((1,), (1,)), ((), ())), preferred_element_type=F32)


def _dot_tn(a, b):
    return lax.dot_general(a, b, (((0,), (0,)), ((), ())), preferred_element_type=F32)


def _split_bf16(a):
    hi = a.astype(BF16)
    return hi, (a - hi.astype(F32)).astype(BF16)


def _mod_kernel(c_ref, w_ref, b_ref, o_ref):
    c = c_ref[...]
    h_hi, h_lo = _split_bf16(c * jax.nn.sigmoid(c))
    w_hi, w_lo = _split_bf16(w_ref[...])
    o_ref[...] = _dot(h_hi, w_hi) + (_dot(h_hi, w_lo) + _dot(h_lo, w_hi) + _dot(h_lo, w_lo)) + b_ref[...]


def _modulation(cc, w, b, layer):
    rows, d = cc.shape
    n = w.shape[2]
    tn = 1024
    return pl.pallas_call(
        _mod_kernel,
        out_shape=jax.ShapeDtypeStruct((rows, n), F32),
        grid=(n // tn,),
        in_specs=[pl.BlockSpec((rows, d), lambda j: (0, 0)),
                  pl.BlockSpec((None, d, tn), lambda j: (layer, 0, j)),
                  pl.BlockSpec((None, 1, tn), lambda j: (layer, 0, j))],
        out_specs=pl.BlockSpec((rows, tn), lambda j: (0, j)),
        compiler_params=_params(("parallel",)),
        name="modulation",
    )(cc, w, b)


def _token_tile(n_tokens, tokens_per_row, tile=None):
    tm = min(tile or TOKEN_TILE, tokens_per_row)
    assert tokens_per_row % tm == 0 and n_tokens % tm == 0
    return tm


def _mod_spec(sub, tiles_per_row):
    return pl.BlockSpec((None, None, 3, D_MODEL), lambda i: (i // tiles_per_row, sub, 0, 0))


def _ffn_kernel(x_ref, mod_ref, w_in_ref, w_out_ref, g_ref, b_ref, o_ref, act_ref, *, n_sub):
    shift, scale, gate = mod_ref[0:1, :], mod_ref[1:2, :], mod_ref[2:3, :]
    rows = x_ref.shape[0] // n_sub

    def finish(s):
        x = x_ref[s * rows:(s + 1) * rows, :]
        y = ALPHA * x + 0.5 * gate * _dot(act_ref[s], w_out_ref[...])
        o_ref[s * rows:(s + 1) * rows, :] = _layer_norm(y, g_ref[...], b_ref[...])

    for s in range(n_sub):
        h = (x_ref[s * rows:(s + 1) * rows, :] * (1.0 + scale) + shift).astype(BF16)
        for c in range(D_FF // FF_CHUNK):
            lo = c * FF_CHUNK
            gt = _dot(h, w_in_ref[:, lo:lo + FF_CHUNK])
            up = _dot(h, w_in_ref[:, D_FF + lo:D_FF + lo + FF_CHUNK])
            act_ref[s, :, lo:lo + FF_CHUNK] = (gt * jax.nn.sigmoid(gt) * up).astype(BF16)
            if c == 0 and s > 0:
                finish(s - 1)
    finish(n_sub - 1)


def _ffn(x2, mod4, sub, tokens_per_row, w_in, w_out, layer, which, g, b):
    n = x2.shape[0]
    tm = _token_tile(n, tokens_per_row, FFN_TILE)
    n_sub = FFN_SUB_TILES if (tokens_per_row % (FFN_SUB_TILES * tm) == 0) else 1
    tb = tm * n_sub
    tile = pl.BlockSpec((tb, D_MODEL), lambda i: (i, 0))
    weight = lambda shape: pl.BlockSpec((None, None) + shape, lambda i: (layer, which, 0, 0),
                                        pipeline_mode=pl.Buffered(1))
    return pl.pallas_call(
        functools.partial(_ffn_kernel, n_sub=n_sub),
        out_shape=jax.ShapeDtypeStruct((n, D_MODEL), F32),
        grid=(n // tb,),
        in_specs=[tile, _mod_spec(sub, tokens_per_row // tb),
                  weight((D_MODEL, 2 * D_FF)), weight((D_FF, D_MODEL)),
                  _resident((1, D_MODEL)), _resident((1, D_MODEL))],
        out_specs=tile,
        scratch_shapes=[pltpu.VMEM((n_sub, tm, D_FF), BF16)],
        compiler_params=_params(("parallel",)),
        name="ffn_sublayer",
    )(x2, mod4, w_in, w_out, g, b)


def _rope(p, cos, sin):
    lane = lax.broadcasted_iota(jnp.int32, (p.shape[0], LANES), 1)
    first_half = (lane & ROPE_FREQS) == 0
    outs = []
    for j in range(p.shape[1] // LANES):
        xs = p[:, j * LANES:(j + 1) * LANES]
        partner = jnp.where(first_half, pltpu.roll(xs, LANES - ROPE_FREQS, 1), pltpu.roll(xs, ROPE_FREQS, 1))
        outs.append(xs * cos + partner * sin)
    return jnp.concatenate(outs, axis=1)


_ATTN_COLS = (("aq", 512), ("ak", 512), ("av", 512), ("bq", 512), ("bk", 256), ("bv", 128))
_ATTN_GROUPS = (("aq",), ("ak",), ("av",), ("bq",), ("bk", "bv"))


def _attn_proj_kernel(*refs, rope):
    if rope:
        x_ref, mod_ref, w_ref, cos_ref, sin_ref = refs[:5]
        outs = refs[5:]
        cos, sin = cos_ref[...], sin_ref[...]
    else:
        x_ref, mod_ref, w_ref = refs[:3]
        outs = refs[3:]
    x = x_ref[...]
    shift, scale = mod_ref[0:1, :], mod_ref[1:2, :]
    h = (x * (1.0 + scale) + shift).astype(BF16)
    widths = dict(_ATTN_COLS)
    out_refs = dict(zip(widths, outs))
    lo = 0
    for group in _ATTN_GROUPS:
        total = sum(widths[name] for name in group)
        pg = _dot(h, w_ref[:, lo:lo + total])
        lo += total
        off = 0
        for name in group:
            p = pg[:, off:off + widths[name]]
            off += widths[name]
            if rope and name in ("aq", "ak", "bq", "bk"):
                p = _rope(p, cos, sin)
            if name in ("aq", "bq"):
                head_dim = DIFF_HEAD_DIM if name == "aq" else WIN_HEAD_DIM
                p = p * (head_dim ** -0.5 * LOG2_E)
            out_refs[name][...] = p.astype(BF16)


def _attn_proj(x2, mod4, tokens_per_row, w, tables):
    n = x2.shape[0]
    tm = _token_tile(n, tokens_per_row)
    rope = tables is not None
    tile = lambda width: pl.BlockSpec((tm, width), lambda i: (i, 0))
    in_specs = [tile(D_MODEL), _mod_spec(1, tokens_per_row // tm), _resident(w.shape)]
    args = [x2, mod4, w]
    if rope:
        tpr = tokens_per_row // tm
        tab = pl.BlockSpec((tm, LANES), lambda i: (i % tpr, 0))
        in_specs += [tab, tab]
        args += list(tables)
    return pl.pallas_call(
        functools.partial(_attn_proj_kernel, rope=rope),
        out_shape=[jax.ShapeDtypeStruct((n, width), BF16) for _, width in _ATTN_COLS],
        grid=(n // tm,),
        in_specs=in_specs,
        out_specs=[tile(width) for _, width in _ATTN_COLS],
        compiler_params=_params(("parallel",)),
        name="attn_in_proj",
    )(*args)


def _attn_out_kernel(x_ref, mod_ref, a_ref, b_ref, wa_ref, wb_ref, g_ref, bb_ref, o_ref):
    y = _dot(a_ref[...], wa_ref[...]) + _dot(b_ref[...], wb_ref[...])
    z = ALPHA * x_ref[...] + mod_ref[2:3, :] * y
    o_ref[...] = _layer_norm(z, g_ref[...], bb_ref[...])


def _attn_out(x2, mod4, tokens_per_row, a, b, wa, wb, g, bb):
    n = x2.shape[0]
    tm = _token_tile(n, tokens_per_row)
    tile = lambda width: pl.BlockSpec((tm, width), lambda i: (i, 0))
    return pl.pallas_call(
        _attn_out_kernel,
        out_shape=jax.ShapeDtypeStruct((n, D_MODEL), F32),
        grid=(n // tm,),
        in_specs=[tile(D_MODEL), _mod_spec(1, tokens_per_row // tm), tile(a.shape[1]), tile(b.shape[1]),
                  _resident(wa.shape), _resident(wb.shape), _resident((1, D_MODEL)), _resident((1, D_MODEL))],
        out_specs=tile(D_MODEL),
        compiler_params=_params(("parallel",)),
        name="attn_out_proj",
    )(x2, mod4, a, b, wa, wb, g, bb)


def _diff_attn_kernel(*refs, n_lat, lc, lambda_init):
    if n_lat:
        q_ref, k_ref, vt_ref, kc_ref, vct_ref, lam_ref, g_ref, o_ref = refs
    else:
        q_ref, kc_ref, vct_ref, lam_ref, g_ref, o_ref = refs
    t = lam_ref[...]
    lam = (jnp.exp(jnp.sum(t[0:1, :] * t[1:2, :], axis=1, keepdims=True))
           - jnp.exp(jnp.sum(t[2:3, :] * t[3:4, :], axis=1, keepdims=True)) + lambda_init)
    n_heads = q_ref.shape[1] // LANES
    ctx_pieces = [(kc_ref, vct_ref, j * ATTN_PIECE) for j in range(lc // ATTN_PIECE)]
    chunks = [[(k_ref, vt_ref, c * ATTN_TK + j * ATTN_PIECE) for j in range(ATTN_TK // ATTN_PIECE)]
              for c in range(n_lat)] + [ctx_pieces]
    stages = [(hd, idx, idx == len(chunks) - 1, chunk) for hd in range(n_heads) for idx, chunk in enumerate(chunks)]

    def n_pieces(stage):
        return len(stage[3])

    def scores(stage, piece, sub):
        hd, _, _, pieces = stage
        kr, _, lo = pieces[piece]
        q = q_ref[:, hd * LANES:(hd + 1) * LANES]
        kblk = kr[lo:lo + ATTN_PIECE, hd * LANES:(hd + 1) * LANES]
        lane = lax.broadcasted_iota(jnp.int32, kblk.shape, 1)
        keep = (lane < DIFF_HEAD_DIM) if sub == 0 else (lane >= DIFF_HEAD_DIM)
        return _dot_nt(jnp.where(keep, kblk, jnp.zeros_like(kblk)), q)

    def finish(hd, acc):
        num = [a[:DIFF_V_DIM, :] * (1.0 / a[DIFF_V_DIM:DIFF_V_DIM + 1, :]) for a in acc]
        o = (num[0] - lam * num[1]).T
        o = o * lax.rsqrt(jnp.mean(o * o, axis=-1, keepdims=True) + EPS) * g_ref[...]
        o_ref[:, hd * LANES:(hd + 1) * LANES] = (o * (1.0 - lambda_init)).astype(BF16)

    m, acc = [None, None], [None, None]
    s_next = [[scores(stages[0], j, sub) for j in range(n_pieces(stages[0]))] for sub in range(2)]
    for si, stage in enumerate(stages):
        hd, idx, is_last, pieces = stage
        values = jnp.concatenate([vr[hd * LANES:(hd + 1) * LANES, lo:lo + ATTN_PIECE] for _, vr, lo in pieces], axis=1)
        vt = jnp.concatenate([values, jnp.ones((BF16_SUBLANES, values.shape[1]), BF16)], axis=0)
        nxt = stages[si + 1] if si + 1 < len(stages) else None
        for sub in range(2):
            s_cur, s_new = s_next[sub], []
            smax = functools.reduce(jnp.maximum, [jnp.max(s, axis=0, keepdims=True) for s in s_cur])
            m_new = smax if idx == 0 else jnp.maximum(m[sub], smax)
            ps = []
            for j, s in enumerate(s_cur):
                if nxt is not None and j < n_pieces(nxt):
                    s_new.append(scores(nxt, j, sub))
                ps.append(jnp.exp2(s - m_new).astype(BF16))
            if nxt is not None:
                s_new += [scores(nxt, j, sub) for j in range(len(s_new), n_pieces(nxt))]
            part = _dot(vt, jnp.concatenate(ps, axis=0))
            acc[sub] = part if idx == 0 else jnp.exp2(m[sub] - m_new) * acc[sub] + part
            m[sub], s_next[sub] = m_new, s_new
        if is_last:
            finish(hd, acc)


def _diff_attn(q, k, vt, kc, vct, lam_vec, g, lambda_init):
    bsz, tq_total, _ = q.shape
    lc = kc.shape[1]
    tq = min(ATTN_TQ, tq_total)
    n_lat = 0 if k is None else k.shape[1] // ATTN_TK
    assert tq_total % tq == 0 and (k is None or k.shape[1] % ATTN_TK == 0)
    width = ATTN_HEADS_PER_STEP * LANES
    qspec = pl.BlockSpec((None, tq, width), lambda b, h, i: (b, i, h))
    in_specs, args = [qspec], [q]
    if n_lat:
        t = k.shape[1]
        in_specs += [pl.BlockSpec((None, t, width), lambda b, h, i: (b, 0, h)),
                     pl.BlockSpec((None, width, t), lambda b, h, i: (b, h, 0))]
        args += [k, vt]
    in_specs += [pl.BlockSpec((None, lc, width), lambda b, h, i: (b, 0, h)),
                 pl.BlockSpec((None, width, lc), lambda b, h, i: (b, h, 0)),
                 pl.BlockSpec((4, DIFF_HEAD_DIM), lambda b, h, i: (0, 0)),
                 pl.BlockSpec((1, DIFF_V_DIM), lambda b, h, i: (0, 0))]
    args += [kc, vct, lam_vec, g]
    return pl.pallas_call(
        functools.partial(_diff_attn_kernel, n_lat=n_lat, lc=lc, lambda_init=lambda_init),
        out_shape=jax.ShapeDtypeStruct((bsz, tq_total, DIFF_HEADS * DIFF_V_DIM), BF16),
        grid=(bsz, DIFF_HEADS // ATTN_HEADS_PER_STEP, tq_total // tq),
        in_specs=in_specs,
        out_specs=qspec,
        compiler_params=_params(("parallel", "parallel", "parallel")),
        name="diff_attention",
    )(*args)


def _win_attn_kernel(*refs, has_window, lc, tq):
    hd = WIN_HEAD_DIM
    if has_window:
        n_blk = tq // Q_BLOCK + 2
        q_ref = refs[0]
        k_refs, v_refs = refs[1:1 + n_blk], refs[1 + n_blk:1 + 2 * n_blk]
        kc_ref, vc_ref, sink_ref, o_ref = refs[1 + 2 * n_blk:]
        kw = jnp.concatenate([ref[...] for ref in k_refs] + [kc_ref[...]], axis=0)
        vt = jnp.concatenate([ref[...] for ref in v_refs] + [vc_ref[...]], axis=1)
        span = tq + 2 * WINDOW
        nk = span + lc
        i = pl.program_id(1)
        last = pl.num_programs(1) - 1
        c = lax.broadcasted_iota(jnp.int32, (span, tq), 0)
        r = lax.broadcasted_iota(jnp.int32, (span, tq), 1)
        lo_valid = jnp.where(i == 0, WINDOW, 0)
        hi_valid = jnp.where(i == last, WINDOW + tq, span)
        as_u32 = lambda v: lax.bitcast_convert_type(v, jnp.uint32)
        band = as_u32(c - r) <= jnp.uint32(2 * WINDOW)
        inside = as_u32(c - lo_valid) < as_u32(hi_valid - lo_valid)
        win_bias = jnp.where(band, jnp.where(inside, 0.0, NEG_BIG), NEG_BIG)
        bias = jnp.concatenate([win_bias, jnp.zeros((lc, tq), F32)], axis=0)
        bias = jnp.concatenate([bias, bias], axis=1)
    else:
        q_ref, kc_ref, vc_ref, sink_ref, o_ref = refs
        kw, vt = kc_ref[...], vc_ref[...]
        nk = lc
        bias = None
    lane = lax.broadcasted_iota(jnp.int32, (nk, LANES), 1)
    zero = jnp.zeros((nk, LANES), BF16)
    ones = jnp.ones((BF16_SUBLANES, nk), BF16)
    units = [(kh, p) for kh in range(WIN_KV_HEADS) for p in range(2)]
    heads = {(kh, p): [kh * WIN_GROUP + 2 * j + p for j in range(2)] for kh, p in units}

    def scores(kh, p):
        ksrc = kw[:, (0 if p == kh else 1) * LANES:(1 if p == kh else 2) * LANES]
        kmat = jnp.where((lane < hd) if p == 0 else (lane >= hd), ksrc, zero)
        qcat = jnp.concatenate([q_ref[:, (h // 2) * LANES:(h // 2 + 1) * LANES] for h in heads[kh, p]], axis=0)
        return _dot_nt(kmat, qcat)

    out_t = {}
    s_next = scores(*units[0])
    for idx, (kh, p) in enumerate(units):
        s = s_next if bias is None else s_next + bias
        if idx + 1 < len(units):
            s_next = scores(*units[idx + 1])
        sink = jnp.concatenate([jnp.full((1, tq), sink_ref[h] * LOG2_E, F32) for h in heads[kh, p]], axis=1)
        m = jnp.maximum(jnp.max(s, axis=0, keepdims=True), sink)
        vt_ext = jnp.concatenate([vt[kh * hd:(kh + 1) * hd, :], ones], axis=0)
        acc = _dot(vt_ext, jnp.exp2(s - m).astype(BF16))
        denom = acc[hd:hd + 1, :] + jnp.exp2(sink - m)
        out_t[kh, p] = acc[:hd, :] * (1.0 / denom)
    for kh in range(WIN_KV_HEADS):
        for j in range(2):
            pair = kh * 2 + j
            pair_t = jnp.concatenate([out_t[kh, p][:, j * tq:(j + 1) * tq] for p in range(2)], axis=0)
            o_ref[:, pair * LANES:(pair + 1) * LANES] = pair_t.T.astype(BF16)


def _win_attn(q, k, vt, kc, vct, sink, has_window):
    bsz, tq_total, width = q.shape
    lc = kc.shape[1]
    tq = min(WIN_TQ, tq_total)
    assert tq_total % tq == 0 and tq % Q_BLOCK == 0
    nq = tq_total // tq
    per = tq // Q_BLOCK
    n_kblk = tq_total // Q_BLOCK
    qspec = pl.BlockSpec((None, tq, width), lambda b, i: (b, i, 0))
    in_specs, args = [qspec], [q]
    if has_window:
        def kblock(j):
            return lambda b, i: (b, jnp.clip(i * per + j - 1, 0, n_kblk - 1))
        blk = [kblock(j) for j in range(per + 2)]
        in_specs += [pl.BlockSpec((None, Q_BLOCK, 2 * LANES), lambda b, i, f=f: (*f(b, i), 0)) for f in blk]
        in_specs += [pl.BlockSpec((None, LANES, Q_BLOCK), lambda b, i, f=f: (f(b, i)[0], 0, f(b, i)[1])) for f in blk]
        args += [k] * len(blk) + [vt] * len(blk)
    in_specs += [pl.BlockSpec((None, lc, 2 * LANES), lambda b, i: (b, 0, 0)),
                 pl.BlockSpec((None, LANES, lc), lambda b, i: (b, 0, 0)),
                 pl.BlockSpec(memory_space=pltpu.SMEM)]
    args += [kc, vct, sink]
    return pl.pallas_call(
        functools.partial(_win_attn_kernel, has_window=has_window, lc=lc, tq=tq),
        out_shape=jax.ShapeDtypeStruct((bsz, tq_total, width), BF16),
        grid=(bsz, nq),
        in_specs=in_specs,
        out_specs=qspec,
        compiler_params=_params(("parallel", "parallel")),
        name="window_attention",
    )(*args)


_REC_COLS = (("mq", 512, BF16), ("mk", 512, BF16), ("mv", 512, BF16), ("mo", 512, BF16),
             ("rq", 256, BF16), ("rk", 256, BF16), ("rv", 512, BF16), ("rg", 512, BF16),
             ("gi", 128, F32), ("gf", 128, F32))
_N_DIR_HEADS = 2 * MLSTM_HEADS


def _rec_proj_kernel(x_ref, mod_ref, w_ref, gate_b_ref, *outs):
    x = x_ref[...]
    shift, scale = mod_ref[0:1, :], mod_ref[1:2, :]
    h = (x * (1.0 + scale) + shift).astype(BF16)
    lo = 0
    for (name, width, dtype), o_ref in zip(_REC_COLS, outs):
        p = _dot(h, w_ref[:, lo:lo + width])
        lo += width
        if name == "mk":
            p = p * (MLSTM_HEAD_DIM ** -0.5)
        elif name == "rk":
            p = p * (RET_QK_DIM ** -0.5)
        elif name == "gi":
            p = p + gate_b_ref[0:1, :]
        elif name == "gf":
            p = _log_sigmoid(p + gate_b_ref[1:2, :])
        o_ref[...] = p.astype(dtype)


def _rec_proj(x2, mod4, tokens_per_row, w, gate_b):
    n = x2.shape[0]
    tm = _token_tile(n, tokens_per_row)
    tile = lambda width: pl.BlockSpec((tm, width), lambda i: (i, 0))
    return pl.pallas_call(
        _rec_proj_kernel,
        out_shape=[jax.ShapeDtypeStruct((n, width), dtype) for _, width, dtype in _REC_COLS],
        grid=(n // tm,),
        in_specs=[tile(D_MODEL), _mod_spec(1, tokens_per_row // tm), _resident(w.shape), _resident((2, LANES))],
        out_specs=[tile(width) for _, width, _ in _REC_COLS],
        compiler_params=_params(("parallel",)),
        name="rec_in_proj",
    )(x2, mod4, w, gate_b)


def _scan_specs(ncc, ncl, width, rows, nb):
    ctx_f = pl.BlockSpec((nb, rows, width), lambda b, s: (b, jnp.minimum(s, ncc - 1), 0))
    ctx_b = pl.BlockSpec((nb, rows, width), lambda b, s: (b, jnp.maximum(ncc - 1 - s, 0), 0))
    lat_f = pl.BlockSpec((nb, rows, width), lambda b, s: (b, jnp.maximum(s - ncc, 0), 0))
    lat_b = pl.BlockSpec((nb, rows, width), lambda b, s: (b, jnp.minimum(ncl - 1 - (s - ncc), ncl - 1), 0))
    return ctx_f, ctx_b, lat_f, lat_b


def _tri_masks():
    r = lax.broadcasted_iota(jnp.int32, (CHUNK, CHUNK), 0)
    c = lax.broadcasted_iota(jnp.int32, (CHUNK, CHUNK), 1)
    return r, c


def _head_cols(ref, h):
    return ref[:, h * LANES:(h + 1) * LANES]


def _mlstm_step(srcs, masks, tri, c_ref, m_ref, with_output):
    units = [(d, h) for d in range(len(srcs)) for h in range(MLSTM_HEADS)]
    dv = MLSTM_HEAD_DIM
    ones = jnp.ones((CHUNK, LANES), BF16)
    v_ext = {(d, h): jnp.concatenate([_head_cols(srcs[d][2], h), ones], axis=1) for d, h in units}
    gate = []
    for d in range(len(srcs)):
        gi, gf = srcs[d][3][...], srcs[d][4][...]
        g_hi, g_rest = gf.astype(BF16), gf - gf.astype(BF16).astype(F32)
        g_mid, g_lo = _split_bf16(g_rest)
        tri_d = tri[d % 2].astype(BF16)
        cum = _dot(tri_d, g_hi) + (_dot(tri_d, g_mid) + _dot(tri_d, g_lo))
        b_end = cum[CHUNK - 1:CHUNK, :] if d % 2 == 0 else cum[0:1, :]
        m_prev = m_ref[d:d + 1, :]
        w_end = b_end - cum + gi
        m_new = jnp.maximum(b_end + m_prev, jnp.max(w_end, axis=0, keepdims=True))
        decay = jnp.exp(b_end + m_prev - m_new)
        w = jnp.exp(w_end - m_new)
        m_ref[d:d + 1, :] = m_new
        log_inter = cum + m_prev if with_output else None
        key_term = (gi - cum).T if with_output else None
        gate.append((cum, decay, w, log_inter, key_term))
    if with_output:
        qk, qcn, log_d, li, rmax, m_t, sm, w_inter, pv = {}, {}, {}, {}, {}, {}, {}, {}, {}
        for d, h in units:
            q = _head_cols(srcs[d][0], h)
            qk[d, h] = _dot_nt(q, _head_cols(srcs[d][1], h))
            qcn[d, h] = _dot(q, c_ref[d * MLSTM_HEADS + h].astype(BF16))
        for d, h in units:
            lane = (d % 2) * MLSTM_HEADS + h
            cum, _, _, log_inter, key_term = gate[d]
            log_d[d, h] = jnp.where(masks[d % 2], cum[:, lane:lane + 1] + key_term[lane:lane + 1, :], NEG_BIG)
            li[d, h] = log_inter[:, lane:lane + 1]
        for u in units:
            rmax[u] = jnp.max(log_d[u], axis=1, keepdims=True)
        for u in units:
            m_t[u] = jnp.maximum(li[u], rmax[u])
            sm[u] = (qk[u] * jnp.exp(log_d[u] - m_t[u])).astype(BF16)
            w_inter[u] = jnp.exp(li[u] - m_t[u])
        for u in units:
            pv[u] = _dot(sm[u], v_ext[u])
        for d, h in units:
            u = (d, h)
            both = pv[u] + w_inter[u] * qcn[u]
            den = jnp.maximum(jnp.abs(both[:, dv:]), jnp.exp(-m_t[u]))
            srcs[d][5][:, h * LANES:(h + 1) * LANES] = (both[:, :dv] * (1.0 / den)).astype(BF16)
    for d, h in units:
        row, lane = d * MLSTM_HEADS + h, (d % 2) * MLSTM_HEADS + h
        _, decay, w, _, _ = gate[d]
        kw = (_head_cols(srcs[d][1], h).astype(F32) * w[:, lane:lane + 1]).astype(BF16)
        c_ref[row] = decay[:, lane:lane + 1] * c_ref[row] + _dot_tn(kw, v_ext[d, h])


def _ret_tables(logit_ref, intra_ref, to_end_ref, from_start_ref):
    log_gamma = _log_sigmoid(logit_ref[...])
    r, c = _tri_masks()
    rf, cf = r.astype(F32), c.astype(F32)
    for d in range(2):
        for h in range(RET_HEADS):
            row = d * RET_HEADS + h
            lg = log_gamma[d:d + 1, h:h + 1]
            rel = (rf - cf) if d == 0 else (cf - rf)
            intra_ref[row] = jnp.where(rel >= 0.0, jnp.exp(lg * jnp.maximum(rel, 0.0)), 0.0)
            to_end_ref[row] = jnp.exp(lg * ((CHUNK - 1.0 - rf) if d == 0 else rf))
            from_start_ref[row] = jnp.exp(lg * ((rf + 1.0) if d == 0 else (CHUNK - rf)))


def _ret_step(srcs, s_ref, intra_ref, to_end_ref, from_start_ref, with_output):
    units = [(d, h) for d in range(len(srcs)) for h in range(RET_HEADS)]
    lane = lax.broadcasted_iota(jnp.int32, (CHUNK, LANES), 1)

    def k_head(d, h):
        pair = _head_cols(srcs[d][1], h // 2)
        keep = (lane < RET_QK_DIM) if h % 2 == 0 else (lane >= RET_QK_DIM)
        return jnp.where(keep, pair, jnp.zeros_like(pair))

    ks = {u: k_head(*u) for u in units}
    qk, qs = {}, {}
    if with_output:
        for d, h in units:
            q = _head_cols(srcs[d][0], h // 2)
            qk[d, h] = _dot_nt(q, ks[d, h])
            qs[d, h] = _dot(q, s_ref[d * RET_HEADS + h].astype(BF16))
        for d, h in units:
            tab = (d % 2) * RET_HEADS + h
            sc = (qk[d, h] * intra_ref[tab]).astype(BF16)
            out = _dot(sc, _head_cols(srcs[d][2], h)) + from_start_ref[tab] * qs[d, h]
            srcs[d][3][:, h * LANES:(h + 1) * LANES] = out.astype(BF16)
    for d, h in units:
        row, tab = d * RET_HEADS + h, (d % 2) * RET_HEADS + h
        kd = (ks[d, h].astype(F32) * to_end_ref[tab]).astype(BF16)
        chunk_decay = from_start_ref[tab, CHUNK - 1:CHUNK, :] if d % 2 == 0 else from_start_ref[tab, 0:1, :]
        s_ref[row] = chunk_decay * s_ref[row] + _dot_tn(kd, _head_cols(srcs[d][2], h))


def _rec_scan_kernel(mkc_f, mvc_f, gic_f, gfc_f, mkc_b, mvc_b, gic_b, gfc_b,
                     mq_f, mk_f, mv_f, gi_f, gf_f, mq_b, mk_b, mv_b, gi_b, gf_b,
                     rkc_f, rvc_f, rkc_b, rvc_b, rq_f, rk_f, rv_f, rq_b, rk_b, rv_b, logit_ref,
                     mo_f, mo_b, ro_f, ro_b,
                     c_ref, m_ref, s_ref, intra_ref, to_end_ref, from_start_ref, *, ncc):
    s = pl.program_id(1)

    @pl.when(s == 0)
    def _():
        c_ref[...] = jnp.zeros_like(c_ref)
        m_ref[...] = jnp.zeros_like(m_ref)
        s_ref[...] = jnp.zeros_like(s_ref)
        _ret_tables(logit_ref, intra_ref, to_end_ref, from_start_ref)

    r, c = _tri_masks()
    masks = (c <= r, c >= r)
    tri = (jnp.where(masks[0], 1.0, 0.0), jnp.where(masks[1], 1.0, 0.0))

    def run(m_f, m_b, r_f, r_b, with_output):
        n_batch = mkc_f.shape[0]
        for sub in range(SCAN_CHUNKS):
            lo = (sub * CHUNK, (SCAN_CHUNKS - 1 - sub) * CHUNK)
            view = lambda refs, bb, d: tuple(None if ref is None else _Rows(ref, bb, lo[d]) for ref in refs)
            m_srcs = [view(refs, bb, d) for bb in range(n_batch) for d, refs in enumerate((m_f, m_b))]
            r_srcs = [view(refs, bb, d) for bb in range(n_batch) for d, refs in enumerate((r_f, r_b))]
            _mlstm_step(m_srcs, masks, tri, c_ref, m_ref, with_output)
            _ret_step(r_srcs, s_ref, intra_ref, to_end_ref, from_start_ref, with_output)

    @pl.when(s < ncc)
    def _():
        run((None, mkc_f, mvc_f, gic_f, gfc_f, None), (None, mkc_b, mvc_b, gic_b, gfc_b, None),
            (None, rkc_f, rvc_f, None), (None, rkc_b, rvc_b, None), False)

    @pl.when(s >= ncc)
    def _():
        run((mq_f, mk_f, mv_f, gi_f, gf_f, mo_f), (mq_b, mk_b, mv_b, gi_b, gf_b, mo_b),
            (rq_f, rk_f, rv_f, ro_f), (rq_b, rk_b, rv_b, ro_b), True)


class _Rows:
    def __init__(self, ref, bb, lo):
        self.ref, self.bb, self.rows = ref, bb, slice(lo, lo + CHUNK)

    def __getitem__(self, idx):
        return self.ref[self.bb, self.rows, :] if idx is Ellipsis else self.ref[self.bb, self.rows, idx[1]]

    def __setitem__(self, idx, value):
        self.ref[self.bb, self.rows, idx[1]] = value


def _rec_scan(mq, mk, mv, gi, gf, rq, rk, rv, mkc, mvc, gic, gfc, rkc, rvc, logit):
    bsz, t, width = mq.shape
    rows = CHUNK * SCAN_CHUNKS
    assert t % rows == 0 and mkc.shape[1] % rows == 0
    ncl, ncc = t // rows, mkc.shape[1] // rows
    nb = SCAN_BATCH if bsz % SCAN_BATCH == 0 else 1
    cf, cb, lf, lb = _scan_specs(ncc, ncl, width, rows, nb)
    gcf, gcb, glf, glb = _scan_specs(ncc, ncl, LANES, rows, nb)
    kcf, kcb, klf, klb = _scan_specs(ncc, ncl, rq.shape[2], rows, nb)
    return pl.pallas_call(
        functools.partial(_rec_scan_kernel, ncc=ncc),
        out_shape=[jax.ShapeDtypeStruct((bsz, t, width), BF16)] * 4,
        grid=(bsz // nb, ncc + ncl),
        in_specs=[cf, cf, gcf, gcf, cb, cb, gcb, gcb, lf, lf, lf, glf, glf, lb, lb, lb, glb, glb,
                  kcf, cf, kcb, cb, klf, klf, lf, klb, klb, lb, pl.BlockSpec((8, LANES), lambda b, s: (0, 0))],
        out_specs=[lf, lb, lf, lb],
        scratch_shapes=[pltpu.VMEM((nb * _N_DIR_HEADS, MLSTM_HEAD_DIM, 2 * MLSTM_HEAD_DIM), F32),
                        pltpu.VMEM((8, LANES), F32),
                        pltpu.VMEM((nb * 2 * RET_HEADS, LANES, RET_V_DIM), F32)]
        + [pltpu.VMEM((2 * RET_HEADS, CHUNK, LANES), F32)] * 3,
        compiler_params=_params(("parallel", "arbitrary")),
        name="rec_scan",
    )(mkc, mvc, gic, gfc, mkc, mvc, gic, gfc, mq, mk, mv, gi, gf, mq, mk, mv, gi, gf,
      rkc, rvc, rkc, rvc, rq, rk, rv, rq, rk, rv, logit)


def _rec_out_kernel(x_ref, mod_ref, mf_ref, mb_ref, rf_ref, rb_ref, o_ref_in, rg_ref, mg_ref, rng_ref, rnb_ref,
                    w_ref, g_ref, b_ref, out_ref):
    hm = mf_ref[...].astype(F32) + mb_ref[...].astype(F32)
    hr = rf_ref[...].astype(F32) + rb_ref[...].astype(F32)
    o_gate = jax.nn.sigmoid(o_ref_in[...].astype(F32))
    rg = rg_ref[...].astype(F32)
    r_gate = rg * jax.nn.sigmoid(rg)
    parts_m, parts_r = [], []
    for h in range(MLSTM_HEADS):
        cols = slice(h * LANES, (h + 1) * LANES)
        parts_m.append(_layer_norm(hm[:, cols], mg_ref[:, cols], None) * o_gate[:, cols])
        parts_r.append(_layer_norm(hr[:, cols], rng_ref[:, cols], rnb_ref[:, cols]) * r_gate[:, cols])
    width = MLSTM_HEADS * MLSTM_HEAD_DIM
    y = (_dot(jnp.concatenate(parts_m, axis=1).astype(BF16), w_ref[0:width, :])
         + _dot(jnp.concatenate(parts_r, axis=1).astype(BF16), w_ref[width:, :]))
    z = ALPHA * x_ref[...] + mod_ref[2:3, :] * y
    out_ref[...] = _layer_norm(z, g_ref[...], b_ref[...])


def _rec_out(x2, mod4, tokens_per_row, mf, mb, rf, rb, o, rg, mg, rng, rnb, w, g, b):
    n = x2.shape[0]
    tm = _token_tile(n, tokens_per_row)
    tile = lambda width: pl.BlockSpec((tm, width), lambda i: (i, 0))
    half = tile(512)
    return pl.pallas_call(
        _rec_out_kernel,
        out_shape=jax.ShapeDtypeStruct((n, D_MODEL), F32),
        grid=(n // tm,),
        in_specs=[tile(D_MODEL), _mod_spec(1, tokens_per_row // tm), half, half, half, half, half, half,
                  _resident((1, 512)), _resident((1, 512)), _resident((1, 512)),
                  _resident(w.shape), _resident((1, D_MODEL)), _resident((1, D_MODEL))],
        out_specs=tile(D_MODEL),
        compiler_params=_params(("parallel",)),
        name="rec_out_proj",
    )(x2, mod4, mf, mb, rf, rb, o, rg, mg, rng, rnb, w, g, b)


def _attn_weights(w_in, w_out):
    aq, ak, av, bq, bk, bv = jnp.split(w_in, [512, 1024, 1536, 2048, 2176], axis=1)
    bk_swapped = jnp.concatenate([bk[:, WIN_HEAD_DIM:], bk[:, :WIN_HEAD_DIM]], axis=1)
    w = jnp.concatenate([aq, ak, av, bq, bk, bk_swapped, bv], axis=1)
    wa, wb = w_out[:DIFF_HEADS * DIFF_V_DIM], w_out[DIFF_HEADS * DIFF_V_DIM:]
    return w.astype(BF16), wa.astype(BF16), wb.astype(BF16)


def _rec_weights(w_in):
    mq, mk, mv, mo, mg, rq, rk, rv, rg = jnp.split(w_in, [512, 1024, 1536, 2048, 2064, 2320, 2576, 3088], axis=1)
    gi, gf = _split_gates(mg)
    w = jnp.concatenate([mq, mk, mv, mo, rq, rk, rv, rg, gi, gf], axis=1)
    return w.astype(BF16)


def _split_gates(g):
    g = g.reshape(*g.shape[:-1], 2, 2, MLSTM_HEADS)
    pad = [(0, 0)] * (g.ndim - 3) + [(0, LANES - _N_DIR_HEADS)]
    return tuple(jnp.pad(g[..., io, :].reshape(*g.shape[:-3], _N_DIR_HEADS), pad) for io in range(2))


def _rope_tables(t):
    rows = t // GRID_W
    row = jnp.repeat(jnp.arange(rows), GRID_W)
    col = jnp.tile(jnp.arange(GRID_W), rows)
    inv = ROPE_BASE ** (-jnp.arange(ROPE_FREQS, dtype=F32) / ROPE_FREQS)
    ang_r, ang_c = row[:, None] * inv, col[:, None] * inv
    cos = jnp.concatenate([jnp.cos(ang_r)] * 2 + [jnp.cos(ang_c)] * 2, axis=1)
    sin = jnp.concatenate([-jnp.sin(ang_r), jnp.sin(ang_r), -jnp.sin(ang_c), jnp.sin(ang_c)], axis=1)
    return jnp.tile(cos, (1, 2)), jnp.tile(sin, (1, 2))


def kernel(x, c, ctx, c_ctx, ada_w, ada_b, ln_g, ln_b, ffn_w_in, ffn_w_out, attn_w_in, attn_w_out,
           diff_lambda, diff_norm_g, sink_logits, rec_w_in, rec_w_out, mlstm_gate_b, mlstm_norm_g,
           ret_decay_logit, ret_norm_g, ret_norm_b):
    bsz, t, d = x.shape
    lc = ctx.shape[1]
    assert d == D_MODEL and t % ATTN_TK == 0 and lc % CHUNK == 0
    out_dtype = x.dtype
    x2 = x.reshape(bsz * t, d).astype(F32)
    c2 = ctx.reshape(bsz * lc, d).astype(F32)
    mod_rows = 16
    cc = jnp.zeros((mod_rows, d), F32).at[:bsz].set(c).at[bsz].set(c_ctx)
    row2 = lambda v: v.reshape(1, -1).astype(F32)
    ada_b3 = ada_b.reshape(DEPTH, 1, -1)
    ffn_w_in16, ffn_w_out16 = ffn_w_in.astype(BF16), ffn_w_out.astype(BF16)

    for l in range(DEPTH):
        last = l == DEPTH - 1
        i = l // 2
        mod = _modulation(cc, ada_w, ada_b3, l).reshape(mod_rows, 3, 3, d)
        mod_x, mod_c = mod[:bsz], mod[bsz:bsz + 1]
        ffn = lambda z, m, sub, tpr, j: _ffn(z, m, sub, tpr, ffn_w_in16, ffn_w_out16, l, j,
                                             row2(ln_g[l, sub]), row2(ln_b[l, sub]))
        x2 = ffn(x2, mod_x, 0, t, 0)
        c2 = ffn(c2, mod_c, 0, bsz * lc, 0)
        g1, b1 = row2(ln_g[l, 1]), row2(ln_b[l, 1])
        if l % 2 == 0:
            lambda_init = 0.8 - 0.6 * math.exp(-0.3 * l)
            w, wa, wb = _attn_weights(attn_w_in[i], attn_w_out[i])
            aq, ak, av, bq, bk, bv = [a.reshape(bsz, t, -1) for a in _attn_proj(x2, mod_x, t, w, _rope_tables(t))]
            aqc, akc, avc, bqc, bkc, bvc = [a.reshape(bsz, lc, -1) for a in _attn_proj(c2, mod_c, bsz * lc, w, None)]
            avt, avct = jnp.swapaxes(av, 1, 2), jnp.swapaxes(avc, 1, 2)
            bvt, bvct = jnp.swapaxes(bv, 1, 2), jnp.swapaxes(bvc, 1, 2)
            lam_vec, sub_g = diff_lambda[i].astype(F32), row2(diff_norm_g[i])
            sink = sink_logits[i].astype(F32)
            a_x = _diff_attn(aq, ak, avt, akc, avct, lam_vec, sub_g, lambda_init)
            b_x = _win_attn(bq, bk, bvt, bkc, bvct, sink, True)
            flat = lambda a: a.reshape(-1, a.shape[-1])
            if not last:
                a_c = _diff_attn(aqc, None, None, akc, avct, lam_vec, sub_g, lambda_init)
                b_c = _win_attn(bqc, None, None, bkc, bvct, sink, False)
                c2_mix = _attn_out(c2, mod_c, bsz * lc, flat(a_c), flat(b_c), wa, wb, g1, b1)
            x2 = _attn_out(x2, mod_x, t, flat(a_x), flat(b_x), wa, wb, g1, b1)
        else:
            w = _rec_weights(rec_w_in[i])
            gate_b = jnp.stack(_split_gates(mlstm_gate_b[i].reshape(-1).astype(F32)))
            px = [a.reshape(bsz, t, -1) for a in _rec_proj(x2, mod_x, t, w, gate_b)]
            pc = [a.reshape(bsz, lc, -1) for a in _rec_proj(c2, mod_c, bsz * lc, w, gate_b)]
            mq, mk, mv, mo, rq, rk, rv, rg, gi, gf = px
            _, mkc, mvc, _, _, rkc, rvc, _, gic, gfc = pc
            assert last
            logit = jnp.zeros((8, LANES), F32).at[:2, :RET_HEADS].set(ret_decay_logit[i].astype(F32))
            m_f, m_b, r_f, r_b = _rec_scan(mq, mk, mv, gi, gf, rq, rk, rv, mkc, mvc, gic, gfc, rkc, rvc, logit)
            flat = lambda a: a.reshape(-1, a.shape[-1])
            x2 = _rec_out(x2, mod_x, t, flat(m_f), flat(m_b), flat(r_f), flat(r_b), flat(mo), flat(rg),
                          row2(mlstm_norm_g[i]), row2(ret_norm_g[i]), row2(ret_norm_b[i]),
                          rec_w_out[i].astype(BF16), g1, b1)
        x2 = ffn(x2, mod_x, 2, t, 1)
        if not last:
            c2 = ffn(c2_mix, mod_c, 2, bsz * lc, 1)
    return x2.reshape(bsz, t, d).astype(out_dtype)
```

```python
import functools
import math

import jax
import jax.numpy as jnp
from jax import lax
from jax.experimental import pallas as pl
from jax.experimental.pallas import tpu as pltpu

F32 = jnp.float32
BF16 = jnp.bfloat16

D_MODEL = 1024
DEPTH = 2
GRID_W = 64
D_FF = 2816
DIFF_HEADS = 4
DIFF_HEAD_DIM = 64
DIFF_V_DIM = 128
WIN_HEADS = 8
WIN_KV_HEADS = 2
WIN_GROUP = WIN_HEADS // WIN_KV_HEADS
WIN_HEAD_DIM = 64
WINDOW = 128
Q_BLOCK = 128
ROPE_BASE = 10000.0
ROPE_FREQS = 16
MLSTM_HEADS = 4
MLSTM_HEAD_DIM = 128
RET_HEADS = 4
RET_QK_DIM = 64
RET_V_DIM = 128
CHUNK = 128
ALPHA = (2.0 * DEPTH) ** 0.25
EPS = 1e-5
LOG2_E = 1.4426950408889634
NEG_BIG = -1e30

LANES = 128
BF16_SUBLANES = 16
V7X_VMEM_LIMIT_BYTES = 56 * 1024 * 1024

TOKEN_TILE = 1024
FFN_TILE = 256
FF_CHUNK = 256
FFN_SUB_TILES = 4
ATTN_TQ = 512
ATTN_TK = 1024
ATTN_PIECE = 256
ATTN_HEADS_PER_STEP = 4
WIN_TQ = 256
SCAN_CHUNKS = 1
SCAN_BATCH = 4


def _params(semantics):
    return pltpu.CompilerParams(dimension_semantics=semantics, vmem_limit_bytes=V7X_VMEM_LIMIT_BYTES)


def _resident(shape):
    return pl.BlockSpec(shape, lambda *_: (0,) * len(shape), pipeline_mode=pl.Buffered(1))


def _layer_norm(y, g, b):
    mu = jnp.mean(y, axis=-1, keepdims=True)
    yc = y - mu
    var = jnp.mean(yc * yc, axis=-1, keepdims=True)
    out = yc * lax.rsqrt(var + EPS) * g
    return out if b is None else out + b


def _log_sigmoid(x):
    return jnp.minimum(x, 0.0) - jnp.log1p(jnp.exp(-jnp.abs(x)))


def _dot(a, b):
    return jnp.dot(a, b, preferred_element_type=F32)


def _dot_nt(a, b):
    return lax.dot_general(a, b, (((1,), (1,)), ((), ())), preferred_element_type=F32)


def _dot_tn(a, b):
    return lax.dot_general(a, b, (((0,), (0,)), ((), ())), preferred_element_type=F32)


def _split_bf16(a):
    hi = a.astype(BF16)
    return hi, (a - hi.astype(F32)).astype(BF16)


def _mod_kernel(c_ref, w_ref, b_ref, o_ref):
    c = c_ref[...]
    h_hi, h_lo = _split_bf16(c * jax.nn.sigmoid(c))
    w_hi, w_lo = _split_bf16(w_ref[...])
    o_ref[...] = _dot(h_hi, w_hi) + (_dot(h_hi, w_lo) + _dot(h_lo, w_hi) + _dot(h_lo, w_lo)) + b_ref[...]


def _modulation(cc, w, b, layer):
    rows, d = cc.shape
    n = w.shape[2]
    tn = 1024
    return pl.pallas_call(
        _mod_kernel,
        out_shape=jax.ShapeDtypeStruct((rows, n), F32),
        grid=(n // tn,),
        in_specs=[pl.BlockSpec((rows, d), lambda j: (0, 0)),
                  pl.BlockSpec((None, d, tn), lambda j: (layer, 0, j)),
                  pl.BlockSpec((None, 1, tn), lambda j: (layer, 0, j))],
        out_specs=pl.BlockSpec((rows, tn), lambda j: (0, j)),
        compiler_params=_params(("parallel",)),
        name="modulation",
    )(cc, w, b)


def _token_tile(n_tokens, tokens_per_row, tile=None):
    tm = min(tile or TOKEN_TILE, tokens_per_row)
    assert tokens_per_row % tm == 0 and n_tokens % tm == 0
    return tm


def _mod_spec(sub, tiles_per_row):
    return pl.BlockSpec((None, None, 3, D_MODEL), lambda i: (i // tiles_per_row, sub, 0, 0))


def _ffn_kernel(x_ref, mod_ref, w_in_ref, w_out_ref, g_ref, b_ref, o_ref, act_ref, *, n_sub):
    shift, scale, gate = mod_ref[0:1, :], mod_ref[1:2, :], mod_ref[2:3, :]
    rows = x_ref.shape[0] // n_sub

    def finish(s):
        x = x_ref[s * rows:(s + 1) * rows, :]
        y = ALPHA * x + 0.5 * gate * _dot(act_ref[s], w_out_ref[...])
        o_ref[s * rows:(s + 1) * rows, :] = _layer_norm(y, g_ref[...], b_ref[...])

    for s in range(n_sub):
        h = (x_ref[s * rows:(s + 1) * rows, :] * (1.0 + scale) + shift).astype(BF16)
        for c in range(D_FF // FF_CHUNK):
            lo = c * FF_CHUNK
            gt = _dot(h, w_in_ref[:, lo:lo + FF_CHUNK])
            up = _dot(h, w_in_ref[:, D_FF + lo:D_FF + lo + FF_CHUNK])
            act_ref[s, :, lo:lo + FF_CHUNK] = (gt * jax.nn.sigmoid(gt) * up).astype(BF16)
            if c == 0 and s > 0:
                finish(s - 1)
    finish(n_sub - 1)


def _ffn(x2, mod4, sub, tokens_per_row, w_in, w_out, layer, which, g, b):
    n = x2.shape[0]
    tm = _token_tile(n, tokens_per_row, FFN_TILE)
    n_sub = FFN_SUB_TILES if (tokens_per_row % (FFN_SUB_TILES * tm) == 0) else 1
    tb = tm * n_sub
    tile = pl.BlockSpec((tb, D_MODEL), lambda i: (i, 0))
    weight = lambda shape: pl.BlockSpec((None, None) + shape, lambda i: (layer, which, 0, 0),
                                        pipeline_mode=pl.Buffered(1))
    return pl.pallas_call(
        functools.partial(_ffn_kernel, n_sub=n_sub),
        out_shape=jax.ShapeDtypeStruct((n, D_MODEL), F32),
        grid=(n // tb,),
        in_specs=[tile, _mod_spec(sub, tokens_per_row // tb),
                  weight((D_MODEL, 2 * D_FF)), weight((D_FF, D_MODEL)),
                  _resident((1, D_MODEL)), _resident((1, D_MODEL))],
        out_specs=tile,
        scratch_shapes=[pltpu.VMEM((n_sub, tm, D_FF), BF16)],
        compiler_params=pltpu.CompilerParams(dimension_semantics=("parallel",),
                                             vmem_limit_bytes=V7X_VMEM_LIMIT_BYTES,
                                             allow_input_fusion=[False, False, True, True, False, False]),
        name="ffn_sublayer",
    )(x2, mod4, w_in, w_out, g, b)


def _rope(p, cos, sin):
    lane = lax.broadcasted_iota(jnp.int32, (p.shape[0], LANES), 1)
    first_half = (lane & ROPE_FREQS) == 0
    outs = []
    for j in range(p.shape[1] // LANES):
        xs = p[:, j * LANES:(j + 1) * LANES]
        partner = jnp.where(first_half, pltpu.roll(xs, LANES - ROPE_FREQS, 1), pltpu.roll(xs, ROPE_FREQS, 1))
        outs.append(xs * cos + partner * sin)
    return jnp.concatenate(outs, axis=1)


_ATTN_COLS = (("aq", 512), ("ak", 512), ("av", 512), ("bq", 512), ("bk", 256), ("bv", 128))
_ATTN_GROUPS = (("aq",), ("ak",), ("av",), ("bq",), ("bk", "bv"))


def _attn_proj_kernel(*refs, rope):
    if rope:
        x_ref, mod_ref, w_ref, cos_ref, sin_ref = refs[:5]
        outs = refs[5:]
        cos, sin = cos_ref[...], sin_ref[...]
    else:
        x_ref, mod_ref, w_ref = refs[:3]
        outs = refs[3:]
    x = x_ref[...]
    shift, scale = mod_ref[0:1, :], mod_ref[1:2, :]
    h = (x * (1.0 + scale) + shift).astype(BF16)
    widths = dict(_ATTN_COLS)
    out_refs = dict(zip(widths, outs))
    lo = 0
    for group in _ATTN_GROUPS:
        total = sum(widths[name] for name in group)
        pg = _dot(h, w_ref[:, lo:lo + total])
        lo += total
        off = 0
        for name in group:
            p = pg[:, off:off + widths[name]]
            off += widths[name]
            if rope and name in ("aq", "ak", "bq", "bk"):
                p = _rope(p, cos, sin)
            if name in ("aq", "bq"):
                head_dim = DIFF_HEAD_DIM if name == "aq" else WIN_HEAD_DIM
                p = p * (head_dim ** -0.5 * LOG2_E)
            out_refs[name][...] = p.astype(BF16)


def _attn_proj(x2, mod4, tokens_per_row, w, tables):
    n = x2.shape[0]
    tm = _token_tile(n, tokens_per_row)
    rope = tables is not None
    tile = lambda width: pl.BlockSpec((tm, width), lambda i: (i, 0))
    in_specs = [tile(D_MODEL), _mod_spec(1, tokens_per_row // tm), _resident(w.shape)]
    args = [x2, mod4, w]
    if rope:
        tpr = tokens_per_row // tm
        tab = pl.BlockSpec((tm, LANES), lambda i: (i % tpr, 0))
        in_specs += [tab, tab]
        args += list(tables)
    return pl.pallas_call(
        functools.partial(_attn_proj_kernel, rope=rope),
        out_shape=[jax.ShapeDtypeStruct((n, width), BF16) for _, width in _ATTN_COLS],
        grid=(n // tm,),
        in_specs=in_specs,
        out_specs=[tile(width) for _, width in _ATTN_COLS],
        compiler_params=_params(("parallel",)),
        name="attn_in_proj",
    )(*args)


def _attn_out_kernel(x_ref, mod_ref, a_ref, b_ref, wa_ref, wb_ref, g_ref, bb_ref, o_ref):
    y = _dot(a_ref[...], wa_ref[...]) + _dot(b_ref[...], wb_ref[...])
    z = ALPHA * x_ref[...] + mod_ref[2:3, :] * y
    o_ref[...] = _layer_norm(z, g_ref[...], bb_ref[...])


def _attn_out(x2, mod4, tokens_per_row, a, b, wa, wb, g, bb):
    n = x2.shape[0]
    tm = _token_tile(n, tokens_per_row)
    tile = lambda width: pl.BlockSpec((tm, width), lambda i: (i, 0))
    return pl.pallas_call(
        _attn_out_kernel,
        out_shape=jax.ShapeDtypeStruct((n, D_MODEL), F32),
        grid=(n // tm,),
        in_specs=[tile(D_MODEL), _mod_spec(1, tokens_per_row // tm), tile(a.shape[1]), tile(b.shape[1]),
                  _resident(wa.shape), _resident(wb.shape), _resident((1, D_MODEL)), _resident((1, D_MODEL))],
        out_specs=tile(D_MODEL),
        compiler_params=_params(("parallel",)),
        name="attn_out_proj",
    )(x2, mod4, a, b, wa, wb, g, bb)


def _diff_attn_kernel(*refs, n_lat, lc, lambda_init):
    if n_lat:
        q_ref, k_ref, vt_ref, kc_ref, vct_ref, lam_ref, g_ref, o_ref = refs
    else:
        q_ref, kc_ref, vct_ref, lam_ref, g_ref, o_ref = refs
    t = lam_ref[...]
    lam = (jnp.exp(jnp.sum(t[0:1, :] * t[1:2, :], axis=1, keepdims=True))
           - jnp.exp(jnp.sum(t[2:3, :] * t[3:4, :], axis=1, keepdims=True)) + lambda_init)
    n_heads = q_ref.shape[1] // LANES
    ctx_pieces = [(kc_ref, vct_ref, j * ATTN_PIECE) for j in range(lc // ATTN_PIECE)]
    chunks = [[(k_ref, vt_ref, c * ATTN_TK + j * ATTN_PIECE) for j in range(ATTN_TK // ATTN_PIECE)]
              for c in range(n_lat)] + [ctx_pieces]
    stages = [(hd, idx, idx == len(chunks) - 1, chunk) for hd in range(n_heads) for idx, chunk in enumerate(chunks)]

    def n_pieces(stage):
        return len(stage[3])

    def scores(stage, piece, sub):
        hd, _, _, pieces = stage
        kr, _, lo = pieces[piece]
        q = q_ref[:, hd * LANES:(hd + 1) * LANES]
        kblk = kr[lo:lo + ATTN_PIECE, hd * LANES:(hd + 1) * LANES]
        lane = lax.broadcasted_iota(jnp.int32, kblk.shape, 1)
        keep = (lane < DIFF_HEAD_DIM) if sub == 0 else (lane >= DIFF_HEAD_DIM)
        return _dot_nt(jnp.where(keep, kblk, jnp.zeros_like(kblk)), q)

    def finish(hd, acc):
        num = [a[:DIFF_V_DIM, :] * (1.0 / a[DIFF_V_DIM:DIFF_V_DIM + 1, :]) for a in acc]
        o = (num[0] - lam * num[1]).T
        o = o * lax.rsqrt(jnp.mean(o * o, axis=-1, keepdims=True) + EPS) * g_ref[...]
        o_ref[:, hd * LANES:(hd + 1) * LANES] = (o * (1.0 - lambda_init)).astype(BF16)

    m, acc = [None, None], [None, None]
    s_next = [[scores(stages[0], j, sub) for j in range(n_pieces(stages[0]))] for sub in range(2)]
    for si, stage in enumerate(stages):
        hd, idx, is_last, pieces = stage
        values = jnp.concatenate([vr[hd * LANES:(hd + 1) * LANES, lo:lo + ATTN_PIECE] for _, vr, lo in pieces], axis=1)
        vt = jnp.concatenate([values, jnp.ones((BF16_SUBLANES, values.shape[1]), BF16)], axis=0)
        nxt = stages[si + 1] if si + 1 < len(stages) else None
        for sub in range(2):
            s_cur, s_new = s_next[sub], []
            smax = functools.reduce(jnp.maximum, [jnp.max(s, axis=0, keepdims=True) for s in s_cur])
            m_new = smax if idx == 0 else jnp.maximum(m[sub], smax)
            ps = []
            for j, s in enumerate(s_cur):
                if nxt is not None and j < n_pieces(nxt):
                    s_new.append(scores(nxt, j, sub))
                ps.append(jnp.exp2(s - m_new).astype(BF16))
            if nxt is not None:
                s_new += [scores(nxt, j, sub) for j in range(len(s_new), n_pieces(nxt))]
            part = _dot(vt, jnp.concatenate(ps, axis=0))
            acc[sub] = part if idx == 0 else jnp.exp2(m[sub] - m_new) * acc[sub] + part
            m[sub], s_next[sub] = m_new, s_new
        if is_last:
            finish(hd, acc)


def _diff_attn(q, k, vt, kc, vct, lam_vec, g, lambda_init):
    bsz, tq_total, _ = q.shape
    lc = kc.shape[1]
    tq = min(ATTN_TQ, tq_total)
    n_lat = 0 if k is None else k.shape[1] // ATTN_TK
    assert tq_total % tq == 0 and (k is None or k.shape[1] % ATTN_TK == 0)
    width = ATTN_HEADS_PER_STEP * LANES
    qspec = pl.BlockSpec((None, tq, width), lambda b, h, i: (b, i, h))
    in_specs, args = [qspec], [q]
    if n_lat:
        t = k.shape[1]
        in_specs += [pl.BlockSpec((None, t, width), lambda b, h, i: (b, 0, h)),
                     pl.BlockSpec((None, width, t), lambda b, h, i: (b, h, 0))]
        args += [k, vt]
    in_specs += [pl.BlockSpec((None, lc, width), lambda b, h, i: (b, 0, h)),
                 pl.BlockSpec((None, width, lc), lambda b, h, i: (b, h, 0)),
                 pl.BlockSpec((4, DIFF_HEAD_DIM), lambda b, h, i: (0, 0)),
                 pl.BlockSpec((1, DIFF_V_DIM), lambda b, h, i: (0, 0))]
    args += [kc, vct, lam_vec, g]
    return pl.pallas_call(
        functools.partial(_diff_attn_kernel, n_lat=n_lat, lc=lc, lambda_init=lambda_init),
        out_shape=jax.ShapeDtypeStruct((bsz, tq_total, DIFF_HEADS * DIFF_V_DIM), BF16),
        grid=(bsz, DIFF_HEADS // ATTN_HEADS_PER_STEP, tq_total // tq),
        in_specs=in_specs,
        out_specs=qspec,
        compiler_params=_params(("parallel", "parallel", "parallel")),
        name="diff_attention",
    )(*args)


def _win_attn_kernel(*refs, has_window, lc, tq):
    hd = WIN_HEAD_DIM
    if has_window:
        n_blk = tq // Q_BLOCK + 2
        q_ref = refs[0]
        k_refs, v_refs = refs[1:1 + n_blk], refs[1 + n_blk:1 + 2 * n_blk]
        kc_ref, vc_ref, sink_ref, o_ref = refs[1 + 2 * n_blk:]
        kw = jnp.concatenate([ref[...] for ref in k_refs] + [kc_ref[...]], axis=0)
        vt = jnp.concatenate([ref[...] for ref in v_refs] + [vc_ref[...]], axis=1)
        span = tq + 2 * WINDOW
        nk = span + lc
        i = pl.program_id(1)
        last = pl.num_programs(1) - 1
        c = lax.broadcasted_iota(jnp.int32, (span, tq), 0)
        r = lax.broadcasted_iota(jnp.int32, (span, tq), 1)
        lo_valid = jnp.where(i == 0, WINDOW, 0)
        hi_valid = jnp.where(i == last, WINDOW + tq, span)
        as_u32 = lambda v: lax.bitcast_convert_type(v, jnp.uint32)
        band = as_u32(c - r) <= jnp.uint32(2 * WINDOW)
        inside = as_u32(c - lo_valid) < as_u32(hi_valid - lo_valid)
        win_bias = jnp.where(band, jnp.where(inside, 0.0, NEG_BIG), NEG_BIG)
        bias = jnp.concatenate([win_bias, jnp.zeros((lc, tq), F32)], axis=0)
        bias = jnp.concatenate([bias, bias], axis=1)
    else:
        q_ref, kc_ref, vc_ref, sink_ref, o_ref = refs
        kw, vt = kc_ref[...], vc_ref[...]
        nk = lc
        bias = None
    lane = lax.broadcasted_iota(jnp.int32, (nk, LANES), 1)
    zero = jnp.zeros((nk, LANES), BF16)
    ones = jnp.ones((BF16_SUBLANES, nk), BF16)
    units = [(kh, p) for kh in range(WIN_KV_HEADS) for p in range(2)]
    heads = {(kh, p): [kh * WIN_GROUP + 2 * j + p for j in range(2)] for kh, p in units}

    def scores(kh, p):
        ksrc = kw[:, (0 if p == kh else 1) * LANES:(1 if p == kh else 2) * LANES]
        kmat = jnp.where((lane < hd) if p == 0 else (lane >= hd), ksrc, zero)
        qcat = jnp.concatenate([q_ref[:, (h // 2) * LANES:(h // 2 + 1) * LANES] for h in heads[kh, p]], axis=0)
        return _dot_nt(kmat, qcat)

    out_t = {}
    s_next = scores(*units[0])
    for idx, (kh, p) in enumerate(units):
        s = s_next if bias is None else s_next + bias
        if idx + 1 < len(units):
            s_next = scores(*units[idx + 1])
        sink = jnp.concatenate([jnp.full((1, tq), sink_ref[h] * LOG2_E, F32) for h in heads[kh, p]], axis=1)
        m = jnp.maximum(jnp.max(s, axis=0, keepdims=True), sink)
        vt_ext = jnp.concatenate([vt[kh * hd:(kh + 1) * hd, :], ones], axis=0)
        acc = _dot(vt_ext, jnp.exp2(s - m).astype(BF16))
        denom = acc[hd:hd + 1, :] + jnp.exp2(sink - m)
        out_t[kh, p] = acc[:hd, :] * (1.0 / denom)
    for kh in range(WIN_KV_HEADS):
        for j in range(2):
            pair = kh * 2 + j
            pair_t = jnp.concatenate([out_t[kh, p][:, j * tq:(j + 1) * tq] for p in range(2)], axis=0)
            o_ref[:, pair * LANES:(pair + 1) * LANES] = pair_t.T.astype(BF16)


def _win_attn(q, k, vt, kc, vct, sink, has_window):
    bsz, tq_total, width = q.shape
    lc = kc.shape[1]
    tq = min(WIN_TQ, tq_total)
    assert tq_total % tq == 0 and tq % Q_BLOCK == 0
    nq = tq_total // tq
    per = tq // Q_BLOCK
    n_kblk = tq_total // Q_BLOCK
    qspec = pl.BlockSpec((None, tq, width), lambda b, i: (b, i, 0))
    in_specs, args = [qspec], [q]
    if has_window:
        def kblock(j):
            return lambda b, i: (b, jnp.clip(i * per + j - 1, 0, n_kblk - 1))
        blk = [kblock(j) for j in range(per + 2)]
        in_specs += [pl.BlockSpec((None, Q_BLOCK, 2 * LANES), lambda b, i, f=f: (*f(b, i), 0)) for f in blk]
        in_specs += [pl.BlockSpec((None, LANES, Q_BLOCK), lambda b, i, f=f: (f(b, i)[0], 0, f(b, i)[1])) for f in blk]
        args += [k] * len(blk) + [vt] * len(blk)
    in_specs += [pl.BlockSpec((None, lc, 2 * LANES), lambda b, i: (b, 0, 0)),
                 pl.BlockSpec((None, LANES, lc), lambda b, i: (b, 0, 0)),
                 pl.BlockSpec(memory_space=pltpu.SMEM)]
    args += [kc, vct, sink]
    return pl.pallas_call(
        functools.partial(_win_attn_kernel, has_window=has_window, lc=lc, tq=tq),
        out_shape=jax.ShapeDtypeStruct((bsz, tq_total, width), BF16),
        grid=(bsz, nq),
        in_specs=in_specs,
        out_specs=qspec,
        compiler_params=_params(("parallel", "parallel")),
        name="window_attention",
    )(*args)


_REC_COLS = (("mq", 512, BF16), ("mk", 512, BF16), ("mv", 512, BF16), ("mo", 512, BF16),
             ("rq", 256, BF16), ("rk", 256, BF16), ("rv", 512, BF16), ("rg", 512, BF16),
             ("gi", 128, F32), ("gf", 128, F32))
_N_DIR_HEADS = 2 * MLSTM_HEADS


def _rec_proj_kernel(x_ref, mod_ref, w_ref, gate_b_ref, *outs):
    x = x_ref[...]
    shift, scale = mod_ref[0:1, :], mod_ref[1:2, :]
    h = (x * (1.0 + scale) + shift).astype(BF16)
    lo = 0
    for (name, width, dtype), o_ref in zip(_REC_COLS, outs):
        p = _dot(h, w_ref[:, lo:lo + width])
        lo += width
        if name == "mk":
            p = p * (MLSTM_HEAD_DIM ** -0.5)
        elif name == "rk":
            p = p * (RET_QK_DIM ** -0.5)
        elif name == "gi":
            p = p + gate_b_ref[0:1, :]
        elif name == "gf":
            p = _log_sigmoid(p + gate_b_ref[1:2, :])
        o_ref[...] = p.astype(dtype)


def _rec_proj(x2, mod4, tokens_per_row, w, gate_b):
    n = x2.shape[0]
    tm = _token_tile(n, tokens_per_row)
    tile = lambda width: pl.BlockSpec((tm, width), lambda i: (i, 0))
    return pl.pallas_call(
        _rec_proj_kernel,
        out_shape=[jax.ShapeDtypeStruct((n, width), dtype) for _, width, dtype in _REC_COLS],
        grid=(n // tm,),
        in_specs=[tile(D_MODEL), _mod_spec(1, tokens_per_row // tm), _resident(w.shape), _resident((2, LANES))],
        out_specs=[tile(width) for _, width, _ in _REC_COLS],
        compiler_params=_params(("parallel",)),
        name="rec_in_proj",
    )(x2, mod4, w, gate_b)


def _scan_specs(ncc, ncl, width, rows, nb):
    ctx_f = pl.BlockSpec((nb, rows, width), lambda b, s: (b, jnp.minimum(s, ncc - 1), 0))
    ctx_b = pl.BlockSpec((nb, rows, width), lambda b, s: (b, jnp.maximum(ncc - 1 - s, 0), 0))
    lat_f = pl.BlockSpec((nb, rows, width), lambda b, s: (b, jnp.maximum(s - ncc, 0), 0))
    lat_b = pl.BlockSpec((nb, rows, width), lambda b, s: (b, jnp.minimum(ncl - 1 - (s - ncc), ncl - 1), 0))
    return ctx_f, ctx_b, lat_f, lat_b


def _tri_masks():
    r = lax.broadcasted_iota(jnp.int32, (CHUNK, CHUNK), 0)
    c = lax.broadcasted_iota(jnp.int32, (CHUNK, CHUNK), 1)
    return r, c


def _head_cols(ref, h):
    return ref[:, h * LANES:(h + 1) * LANES]


def _mlstm_step(srcs, masks, tri, c_ref, m_ref, with_output):
    units = [(d, h) for d in range(len(srcs)) for h in range(MLSTM_HEADS)]
    dv = MLSTM_HEAD_DIM
    ones = jnp.ones((CHUNK, LANES), BF16)
    v_ext = {(d, h): jnp.concatenate([_head_cols(srcs[d][2], h), ones], axis=1) for d, h in units}
    gate = []
    for d in range(len(srcs)):
        gi, gf = srcs[d][3][...], srcs[d][4][...]
        g_hi, g_rest = gf.astype(BF16), gf - gf.astype(BF16).astype(F32)
        g_mid, g_lo = _split_bf16(g_rest)
        tri_d = tri[d % 2].astype(BF16)
        cum = _dot(tri_d, g_hi) + (_dot(tri_d, g_mid) + _dot(tri_d, g_lo))
        b_end = cum[CHUNK - 1:CHUNK, :] if d % 2 == 0 else cum[0:1, :]
        m_prev = m_ref[d:d + 1, :]
        w_end = b_end - cum + gi
        m_new = jnp.maximum(b_end + m_prev, jnp.max(w_end, axis=0, keepdims=True))
        decay = jnp.exp(b_end + m_prev - m_new)
        w = jnp.exp(w_end - m_new)
        m_ref[d:d + 1, :] = m_new
        log_inter = cum + m_prev if with_output else None
        key_term = (gi - cum).T if with_output else None
        gate.append((cum, decay, w, log_inter, key_term))
    if with_output:
        qk, qcn, log_d, li, rmax, m_t, sm, w_inter, pv = {}, {}, {}, {}, {}, {}, {}, {}, {}
        for d, h in units:
            q = _head_cols(srcs[d][0], h)
            qk[d, h] = _dot_nt(q, _head_cols(srcs[d][1], h))
            qcn[d, h] = _dot(q, c_ref[d * MLSTM_HEADS + h].astype(BF16))
        for d, h in units:
            lane = (d % 2) * MLSTM_HEADS + h
            cum, _, _, log_inter, key_term = gate[d]
            log_d[d, h] = jnp.where(masks[d % 2], cum[:, lane:lane + 1] + key_term[lane:lane + 1, :], NEG_BIG)
            li[d, h] = log_inter[:, lane:lane + 1]
        for u in units:
            rmax[u] = jnp.max(log_d[u], axis=1, keepdims=True)
        for u in units:
            m_t[u] = jnp.maximum(li[u], rmax[u])
            sm[u] = (qk[u] * jnp.exp(log_d[u] - m_t[u])).astype(BF16)
            w_inter[u] = jnp.exp(li[u] - m_t[u])
        for u in units:
            pv[u] = _dot(sm[u], v_ext[u])
        for d, h in units:
            u = (d, h)
            both = pv[u] + w_inter[u] * qcn[u]
            den = jnp.maximum(jnp.abs(both[:, dv:]), jnp.exp(-m_t[u]))
            srcs[d][5][:, h * LANES:(h + 1) * LANES] = (both[:, :dv] * (1.0 / den)).astype(BF16)
    for d, h in units:
        row, lane = d * MLSTM_HEADS + h, (d % 2) * MLSTM_HEADS + h
        _, decay, w, _, _ = gate[d]
        kw = (_head_cols(srcs[d][1], h).astype(F32) * w[:, lane:lane + 1]).astype(BF16)
        c_ref[row] = decay[:, lane:lane + 1] * c_ref[row] + _dot_tn(kw, v_ext[d, h])


def _ret_tables(logit_ref, intra_ref, to_end_ref, from_start_ref):
    log_gamma = _log_sigmoid(logit_ref[...])
    r, c = _tri_masks()
    rf, cf = r.astype(F32), c.astype(F32)
    for d in range(2):
        for h in range(RET_HEADS):
            row = d * RET_HEADS + h
            lg = log_gamma[d:d + 1, h:h + 1]
            rel = (rf - cf) if d == 0 else (cf - rf)
            intra_ref[row] = jnp.where(rel >= 0.0, jnp.exp(lg * jnp.maximum(rel, 0.0)), 0.0)
            to_end_ref[row] = jnp.exp(lg * ((CHUNK - 1.0 - rf) if d == 0 else rf))
            from_start_ref[row] = jnp.exp(lg * ((rf + 1.0) if d == 0 else (CHUNK - rf)))


def _ret_step(srcs, s_ref, intra_ref, to_end_ref, from_start_ref, with_output):
    units = [(d, h) for d in range(len(srcs)) for h in range(RET_HEADS)]
    lane = lax.broadcasted_iota(jnp.int32, (CHUNK, LANES), 1)

    def k_head(d, h):
        pair = _head_cols(srcs[d][1], h // 2)
        keep = (lane < RET_QK_DIM) if h % 2 == 0 else (lane >= RET_QK_DIM)
        return jnp.where(keep, pair, jnp.zeros_like(pair))

    ks = {u: k_head(*u) for u in units}
    qk, qs = {}, {}
    if with_output:
        for d, h in units:
            q = _head_cols(srcs[d][0], h // 2)
            qk[d, h] = _dot_nt(q, ks[d, h])
            qs[d, h] = _dot(q, s_ref[d * RET_HEADS + h].astype(BF16))
        for d, h in units:
            tab = (d % 2) * RET_HEADS + h
            sc = (qk[d, h] * intra_ref[tab]).astype(BF16)
            out = _dot(sc, _head_cols(srcs[d][2], h)) + from_start_ref[tab] * qs[d, h]
            srcs[d][3][:, h * LANES:(h + 1) * LANES] = out.astype(BF16)
    for d, h in units:
        row, tab = d * RET_HEADS + h, (d % 2) * RET_HEADS + h
        kd = (ks[d, h].astype(F32) * to_end_ref[tab]).astype(BF16)
        chunk_decay = from_start_ref[tab, CHUNK - 1:CHUNK, :] if d % 2 == 0 else from_start_ref[tab, 0:1, :]
        s_ref[row] = chunk_decay * s_ref[row] + _dot_tn(kd, _head_cols(srcs[d][2], h))


def _rec_scan_kernel(mkc_f, mvc_f, gic_f, gfc_f, mkc_b, mvc_b, gic_b, gfc_b,
                     mq_f, mk_f, mv_f, gi_f, gf_f, mq_b, mk_b, mv_b, gi_b, gf_b,
                     rkc_f, rvc_f, rkc_b, rvc_b, rq_f, rk_f, rv_f, rq_b, rk_b, rv_b, logit_ref,
                     mo_f, mo_b, ro_f, ro_b,
                     c_ref, m_ref, s_ref, intra_ref, to_end_ref, from_start_ref, *, ncc):
    s = pl.program_id(1)

    @pl.when(s == 0)
    def _():
        c_ref[...] = jnp.zeros_like(c_ref)
        m_ref[...] = jnp.zeros_like(m_ref)
        s_ref[...] = jnp.zeros_like(s_ref)
        _ret_tables(logit_ref, intra_ref, to_end_ref, from_start_ref)

    r, c = _tri_masks()
    masks = (c <= r, c >= r)
    tri = (jnp.where(masks[0], 1.0, 0.0), jnp.where(masks[1], 1.0, 0.0))

    def run(m_f, m_b, r_f, r_b, with_output):
        n_batch = mkc_f.shape[0]
        for sub in range(SCAN_CHUNKS):
            lo = (sub * CHUNK, (SCAN_CHUNKS - 1 - sub) * CHUNK)
            view = lambda refs, bb, d: tuple(None if ref is None else _Rows(ref, bb, lo[d]) for ref in refs)
            m_srcs = [view(refs, bb, d) for bb in range(n_batch) for d, refs in enumerate((m_f, m_b))]
            r_srcs = [view(refs, bb, d) for bb in range(n_batch) for d, refs in enumerate((r_f, r_b))]
            _mlstm_step(m_srcs, masks, tri, c_ref, m_ref, with_output)
            _ret_step(r_srcs, s_ref, intra_ref, to_end_ref, from_start_ref, with_output)

    @pl.when(s < ncc)
    def _():
        run((None, mkc_f, mvc_f, gic_f, gfc_f, None), (None, mkc_b, mvc_b, gic_b, gfc_b, None),
            (None, rkc_f, rvc_f, None), (None, rkc_b, rvc_b, None), False)

    @pl.when(s >= ncc)
    def _():
        run((mq_f, mk_f, mv_f, gi_f, gf_f, mo_f), (mq_b, mk_b, mv_b, gi_b, gf_b, mo_b),
            (rq_f, rk_f, rv_f, ro_f), (rq_b, rk_b, rv_b, ro_b), True)


class _Rows:
    def __init__(self, ref, bb, lo):
        self.ref, self.bb, self.rows = ref, bb, slice(lo, lo + CHUNK)

    def __getitem__(self, idx):
        return self.ref[self.bb, self.rows, :] if idx is Ellipsis else self.ref[self.bb, self.rows, idx[1]]

    def __setitem__(self, idx, value):
        self.ref[self.bb, self.rows, idx[1]] = value


def _rec_scan(mq, mk, mv, gi, gf, rq, rk, rv, mkc, mvc, gic, gfc, rkc, rvc, logit):
    bsz, t, width = mq.shape
    rows = CHUNK * SCAN_CHUNKS
    assert t % rows == 0 and mkc.shape[1] % rows == 0
    ncl, ncc = t // rows, mkc.shape[1] // rows
    nb = SCAN_BATCH if bsz % SCAN_BATCH == 0 else 1
    cf, cb, lf, lb = _scan_specs(ncc, ncl, width, rows, nb)
    gcf, gcb, glf, glb = _scan_specs(ncc, ncl, LANES, rows, nb)
    kcf, kcb, klf, klb = _scan_specs(ncc, ncl, rq.shape[2], rows, nb)
    return pl.pallas_call(
        functools.partial(_rec_scan_kernel, ncc=ncc),
        out_shape=[jax.ShapeDtypeStruct((bsz, t, width), BF16)] * 4,
        grid=(bsz // nb, ncc + ncl),
        in_specs=[cf, cf, gcf, gcf, cb, cb, gcb, gcb, lf, lf, lf, glf, glf, lb, lb, lb, glb, glb,
                  kcf, cf, kcb, cb, klf, klf, lf, klb, klb, lb, pl.BlockSpec((8, LANES), lambda b, s: (0, 0))],
        out_specs=[lf, lb, lf, lb],
        scratch_shapes=[pltpu.VMEM((nb * _N_DIR_HEADS, MLSTM_HEAD_DIM, 2 * MLSTM_HEAD_DIM), F32),
                        pltpu.VMEM((8, LANES), F32),
                        pltpu.VMEM((nb * 2 * RET_HEADS, LANES, RET_V_DIM), F32)]
        + [pltpu.VMEM((2 * RET_HEADS, CHUNK, LANES), F32)] * 3,
        compiler_params=_params(("parallel", "arbitrary")),
        name="rec_scan",
    )(mkc, mvc, gic, gfc, mkc, mvc, gic, gfc, mq, mk, mv, gi, gf, mq, mk, mv, gi, gf,
      rkc, rvc, rkc, rvc, rq, rk, rv, rq, rk, rv, logit)


def _rec_out_kernel(x_ref, mod_ref, mf_ref, mb_ref, rf_ref, rb_ref, o_ref_in, rg_ref, mg_ref, rng_ref, rnb_ref,
                    w_ref, g_ref, b_ref, out_ref):
    hm = mf_ref[...].astype(F32) + mb_ref[...].astype(F32)
    hr = rf_ref[...].astype(F32) + rb_ref[...].astype(F32)
    o_gate = jax.nn.sigmoid(o_ref_in[...].astype(F32))
    rg = rg_ref[...].astype(F32)
    r_gate = rg * jax.nn.sigmoid(rg)
    parts_m, parts_r = [], []
    for h in range(MLSTM_HEADS):
        cols = slice(h * LANES, (h + 1) * LANES)
        parts_m.append(_layer_norm(hm[:, cols], mg_ref[:, cols], None) * o_gate[:, cols])
        parts_r.append(_layer_norm(hr[:, cols], rng_ref[:, cols], rnb_ref[:, cols]) * r_gate[:, cols])
    width = MLSTM_HEADS * MLSTM_HEAD_DIM
    y = (_dot(jnp.concatenate(parts_m, axis=1).astype(BF16), w_ref[0:width, :])
         + _dot(jnp.concatenate(parts_r, axis=1).astype(BF16), w_ref[width:, :]))
    z = ALPHA * x_ref[...] + mod_ref[2:3, :] * y
    out_ref[...] = _layer_norm(z, g_ref[...], b_ref[...])


def _rec_out(x2, mod4, tokens_per_row, mf, mb, rf, rb, o, rg, mg, rng, rnb, w, g, b):
    n = x2.shape[0]
    tm = _token_tile(n, tokens_per_row)
    tile = lambda width: pl.BlockSpec((tm, width), lambda i: (i, 0))
    half = tile(512)
    return pl.pallas_call(
        _rec_out_kernel,
        out_shape=jax.ShapeDtypeStruct((n, D_MODEL), F32),
        grid=(n // tm,),
        in_specs=[tile(D_MODEL), _mod_spec(1, tokens_per_row // tm), half, half, half, half, half, half,
                  _resident((1, 512)), _resident((1, 512)), _resident((1, 512)),
                  _resident(w.shape), _resident((1, D_MODEL)), _resident((1, D_MODEL))],
        out_specs=tile(D_MODEL),
        compiler_params=_params(("parallel",)),
        name="rec_out_proj",
    )(x2, mod4, mf, mb, rf, rb, o, rg, mg, rng, rnb, w, g, b)


def _attn_weights(w_in, w_out):
    aq, ak, av, bq, bk, bv = jnp.split(w_in, [512, 1024, 1536, 2048, 2176], axis=1)
    bk_swapped = jnp.concatenate([bk[:, WIN_HEAD_DIM:], bk[:, :WIN_HEAD_DIM]], axis=1)
    w = jnp.concatenate([aq, ak, av, bq, bk, bk_swapped, bv], axis=1)
    wa, wb = w_out[:DIFF_HEADS * DIFF_V_DIM], w_out[DIFF_HEADS * DIFF_V_DIM:]
    return w.astype(BF16), wa.astype(BF16), wb.astype(BF16)


def _rec_weights(w_in):
    mq, mk, mv, mo, mg, rq, rk, rv, rg = jnp.split(w_in, [512, 1024, 1536, 2048, 2064, 2320, 2576, 3088], axis=1)
    gi, gf = _split_gates(mg)
    w = jnp.concatenate([mq, mk, mv, mo, rq, rk, rv, rg, gi, gf], axis=1)
    return w.astype(BF16)


def _split_gates(g):
    g = g.reshape(*g.shape[:-1], 2, 2, MLSTM_HEADS)
    pad = [(0, 0)] * (g.ndim - 3) + [(0, LANES - _N_DIR_HEADS)]
    return tuple(jnp.pad(g[..., io, :].reshape(*g.shape[:-3], _N_DIR_HEADS), pad) for io in range(2))


def _rope_tables(t):
    rows = t // GRID_W
    row = jnp.repeat(jnp.arange(rows), GRID_W)
    col = jnp.tile(jnp.arange(GRID_W), rows)
    inv = ROPE_BASE ** (-jnp.arange(ROPE_FREQS, dtype=F32) / ROPE_FREQS)
    ang_r, ang_c = row[:, None] * inv, col[:, None] * inv
    cos = jnp.concatenate([jnp.cos(ang_r)] * 2 + [jnp.cos(ang_c)] * 2, axis=1)
    sin = jnp.concatenate([-jnp.sin(ang_r), jnp.sin(ang_r), -jnp.sin(ang_c), jnp.sin(ang_c)], axis=1)
    return jnp.tile(cos, (1, 2)), jnp.tile(sin, (1, 2))


def kernel(x, c, ctx, c_ctx, ada_w, ada_b, ln_g, ln_b, ffn_w_in, ffn_w_out, attn_w_in, attn_w_out,
           diff_lambda, diff_norm_g, sink_logits, rec_w_in, rec_w_out, mlstm_gate_b, mlstm_norm_g,
           ret_decay_logit, ret_norm_g, ret_norm_b):
    bsz, t, d = x.shape
    lc = ctx.shape[1]
    assert d == D_MODEL and t % ATTN_TK == 0 and lc % CHUNK == 0
    out_dtype = x.dtype
    x2 = x.reshape(bsz * t, d).astype(F32)
    c2 = ctx.reshape(bsz * lc, d).astype(F32)
    mod_rows = 16
    cc = jnp.zeros((mod_rows, d), F32).at[:bsz].set(c).at[bsz].set(c_ctx)
    row2 = lambda v: v.reshape(1, -1).astype(F32)
    ada_b3 = ada_b.reshape(DEPTH, 1, -1)
    ffn_w_in16, ffn_w_out16 = ffn_w_in.astype(BF16), ffn_w_out.astype(BF16)

    for l in range(DEPTH):
        last = l == DEPTH - 1
        i = l // 2
        mod = _modulation(cc, ada_w, ada_b3, l).reshape(mod_rows, 3, 3, d)
        mod_x, mod_c = mod[:bsz], mod[bsz:bsz + 1]
        ffn = lambda z, m, sub, tpr, j: _ffn(z, m, sub, tpr, ffn_w_in16, ffn_w_out16, l, j,
                                             row2(ln_g[l, sub]), row2(ln_b[l, sub]))
        x2 = ffn(x2, mod_x, 0, t, 0)
        c2 = ffn(c2, mod_c, 0, bsz * lc, 0)
        g1, b1 = row2(ln_g[l, 1]), row2(ln_b[l, 1])
        if l % 2 == 0:
            lambda_init = 0.8 - 0.6 * math.exp(-0.3 * l)
            w, wa, wb = _attn_weights(attn_w_in[i], attn_w_out[i])
            aq, ak, av, bq, bk, bv = [a.reshape(bsz, t, -1) for a in _attn_proj(x2, mod_x, t, w, _rope_tables(t))]
            aqc, akc, avc, bqc, bkc, bvc = [a.reshape(bsz, lc, -1) for a in _attn_proj(c2, mod_c, bsz * lc, w, None)]
            avt, avct = jnp.swapaxes(av, 1, 2), jnp.swapaxes(avc, 1, 2)
            bvt, bvct = jnp.swapaxes(bv, 1, 2), jnp.swapaxes(bvc, 1, 2)
            lam_vec, sub_g = diff_lambda[i].astype(F32), row2(diff_norm_g[i])
            sink = sink_logits[i].astype(F32)
            a_x = _diff_attn(aq, ak, avt, akc, avct, lam_vec, sub_g, lambda_init)
            b_x = _win_attn(bq, bk, bvt, bkc, bvct, sink, True)
            flat = lambda a: a.reshape(-1, a.shape[-1])
            if not last:
                a_c = _diff_attn(aqc, None, None, akc, avct, lam_vec, sub_g, lambda_init)
                b_c = _win_attn(bqc, None, None, bkc, bvct, sink, False)
                c2_mix = _attn_out(c2, mod_c, bsz * lc, flat(a_c), flat(b_c), wa, wb, g1, b1)
            x2 = _attn_out(x2, mod_x, t, flat(a_x), flat(b_x), wa, wb, g1, b1)
        else:
            w = _rec_weights(rec_w_in[i])
            gate_b = jnp.stack(_split_gates(mlstm_gate_b[i].reshape(-1).astype(F32)))
            px = [a.reshape(bsz, t, -1) for a in _rec_proj(x2, mod_x, t, w, gate_b)]
            pc = [a.reshape(bsz, lc, -1) for a in _rec_proj(c2, mod_c, bsz * lc, w, gate_b)]
            mq, mk, mv, mo, rq, rk, rv, rg, gi, gf = px
            _, mkc, mvc, _, _, rkc, rvc, _, gic, gfc = pc
            assert last
            logit = jnp.zeros((8, LANES), F32).at[:2, :RET_HEADS].set(ret_decay_logit[i].astype(F32))
            m_f, m_b, r_f, r_b = _rec_scan(mq, mk, mv, gi, gf, rq, rk, rv, mkc, mvc, gic, gfc, rkc, rvc, logit)
            flat = lambda a: a.reshape(-1, a.shape[-1])
            x2 = _rec_out(x2, mod_x, t, flat(m_f), flat(m_b), flat(r_f), flat(r_b), flat(mo), flat(rg),
                          row2(mlstm_norm_g[i]), row2(ret_norm_g[i]), row2(ret_norm_b[i]),
                          rec_w_out[i].astype(BF16), g1, b1)
        x2 = ffn(x2, mod_x, 2, t, 1)
        if not last:
            c2 = ffn(c2_mix, mod_c, 2, bsz * lc, 1)
    return x2.reshape(bsz, t, d).astype(out_dtype)
```
